```python
import jax, jax.numpy as jnp
from jax import lax
import numpy as np

D_MODEL = 1024
BATCH = 8
SEQ = 2048
DEPTH = 2
DEC_BATCH = 16
DEC_SEQ = 16
PAST_LEN = 1024

CHUNK = 64
MIX_WIDTH = D_MODEL
HEAD_DIM = 64
FOX_WIDTH = MIX_WIDTH // 2
FOX_HEADS = FOX_WIDTH // HEAD_DIM
SGU_WIDTH = MIX_WIDTH // 4
SGU_GROUPS = 4
SGU_GROUP_DIM = SGU_WIDTH // SGU_GROUPS
GMLP_CHUNK = 128
MEM_WIDTH = MIX_WIDTH // 4
MEM_HEADS = 4
MEM_HEAD_DIM = MEM_WIDTH // MEM_HEADS
N_MEM = 256
FFN_DIM = 2816
CONV_WIDTH = 3
Q_BLOCK = 128
RMS_EPS = 1e-6
NEG_INF = -1e30
IN_COLS = 3 * FOX_WIDTH + FOX_HEADS + 2 * SGU_WIDTH + MEM_WIDTH
SPLITS = (FOX_WIDTH, 2 * FOX_WIDTH, 3 * FOX_WIDTH, 3 * FOX_WIDTH + FOX_HEADS,
          3 * FOX_WIDTH + FOX_HEADS + 2 * SGU_WIDTH)

kernel_name = 'hybrid_fox_gmlp_memory_stream'


def rmsnorm(x, g):
    xf = x.astype(jnp.float32)
    y = xf * lax.rsqrt(jnp.mean(xf * xf, axis=-1, keepdims=True) + RMS_EPS)
    return (y * g.astype(jnp.float32)).astype(x.dtype)


def fox_attend(q, k, v, c_q, c_k, pos_q, pos_k):
    s = jnp.einsum('bqhd,bkhd->bhqk', q, k, preferred_element_type=jnp.float32) * (HEAD_DIM ** -0.5)
    s = s + jnp.swapaxes(c_q, 1, 2)[:, :, :, None] - jnp.swapaxes(c_k, 1, 2)[:, :, None, :]
    mask = pos_k[None, :] <= pos_q[:, None]
    s = jnp.where(mask[None, None], s, NEG_INF)
    p = jax.nn.softmax(s, axis=-1)
    return jnp.einsum('bhqk,bkhd->bqhd', p.astype(v.dtype), v)


def fox_prompt(q, k, v, c):
    B, S, H, Dh = q.shape
    nb = S // Q_BLOCK
    pos = jnp.arange(S)
    qb = q.reshape(B, nb, Q_BLOCK, H, Dh).transpose(1, 0, 2, 3, 4)
    cb = c.reshape(B, nb, Q_BLOCK, H).transpose(1, 0, 2, 3)
    pb = pos.reshape(nb, Q_BLOCK)
    out = lax.map(lambda blk: fox_attend(blk[0], k, v, blk[1], c, blk[2], pos), (qb, cb, pb))
    return out.transpose(1, 0, 2, 3, 4).reshape(B, S, H * Dh)


def spatial_gating(gm, w_s, b_s, g_sgu, L):
    z = jax.nn.gelu(gm)
    u, vv = jnp.split(z, 2, axis=-1)
    vv = rmsnorm(vv, g_sgu)
    B, T, _ = vv.shape
    idx = jnp.arange(GMLP_CHUNK)
    mask = (idx[None, :] // CHUNK) <= (idx[:, None] // CHUNK)
    w = jnp.where(mask[None], w_s, 0.0)[:, :L, :L]
    vc = vv.reshape(B, T // L, L, SGU_GROUPS, SGU_GROUP_DIM)
    mixed = jnp.einsum('gij,bcjgd->bcigd', w, vc) + b_s[:, :L].T[None, None, :, :, None]
    return u * mixed.reshape(B, T, SGU_WIDTH), vv


def memory_kv(mem, g_mem, w_mem_kv):
    B, N, _ = mem.shape
    kv = rmsnorm(mem, g_mem) @ w_mem_kv
    mk, mv = jnp.split(kv, 2, axis=-1)
    return (mk.reshape(B, N, MEM_HEADS, MEM_HEAD_DIM), mv.reshape(B, N, MEM_HEADS, MEM_HEAD_DIM))


def memory_attend(q, k, v):
    s = jnp.einsum('bqhd,bkhd->bhqk', q, k, preferred_element_type=jnp.float32) * (MEM_HEAD_DIM ** -0.5)
    p = jax.nn.softmax(s, axis=-1)
    return jnp.einsum('bhqk,bkhd->bqhd', p.astype(v.dtype), v)


def causal_dwconv(a_ext, w, b):
    T = a_ext.shape[1] - (CONV_WIDTH - 1)
    out = b + w[0] * a_ext[:, 0:T]
    for i in range(1, CONV_WIDTH):
        out = out + w[i] * a_ext[:, i:i + T]
    return out


def trunk_layer(x, mem_k, mem_v, hist, p):
    (g_pre_mix, w_in, b_forget, w_s, b_s, g_sgu, g_group_out, w_out, g_post_mix,
     g_pre_ffn, w_up, w_dw, b_dw, w_down, g_post_ffn) = p
    B, T, _ = x.shape
    h = rmsnorm(x, g_pre_mix)
    q, k, v, fg, gm, qm = jnp.split(h @ w_in, SPLITS, axis=-1)
    q = q.reshape(B, T, FOX_HEADS, HEAD_DIM)
    k = k.reshape(B, T, FOX_HEADS, HEAD_DIM)
    v = v.reshape(B, T, FOX_HEADS, HEAD_DIM)
    logf = jax.nn.log_sigmoid((fg + b_forget).astype(jnp.float32))
    if hist is None:
        c = jnp.cumsum(logf, axis=1)
        fox = fox_prompt(q, k, v, c)
        L = GMLP_CHUNK
        a_hist = jnp.zeros((B, CONV_WIDTH - 1, FFN_DIM), x.dtype)
    else:
        hk, hv, hlogf, a_hist = hist
        P = hk.shape[1]
        c = jnp.cumsum(jnp.concatenate([hlogf.astype(jnp.float32), logf], axis=1), axis=1)
        fox = fox_attend(q, jnp.concatenate([hk, k], axis=1), jnp.concatenate([hv, v], axis=1),
                         c[:, P:], c, P + jnp.arange(T), jnp.arange(P + T)).reshape(B, T, FOX_WIDTH)
        L = T
    sgu, v_rows = spatial_gating(gm, w_s, b_s, g_sgu, L)
    mem = memory_attend(qm.reshape(B, T, MEM_HEADS, MEM_HEAD_DIM), mem_k, mem_v).reshape(B, T, MEM_WIDTH)
    mixed = jnp.concatenate([
        rmsnorm(fox, g_group_out[:FOX_WIDTH]),
        rmsnorm(sgu, g_group_out[FOX_WIDTH:FOX_WIDTH + SGU_WIDTH]),
        rmsnorm(mem, g_group_out[FOX_WIDTH + SGU_WIDTH:])], axis=-1)
    x = x + rmsnorm(mixed @ w_out, g_post_mix)
    h2 = rmsnorm(x, g_pre_ffn)
    a, lin = jnp.split(h2 @ w_up, 2, axis=-1)
    a_ext = jnp.concatenate([a_hist.astype(a.dtype), a], axis=1)
    ffn = (jax.nn.silu(causal_dwconv(a_ext, w_dw, b_dw)) * lin) @ w_down
    x = x + rmsnorm(ffn, g_post_ffn)
    return x, (k, v, logf.astype(x.dtype), v_rows, a_ext[:, -(CONV_WIDTH - 1):])


def setup_inputs(seed: int = 0) -> dict:
    key = jax.random.key(seed)
    ks = jax.random.split(key, 32)
    nrm = lambda i, shape, s=1.0: s * jax.random.normal(ks[i], shape, jnp.float32)
    gain = lambda i, shape: 1.0 + 0.1 * jax.random.normal(ks[i], shape, jnp.float32)
    return {
        'x_prompt': nrm(0, (BATCH, SEQ, D_MODEL)),
        'x_sample': nrm(1, (DEC_BATCH, DEC_SEQ, D_MODEL)),
        'mem_prompt': nrm(2, (BATCH, N_MEM, D_MODEL)),
        'cache_fox_k': nrm(3, (DEPTH, DEC_BATCH, PAST_LEN, FOX_HEADS, HEAD_DIM)),
        'cache_fox_v': nrm(4, (DEPTH, DEC_BATCH, PAST_LEN, FOX_HEADS, HEAD_DIM)),
        'cache_fox_logf': jax.nn.log_sigmoid(3.0 + nrm(5, (DEPTH, DEC_BATCH, PAST_LEN, FOX_HEADS))),
        'cache_mem_k': nrm(6, (DEPTH, DEC_BATCH, N_MEM, MEM_HEADS, MEM_HEAD_DIM)),
        'cache_mem_v': nrm(7, (DEPTH, DEC_BATCH, N_MEM, MEM_HEADS, MEM_HEAD_DIM)),
        'cache_ffn_conv': nrm(8, (DEPTH, DEC_BATCH, CONV_WIDTH - 1, FFN_DIM)),
        'g_pre_mix': gain(9, (DEPTH, D_MODEL)),
        'w_in': nrm(10, (DEPTH, D_MODEL, IN_COLS), D_MODEL ** -0.5),
        'b_forget': 3.0 + nrm(11, (DEPTH, FOX_HEADS), 0.5),
        'w_spatial': nrm(12, (DEPTH, SGU_GROUPS, GMLP_CHUNK, GMLP_CHUNK), 0.5 * GMLP_CHUNK ** -0.5),
        'b_spatial': gain(13, (DEPTH, SGU_GROUPS, GMLP_CHUNK)),
        'g_sgu': gain(14, (DEPTH, SGU_WIDTH)),
        'g_mem': gain(15, (DEPTH, D_MODEL)),
        'w_mem_kv': nrm(16, (DEPTH, D_MODEL, 2 * MEM_WIDTH), D_MODEL ** -0.5),
        'g_group_out': gain(17, (DEPTH, MIX_WIDTH)),
        'w_out': nrm(18, (DEPTH, MIX_WIDTH, D_MODEL), MIX_WIDTH ** -0.5),
        'g_post_mix': gain(19, (DEPTH, D_MODEL)),
        'g_pre_ffn': gain(20, (DEPTH, D_MODEL)),
        'w_up': nrm(21, (DEPTH, D_MODEL, 2 * FFN_DIM), D_MODEL ** -0.5),
        'w_dwconv': nrm(22, (DEPTH, CONV_WIDTH, FFN_DIM), CONV_WIDTH ** -0.5),
        'b_dwconv': nrm(23, (DEPTH, FFN_DIM), 0.02),
        'w_down': nrm(24, (DEPTH, FFN_DIM, D_MODEL), FFN_DIM ** -0.5),
        'g_post_ffn': gain(25, (DEPTH, D_MODEL)),
    }


def reference(x_prompt, x_sample, mem_prompt, cache_fox_k, cache_fox_v, cache_fox_logf,
              cache_mem_k, cache_mem_v, cache_ffn_conv, g_pre_mix, w_in, b_forget, w_spatial,
              b_spatial, g_sgu, g_mem, w_mem_kv, g_group_out, w_out, g_post_mix, g_pre_ffn,
              w_up, w_dwconv, b_dwconv, w_down, g_post_ffn):
    yp, ys = x_prompt, x_sample
    pk_l, pv_l, plf_l, pmk_l, pmv_l, pconv_l = [], [], [], [], [], []
    sk_l, sv_l, slf_l, sgv_l, sconv_l = [], [], [], [], []
    for l in range(DEPTH):
        p = (g_pre_mix[l], w_in[l], b_forget[l], w_spatial[l], b_spatial[l], g_sgu[l],
             g_group_out[l], w_out[l], g_post_mix[l], g_pre_ffn[l], w_up[l], w_dwconv[l],
             b_dwconv[l], w_down[l], g_post_ffn[l])
        mk, mv = memory_kv(mem_prompt, g_mem[l], w_mem_kv[l])
        yp, (pk, pv, plf, _, pconv) = trunk_layer(yp, mk, mv, None, p)
        ys, (sk, sv, slf, sgv, sconv) = trunk_layer(
            ys, cache_mem_k[l], cache_mem_v[l],
            (cache_fox_k[l], cache_fox_v[l], cache_fox_logf[l], cache_ffn_conv[l]), p)
        pk_l.append(pk); pv_l.append(pv); plf_l.append(plf)
        pmk_l.append(mk); pmv_l.append(mv); pconv_l.append(pconv)
        sk_l.append(sk); sv_l.append(sv); slf_l.append(slf); sgv_l.append(sgv); sconv_l.append(sconv)
    return (yp, ys,
            jnp.stack(pk_l), jnp.stack(pv_l), jnp.stack(plf_l),
            jnp.stack(pmk_l), jnp.stack(pmv_l), jnp.stack(pconv_l),
            jnp.stack(sk_l), jnp.stack(sv_l), jnp.stack(slf_l),
            jnp.stack(sgv_l), jnp.stack(sconv_l))
```

```python
import functools

import jax
import jax.numpy as jnp
from jax import lax
from jax.experimental import pallas as pl
from jax.experimental.pallas import tpu as pltpu

D_MODEL = 1024
HEAD_DIM = 64
FOX_WIDTH = 512
FOX_HEADS = 8
SGU_WIDTH = 256
SGU_GROUPS = 4
GMLP_CHUNK = 128
CHUNK = 64
MEM_WIDTH = 256
MEM_HEADS = 4
N_MEM = 256
FFN_DIM = 2816
CONV_WIDTH = 3
RMS_EPS = 1e-6
NEG_INF = -1e30
QK_SCALE = HEAD_DIM ** -0.5

LANES = 128
SUBLANES = 8
HEAD_PAIR = LANES // HEAD_DIM
VMEM_LIMIT_BYTES = 56 * 1024 * 1024

IN_PROJ_ROWS = 512
ATTN_Q_ROWS = 256
ATTN_K_ROWS = 256
FFN_ROWS = 1024
FFN_COLS = 256
B_COLS = 896

BF16 = jnp.bfloat16
F32 = jnp.float32


def _rms(x, g):
    y = x * lax.rsqrt(jnp.mean(x * x, axis=-1, keepdims=True) + RMS_EPS)
    return y * g


def _dot(a, b):
    return jnp.dot(a, b, preferred_element_type=F32)


def _dot_nt(a, b):
    return lax.dot_general(a, b, (((1,), (1,)), ((), ())), preferred_element_type=F32)


def _params(*semantics):
    return pltpu.CompilerParams(dimension_semantics=semantics, vmem_limit_bytes=VMEM_LIMIT_BYTES)


def _memkv_kernel(mem_ref, g_ref, w_ref, mk_ref, mv_ref):
    h = _rms(mem_ref[...], g_ref[...]).astype(BF16)
    kv = _dot(h, w_ref[...])
    mk_ref[...] = kv[:, :MEM_WIDTH]
    mv_ref[...] = kv[:, MEM_WIDTH:]


def _memory_kv(mem, g_mem, w_mem_kv_bf):
    depth = g_mem.shape[0]
    batch = mem.shape[0]
    out = jax.ShapeDtypeStruct((depth, batch, N_MEM, MEM_WIDTH), F32)
    return pl.pallas_call(
        _memkv_kernel,
        grid=(depth, batch),
        in_specs=[
            pl.BlockSpec((None, N_MEM, D_MODEL), lambda l, b: (b, 0, 0)),
            pl.BlockSpec((None, 1, D_MODEL), lambda l, b: (l, 0, 0)),
            pl.BlockSpec((None, D_MODEL, 2 * MEM_WIDTH), lambda l, b: (l, 0, 0)),
        ],
        out_specs=[
            pl.BlockSpec((None, None, N_MEM, MEM_WIDTH), lambda l, b: (l, b, 0, 0)),
            pl.BlockSpec((None, None, N_MEM, MEM_WIDTH), lambda l, b: (l, b, 0, 0)),
        ],
        out_shape=[out, out],
        compiler_params=_params("arbitrary", "arbitrary"),
        name="memory_kv",
    )(mem, g_mem.reshape(depth, 1, D_MODEL), w_mem_kv_bf)


def _inproj_kernel(x_ref, gpre_ref, wqkv_ref, wb_ref, bfg_ref, ws_ref, bt_ref, gsgu_ref, gg_ref,
                   mk_ref, mv_ref,
                   q_ref, k_ref, v_ref, kb_ref, vb_ref, lf_ref, rest_ref, vrows_ref):
    tm = x_ref.shape[0]
    n_mem_blocks = mk_ref.shape[0]
    h = _rms(x_ref[...], gpre_ref[...]).astype(BF16)

    yq = _dot(h, wqkv_ref[...])
    q_ref[...] = (yq[:, :FOX_WIDTH] * QK_SCALE).astype(BF16)
    k = yq[:, FOX_WIDTH:2 * FOX_WIDTH]
    v = yq[:, 2 * FOX_WIDTH:]
    k_ref[...] = k
    v_ref[...] = v
    kb_ref[...] = k.astype(BF16)
    vb_ref[...] = v.astype(BF16)

    yb = _dot(h, wb_ref[...])
    fg = yb[:, 2 * SGU_WIDTH + MEM_WIDTH:2 * SGU_WIDTH + MEM_WIDTH + FOX_HEADS]
    lf_ref[...] = jax.nn.log_sigmoid(fg + bfg_ref[...])

    z = jax.nn.gelu(yb[:, :2 * SGU_WIDTH])
    u = z[:, :SGU_WIDTH]
    vv = _rms(z[:, SGU_WIDTH:], gsgu_ref[...])
    vrows_ref[...] = vv
    vvb = vv.astype(BF16)
    wrow = lax.broadcasted_iota(jnp.int32, (GMLP_CHUNK, SGU_GROUPS * GMLP_CHUNK), 0)
    wcol = lax.broadcasted_iota(jnp.int32, (GMLP_CHUNK, SGU_GROUPS * GMLP_CHUNK), 1)
    wmask = ((wcol % GMLP_CHUNK) // CHUNK) <= (wrow // CHUNK)
    wcat = jnp.where(wmask, ws_ref[...], 0.0).astype(BF16)
    glane = lax.broadcasted_iota(jnp.int32, (GMLP_CHUNK, SGU_WIDTH), 1) // (SGU_WIDTH // SGU_GROUPS)
    zero_chunk = jnp.zeros((GMLP_CHUNK, SGU_WIDTH), BF16)
    sgu_parts = []
    for c in range(tm // GMLP_CHUNK):
        vc = vvb[c * GMLP_CHUNK:(c + 1) * GMLP_CHUNK, :]
        rhs = jnp.concatenate([jnp.where(glane == g, vc, zero_chunk) for g in range(SGU_GROUPS)], axis=0)
        mixed = _dot(wcat, rhs) + bt_ref[...]
        sgu_parts.append(u[c * GMLP_CHUNK:(c + 1) * GMLP_CHUNK, :] * mixed)
    sgu = jnp.concatenate(sgu_parts, axis=0)
    rest_ref[:, :SGU_WIDTH] = _rms(sgu, gg_ref[:, :SGU_WIDTH]).astype(BF16)

    qmb = (yb[:, 2 * SGU_WIDTH:2 * SGU_WIDTH + MEM_WIDTH] * QK_SCALE).astype(BF16)
    rows = tm // n_mem_blocks
    hlane = lax.broadcasted_iota(jnp.int32, (rows, MEM_WIDTH), 1) // HEAD_DIM
    zero_q = jnp.zeros((rows, MEM_WIDTH), BF16)
    mem_parts = []
    for bi in range(n_mem_blocks):
        qb = qmb[bi * rows:(bi + 1) * rows, :]
        mkb = mk_ref[bi].astype(BF16)
        mvb = mv_ref[bi].astype(BF16)
        out = jnp.zeros((rows, MEM_WIDTH), F32)
        for hd in range(MEM_HEADS):
            hm = hlane == hd
            s = _dot_nt(jnp.where(hm, qb, zero_q), mkb)
            e = jnp.exp(s - jnp.max(s, axis=-1, keepdims=True))
            o = _dot(e.astype(BF16), mvb) / jnp.sum(e, axis=-1, keepdims=True)
            out = jnp.where(hm, o, out)
        mem_parts.append(out)
    mem = mem_parts[0] if n_mem_blocks == 1 else jnp.concatenate(mem_parts, axis=0)
    rest_ref[:, SGU_WIDTH:] = _rms(mem, gg_ref[:, SGU_WIDTH:]).astype(BF16)


def _in_proj(x, mem_k, mem_v, lw, tm, spatial_w, spatial_b):
    nb, t, _ = x.shape
    nt = t // tm
    rows_per_mem = (nb * t) // mem_k.shape[0]
    if rows_per_mem >= tm:
        mem_per_tile = 1
        mem_index = lambda b, i: (((b * nt + i) * tm) // rows_per_mem, 0, 0)
    else:
        mem_per_tile = tm // rows_per_mem
        mem_index = lambda b, i: (b * nt + i, 0, 0)
    row = lambda w: pl.BlockSpec((None, tm, w), lambda b, i: (b, i, 0))
    const = lambda a: pl.BlockSpec(a.shape, lambda b, i: (0,) * a.ndim)
    memspec = pl.BlockSpec((mem_per_tile, N_MEM, MEM_WIDTH), mem_index)
    f32 = lambda w: jax.ShapeDtypeStruct((nb, t, w), F32)
    bf = lambda w: jax.ShapeDtypeStruct((nb, t, w), BF16)
    consts = (lw["g_pre_mix"], lw["w_qkv"], lw["w_b"], lw["b_forget"], spatial_w, spatial_b,
              lw["g_sgu"], lw["g_rest"])
    return pl.pallas_call(
        _inproj_kernel,
        grid=(nb, nt),
        in_specs=[row(D_MODEL)] + [const(a) for a in consts] + [memspec, memspec],
        out_specs=[row(FOX_WIDTH)] * 5 + [row(FOX_HEADS), row(SGU_WIDTH + MEM_WIDTH), row(SGU_WIDTH)],
        out_shape=[bf(FOX_WIDTH), f32(FOX_WIDTH), f32(FOX_WIDTH), bf(FOX_WIDTH), bf(FOX_WIDTH),
                   f32(FOX_HEADS), bf(SGU_WIDTH + MEM_WIDTH), f32(SGU_WIDTH)],
        compiler_params=_params("parallel", "arbitrary"),
        name="in_proj",
    )(x, *consts, mem_k, mem_v)


def _cumsum_kernel(x_ref, o_ref, *, minus_total):
    x = x_ref[...]
    n = x.shape[-1]
    lane = lax.broadcasted_iota(jnp.int32, x.shape, 1)
    shift = 1
    while shift < n:
        x = x + jnp.where(lane >= shift, pltpu.roll(x, shift, 1), 0.0)
        shift *= 2
    if minus_total:
        x = x - x[:, n - 1:n]
    o_ref[...] = x


def _lane_cumsum(x, minus_total=False):
    return pl.pallas_call(
        functools.partial(_cumsum_kernel, minus_total=minus_total),
        out_shape=jax.ShapeDtypeStruct(x.shape, F32),
        name="lane_cumsum",
    )(x)


def _attend_tile(qh, kp, vp, bq, bk, m, l, mask):
    s = _dot_nt(qh, kp) + bq - bk
    if mask is not None:
        s = jnp.where(mask, s, NEG_INF)
    m_new = jnp.maximum(m, jnp.max(s, axis=-1, keepdims=True))
    alpha = jnp.exp(m - m_new)
    e = jnp.exp(s - m_new)
    l_new = alpha * l + jnp.sum(e, axis=-1, keepdims=True)
    return m_new, l_new, alpha, _dot(e.astype(BF16), vp)


def _attend_pair_tile(q0, q1, kp, vp, bq0, bq1, bk0, bk1, carry, lo, mask):
    m0, l0, m1, l1, acc = carry
    m0, l0, a0, pv0 = _attend_tile(q0, kp, vp, bq0, bk0, m0, l0, mask)
    m1, l1, a1, pv1 = _attend_tile(q1, kp, vp, bq1, bk1, m1, l1, mask)
    acc = jnp.where(lo, a0 * acc + pv0, a1 * acc + pv1)
    return m0, l0, m1, l1, acc


def _pair_init(tq):
    neg = jnp.full((tq, 1), NEG_INF, F32)
    zero = jnp.zeros((tq, 1), F32)
    return neg, zero, neg, zero, jnp.zeros((tq, LANES), F32)


def _mix_out(fox, rest, x, wout_ref, gfox_ref, gpost_ref):
    fox_n = _rms(fox, gfox_ref[...]).astype(BF16)
    y = _dot(fox_n, wout_ref[:FOX_WIDTH, :]) + _dot(rest, wout_ref[FOX_WIDTH:, :])
    return x + _rms(y, gpost_ref[...])


def _fox_prompt_kernel(q_ref, k_ref, v_ref, ccol_ref, crow_ref, rest_ref, x_ref, wout_ref, gfox_ref,
                       gpost_ref, o_ref):
    tq = q_ref.shape[0]
    tk = crow_ref.shape[-1]
    i = pl.program_id(1)
    lo = lax.broadcasted_iota(jnp.int32, (tq, LANES), 1) < HEAD_DIM
    zero_q = jnp.zeros((tq, LANES), BF16)
    diag = lax.broadcasted_iota(jnp.int32, (tq, tk), 1) <= lax.broadcasted_iota(jnp.int32, (tq, tk), 0)
    fox_parts = []
    for p in range(FOX_HEADS // HEAD_PAIR):
        lanes = slice(p * LANES, (p + 1) * LANES)
        qp = q_ref[:, lanes]
        q0 = jnp.where(lo, qp, zero_q)
        q1 = jnp.where(lo, zero_q, qp)
        bq0 = ccol_ref[:, 2 * p:2 * p + 1]
        bq1 = ccol_ref[:, 2 * p + 1:2 * p + 2]

        def tile(j, carry, mask):
            off = pl.multiple_of(j * tk, tk)
            kp = k_ref[pl.ds(off, tk), lanes]
            vp = v_ref[pl.ds(off, tk), lanes]
            ck = crow_ref[j]
            return _attend_pair_tile(q0, q1, kp, vp, bq0, bq1, ck[2 * p:2 * p + 1, :],
                                     ck[2 * p + 1:2 * p + 2, :], carry, lo, mask)

        carry = lax.fori_loop(0, i, lambda j, c: tile(j, c, None), _pair_init(tq))
        _, l0, _, l1, acc = tile(i, carry, diag)
        fox_parts.append(acc / jnp.where(lo, l0, l1))
    fox = jnp.concatenate(fox_parts, axis=1)
    o_ref[...] = _mix_out(fox, rest_ref[...], x_ref[...], wout_ref, gfox_ref, gpost_ref)


def _fox_prompt(q, kb, vb, c_col, c_row, rest, x, lw):
    nb, s, _ = x.shape
    tq, tk = ATTN_Q_ROWS, ATTN_K_ROWS
    assert tq == tk
    nq = s // tq
    crow_t = c_row.reshape(nb, FOX_HEADS, s // tk, tk).transpose(0, 2, 1, 3)
    qrow = lambda w: pl.BlockSpec((None, tq, w), lambda b, i: (b, i, 0))
    full = lambda w: pl.BlockSpec((None, s, w), lambda b, i: (b, 0, 0))
    const = lambda a: pl.BlockSpec(a.shape, lambda b, i: (0,) * a.ndim)
    return pl.pallas_call(
        _fox_prompt_kernel,
        grid=(nb, nq),
        in_specs=[qrow(FOX_WIDTH), full(FOX_WIDTH), full(FOX_WIDTH), qrow(FOX_HEADS),
                  pl.BlockSpec((None, s // tk, FOX_HEADS, tk), lambda b, i: (b, 0, 0, 0)),
                  qrow(SGU_WIDTH + MEM_WIDTH), qrow(D_MODEL),
                  const(lw["w_out"]), const(lw["g_fox"]), const(lw["g_post_mix"])],
        out_specs=qrow(D_MODEL),
        out_shape=jax.ShapeDtypeStruct(x.shape, F32),
        compiler_params=_params("parallel", "arbitrary"),
        name="fox_mix_prompt",
    )(q, kb, vb, c_col, crow_t, rest, x, lw["w_out"], lw["g_fox"], lw["g_post_mix"])


def _fox_sample_kernel(q_ref, kn_ref, vn_ref, hk_ref, hv_ref, hrow_ref, ncol_ref, nrow_ref, rest_ref, x_ref,
                       wout_ref, gfox_ref, gpost_ref, o_ref, fox_scr):
    t = q_ref.shape[0]
    past = hk_ref.shape[0]
    tk = ATTN_K_ROWS
    b = pl.program_id(0)
    lo = lax.broadcasted_iota(jnp.int32, (t, LANES), 1) < HEAD_DIM
    zero_q = jnp.zeros((t, LANES), BF16)
    causal = lax.broadcasted_iota(jnp.int32, (t, t), 1) <= lax.broadcasted_iota(jnp.int32, (t, t), 0)
    fox_parts = []
    for p in range(FOX_HEADS // HEAD_PAIR):
        lanes = slice(p * LANES, (p + 1) * LANES)
        qp = q_ref[:, lanes]
        q0 = jnp.where(lo, qp, zero_q)
        q1 = jnp.where(lo, zero_q, qp)
        bq0 = ncol_ref[:, 2 * p:2 * p + 1]
        bq1 = ncol_ref[:, 2 * p + 1:2 * p + 2]
        carry = _pair_init(t)
        for j in range(past // tk):
            rows = slice(j * tk, (j + 1) * tk)
            carry = _attend_pair_tile(q0, q1, hk_ref[rows, lanes].astype(BF16), hv_ref[rows, lanes].astype(BF16),
                                      bq0, bq1, hrow_ref[2 * p:2 * p + 1, rows],
                                      hrow_ref[2 * p + 1:2 * p + 2, rows], carry, lo, None)
        _, l0, _, l1, acc = _attend_pair_tile(q0, q1, kn_ref[:, lanes], vn_ref[:, lanes], bq0, bq1,
                                              nrow_ref[2 * p:2 * p + 1, :t], nrow_ref[2 * p + 1:2 * p + 2, :t],
                                              carry, lo, causal)
        fox_parts.append(acc / jnp.where(lo, l0, l1))
    fox_scr[pl.ds(pl.multiple_of(b * t, t), t), :] = jnp.concatenate(fox_parts, axis=1)

    @pl.when(b == pl.num_programs(0) - 1)
    def _():
        o_ref[...] = _mix_out(fox_scr[...], rest_ref[...], x_ref[...], wout_ref, gfox_ref, gpost_ref)


def _fox_sample(q, kb, vb, hk, hv, hist_row, new_col, new_row, rest, x, lw):
    nb, t, _ = q.shape
    past = hk.shape[1]
    per_b = lambda a: pl.BlockSpec((None,) + a.shape[1:], lambda b: (b,) + (0,) * (a.ndim - 1))
    const = lambda a: pl.BlockSpec(a.shape, lambda b: (0,) * a.ndim)
    return pl.pallas_call(
        _fox_sample_kernel,
        grid=(nb,),
        in_specs=[per_b(q), per_b(kb), per_b(vb), per_b(hk), per_b(hv), per_b(hist_row), per_b(new_col),
                  per_b(new_row), const(rest), const(x), const(lw["w_out"]), const(lw["g_fox"]),
                  const(lw["g_post_mix"])],
        out_specs=const(x),
        out_shape=jax.ShapeDtypeStruct(x.shape, F32),
        scratch_shapes=[pltpu.VMEM((nb * t, FOX_WIDTH), F32)],
        compiler_params=_params("arbitrary"),
        name="fox_mix_sample",
    )(q, kb, vb, hk, hv, hist_row, new_col, new_row, rest, x, lw["w_out"], lw["g_fox"], lw["g_post_mix"])


def _ffn_step(f, x_ref, gpre_ref, wa_ref, wl_ref, wdw_ref, bdw_ref, wd_ref, gpost_ref, o_ref, h2_scr, acc_scr,
              shifted):
    nf = pl.num_programs(2)

    @pl.when(f == 0)
    def _():
        h2_scr[...] = _rms(x_ref[...], gpre_ref[...]).astype(BF16)
        acc_scr[...] = jnp.zeros_like(acc_scr)

    h2 = h2_scr[...]
    a = _dot(h2, wa_ref[...])
    lin = _dot(h2, wl_ref[...])
    a1, a2 = shifted(a)
    conv = bdw_ref[...] + wdw_ref[0:1, :] * a2
    conv = conv + wdw_ref[1:2, :] * a1
    conv = conv + wdw_ref[2:3, :] * a
    acc_scr[...] += _dot((jax.nn.silu(conv) * lin).astype(BF16), wd_ref[...])

    @pl.when(f == nf - 1)
    def _():
        o_ref[...] = x_ref[...] + _rms(acc_scr[...], gpost_ref[...])

    return a


def _ffn_prompt_kernel(x_ref, gpre_ref, wa_ref, wl_ref, wdw_ref, bdw_ref, wd_ref, gpost_ref, hist_ref,
                       o_ref, tail_ref, h2_scr, acc_scr, carry_scr, work_scr):
    tm = x_ref.shape[0]
    i = pl.program_id(1)
    f = pl.program_id(2)
    head = SUBLANES

    def shifted(a):
        @pl.when(i == 0)
        def _():
            work_scr[head - (CONV_WIDTH - 1):head, :] = hist_ref[...]

        @pl.when(i > 0)
        def _():
            work_scr[0:head, :] = carry_scr[f]

        work_scr[head:head + tm, :] = a
        return work_scr[head - 1:head - 1 + tm, :], work_scr[head - 2:head - 2 + tm, :]

    a = _ffn_step(f, x_ref, gpre_ref, wa_ref, wl_ref, wdw_ref, bdw_ref, wd_ref, gpost_ref, o_ref, h2_scr,
                  acc_scr, shifted)
    carry_scr[f] = a[tm - head:, :]
    tail_ref[...] = a[tm - head:, :]


def _ffn_sample_kernel(x_ref, gpre_ref, wa_ref, wl_ref, wdw_ref, bdw_ref, wd_ref, gpost_ref, e1_ref, e2_ref,
                       o_ref, a_ref, h2_scr, acc_scr, work_scr, *, seg):
    tm = x_ref.shape[0]
    f = pl.program_id(2)
    head = SUBLANES
    rmod = lax.broadcasted_iota(jnp.int32, (tm, FFN_COLS), 0) % seg

    def shifted(a):
        work_scr[0:head, :] = jnp.zeros((head, FFN_COLS), F32)
        work_scr[head:head + tm, :] = a
        a1 = jnp.where(rmod >= 1, work_scr[head - 1:head - 1 + tm, :], e1_ref[...])
        a2 = jnp.where(rmod >= 2, work_scr[head - 2:head - 2 + tm, :], e2_ref[...])
        return a1, a2

    a_ref[...] = _ffn_step(f, x_ref, gpre_ref, wa_ref, wl_ref, wdw_ref, bdw_ref, wd_ref, gpost_ref, o_ref,
                           h2_scr, acc_scr, shifted)


def _ffn_common_specs(tm, nf):
    return [
        pl.BlockSpec((None, tm, D_MODEL), lambda b, i, f: (b, i, 0)),
        pl.BlockSpec((1, D_MODEL), lambda b, i, f: (0, 0)),
        pl.BlockSpec((D_MODEL, FFN_COLS), lambda b, i, f: (0, f)),
        pl.BlockSpec((D_MODEL, FFN_COLS), lambda b, i, f: (0, nf + f)),
        pl.BlockSpec((CONV_WIDTH, FFN_COLS), lambda b, i, f: (0, f)),
        pl.BlockSpec((1, FFN_COLS), lambda b, i, f: (0, f)),
        pl.BlockSpec((FFN_COLS, D_MODEL), lambda b, i, f: (f, 0)),
        pl.BlockSpec((1, D_MODEL), lambda b, i, f: (0, 0)),
    ]


def _ffn_prompt(x, hist, lw):
    nb, s, _ = x.shape
    tm = FFN_ROWS
    nt, nf = s // tm, FFN_DIM // FFN_COLS
    return pl.pallas_call(
        _ffn_prompt_kernel,
        grid=(nb, nt, nf),
        in_specs=_ffn_common_specs(tm, nf) + [
            pl.BlockSpec((None, CONV_WIDTH - 1, FFN_COLS), lambda b, i, f: (b, 0, f))],
        out_specs=[pl.BlockSpec((None, tm, D_MODEL), lambda b, i, f: (b, i, 0)),
                   pl.BlockSpec((None, None, SUBLANES, FFN_COLS), lambda b, i, f: (b, i, 0, f))],
        out_shape=[jax.ShapeDtypeStruct(x.shape, F32),
                   jax.ShapeDtypeStruct((nb, nt, SUBLANES, FFN_DIM), F32)],
        scratch_shapes=[pltpu.VMEM((tm, D_MODEL), BF16), pltpu.VMEM((tm, D_MODEL), F32),
                        pltpu.VMEM((nf, SUBLANES, FFN_COLS), F32),
                        pltpu.VMEM((tm + SUBLANES, FFN_COLS), F32)],
        compiler_params=_params("parallel", "arbitrary", "arbitrary"),
        name="conv_ffn_prompt",
    )(x, lw["g_pre_ffn"], lw["w_up"], lw["w_up"], lw["w_dwconv"], lw["b_dwconv"], lw["w_down"],
      lw["g_post_ffn"], hist)


def _ffn_sample(x, e1, e2, lw, seg):
    _, rows, _ = x.shape
    nf = FFN_DIM // FFN_COLS
    hist_spec = pl.BlockSpec((rows, FFN_COLS), lambda b, i, f: (0, f))
    return pl.pallas_call(
        functools.partial(_ffn_sample_kernel, seg=seg),
        grid=(1, 1, nf),
        in_specs=_ffn_common_specs(rows, nf) + [hist_spec, hist_spec],
        out_specs=[pl.BlockSpec((None, rows, D_MODEL), lambda b, i, f: (b, i, 0)),
                   pl.BlockSpec((rows, FFN_COLS), lambda b, i, f: (0, f))],
        out_shape=[jax.ShapeDtypeStruct(x.shape, F32), jax.ShapeDtypeStruct((rows, FFN_DIM), F32)],
        scratch_shapes=[pltpu.VMEM((rows, D_MODEL), BF16), pltpu.VMEM((rows, D_MODEL), F32),
                        pltpu.VMEM((rows + SUBLANES, FFN_COLS), F32)],
        compiler_params=_params("arbitrary", "arbitrary", "arbitrary"),
        name="conv_ffn_sample",
    )(x, lw["g_pre_ffn"], lw["w_up"], lw["w_up"], lw["w_dwconv"], lw["b_dwconv"], lw["w_down"],
      lw["g_post_ffn"], e1, e2)


def _layer_weights(l, g_pre_mix, w_in, b_forget, w_spatial, b_spatial, g_sgu, g_group_out, w_out, g_post_mix,
                   g_pre_ffn, w_up, w_dwconv, b_dwconv, w_down, g_post_ffn, dec_seq):
    row = lambda a: a[l].reshape(1, -1)
    wi = w_in[l]
    c0 = 3 * FOX_WIDTH
    c1 = c0 + FOX_HEADS
    c2 = c1 + 2 * SGU_WIDTH
    w_b = jnp.concatenate([wi[:, c1:c2], wi[:, c2:], wi[:, c0:c1],
                           jnp.zeros((D_MODEL, B_COLS - (2 * SGU_WIDTH + MEM_WIDTH + FOX_HEADS)), F32)], axis=1)
    ws = w_spatial[l]
    bs = b_spatial[l]
    group_dim = SGU_WIDTH // SGU_GROUPS
    ws_prompt = jnp.concatenate([ws[g] for g in range(SGU_GROUPS)], axis=1)
    bt_prompt = jnp.repeat(bs.T, group_dim, axis=1)
    reps = GMLP_CHUNK // dec_seq
    blk = (jnp.arange(GMLP_CHUNK)[:, None] // dec_seq) == (jnp.arange(GMLP_CHUNK)[None, :] // dec_seq)
    ws_sample = jnp.concatenate(
        [jnp.where(blk, jnp.tile(ws[g, :dec_seq, :dec_seq], (reps, reps)), 0.0) for g in range(SGU_GROUPS)], axis=1)
    bt_sample = jnp.tile(jnp.repeat(bs[:, :dec_seq].T, group_dim, axis=1), (reps, 1))
    return {
        "g_pre_mix": row(g_pre_mix), "w_qkv": wi[:, :c0].astype(BF16), "w_b": w_b.astype(BF16),
        "b_forget": row(b_forget), "g_sgu": row(g_sgu),
        "g_fox": row(g_group_out)[:, :FOX_WIDTH], "g_rest": row(g_group_out)[:, FOX_WIDTH:],
        "w_out": w_out[l].astype(BF16), "g_post_mix": row(g_post_mix), "g_pre_ffn": row(g_pre_ffn),
        "w_up": w_up[l].astype(BF16), "w_dwconv": w_dwconv[l], "b_dwconv": row(b_dwconv),
        "w_down": w_down[l].astype(BF16), "g_post_ffn": row(g_post_ffn),
        "ws_prompt": ws_prompt, "bt_prompt": bt_prompt, "ws_sample": ws_sample, "bt_sample": bt_sample,
    }


def kernel(x_prompt, x_sample, mem_prompt, cache_fox_k, cache_fox_v, cache_fox_logf, cache_mem_k, cache_mem_v,
           cache_ffn_conv, g_pre_mix, w_in, b_forget, w_spatial, b_spatial, g_sgu, g_mem, w_mem_kv, g_group_out,
           w_out, g_post_mix, g_pre_ffn, w_up, w_dwconv, b_dwconv, w_down, g_post_ffn):
    depth = w_in.shape[0]
    batch, seq, _ = x_prompt.shape
    dec_batch, dec_seq, _ = x_sample.shape
    past = cache_fox_k.shape[2]
    dec_rows = dec_batch * dec_seq

    mem_k_all, mem_v_all = _memory_kv(mem_prompt, g_mem, w_mem_kv.astype(BF16))

    yp = x_prompt
    ys = x_sample.reshape(1, dec_rows, D_MODEL)
    zeros_hist = jnp.zeros((batch, CONV_WIDTH - 1, FFN_DIM), F32)
    outs = [[] for _ in range(11)]
    for l in range(depth):
        lw = _layer_weights(l, g_pre_mix, w_in, b_forget, w_spatial, b_spatial, g_sgu, g_group_out, w_out,
                            g_post_mix, g_pre_ffn, w_up, w_dwconv, b_dwconv, w_down, g_post_ffn, dec_seq)

        q, k, v, kb, vb, logf, rest, _ = _in_proj(yp, mem_k_all[l], mem_v_all[l], lw, IN_PROJ_ROWS,
                                                  lw["ws_prompt"], lw["bt_prompt"])
        logf_row = jnp.swapaxes(logf, 1, 2).reshape(batch * FOX_HEADS, seq)
        c_row = _lane_cumsum(logf_row).reshape(batch, FOX_HEADS, seq)
        c_col = jnp.swapaxes(c_row, 1, 2)
        yp = _fox_prompt(q, kb, vb, c_col, c_row, rest, yp, lw)
        yp, tail = _ffn_prompt(yp, zeros_hist, lw)
        outs[0].append(k.reshape(batch, seq, FOX_HEADS, HEAD_DIM))
        outs[1].append(v.reshape(batch, seq, FOX_HEADS, HEAD_DIM))
        outs[2].append(logf)
        outs[3].append(mem_k_all[l].reshape(batch, N_MEM, MEM_HEADS, HEAD_DIM))
        outs[4].append(mem_v_all[l].reshape(batch, N_MEM, MEM_HEADS, HEAD_DIM))
        outs[5].append(tail[:, -1, SUBLANES - (CONV_WIDTH - 1):, :])

        smk = cache_mem_k[l].reshape(dec_batch, N_MEM, MEM_WIDTH)
        smv = cache_mem_v[l].reshape(dec_batch, N_MEM, MEM_WIDTH)
        q, k, v, kb, vb, logf, rest, vrows = _in_proj(ys, smk, smv, lw, dec_rows, lw["ws_sample"], lw["bt_sample"])
        per_b = lambda a: a.reshape(dec_batch, dec_seq, a.shape[-1])
        hist_row = _lane_cumsum(jnp.swapaxes(cache_fox_logf[l], 1, 2).reshape(dec_batch * FOX_HEADS, past),
                                minus_total=True).reshape(dec_batch, FOX_HEADS, past)
        logf_new_row = jnp.swapaxes(per_b(logf), 1, 2).reshape(dec_batch * FOX_HEADS, dec_seq)
        new_row = _lane_cumsum(jnp.pad(logf_new_row, ((0, 0), (0, LANES - dec_seq)))
                               ).reshape(dec_batch, FOX_HEADS, LANES)
        new_col = jnp.swapaxes(new_row[:, :, :dec_seq], 1, 2)
        hk = cache_fox_k[l].reshape(dec_batch, past, FOX_WIDTH)
        hv = cache_fox_v[l].reshape(dec_batch, past, FOX_WIDTH)
        ys2 = _fox_sample(per_b(q[0]), per_b(kb[0]), per_b(vb[0]), hk, hv, hist_row, new_col, new_row,
                          rest[0], ys[0], lw)
        conv_hist = cache_ffn_conv[l]
        pad_rows = lambda a: jnp.pad(a, ((0, 0), (0, dec_seq - a.shape[1]), (0, 0))).reshape(dec_rows, FFN_DIM)
        e1 = pad_rows(conv_hist[:, 1:2, :])
        e2 = pad_rows(conv_hist)
        ys, a_all = _ffn_sample(ys2.reshape(1, dec_rows, D_MODEL), e1, e2, lw, dec_seq)
        outs[6].append(per_b(k[0]).reshape(dec_batch, dec_seq, FOX_HEADS, HEAD_DIM))
        outs[7].append(per_b(v[0]).reshape(dec_batch, dec_seq, FOX_HEADS, HEAD_DIM))
        outs[8].append(per_b(logf[0]))
        outs[9].append(per_b(vrows[0]))
        outs[10].append(a_all.reshape(dec_batch, dec_seq, FFN_DIM)[:, dec_seq - (CONV_WIDTH - 1):, :])

    stacked = [jnp.stack(o) for o in outs]
    return (yp, ys.reshape(dec_batch, dec_seq, D_MODEL), *stacked)
```

```python
import functools

import jax
import jax.numpy as jnp
from jax import lax
from jax.experimental import pallas as pl
from jax.experimental.pallas import tpu as pltpu

D_MODEL = 1024
HEAD_DIM = 64
FOX_WIDTH = 512
FOX_HEADS = 8
SGU_WIDTH = 256
SGU_GROUPS = 4
GMLP_CHUNK = 128
CHUNK = 64
MEM_WIDTH = 256
MEM_HEADS = 4
N_MEM = 256
FFN_DIM = 2816
CONV_WIDTH = 3
RMS_EPS = 1e-6
NEG_INF = -1e30
QK_SCALE = HEAD_DIM ** -0.5

LANES = 128
SUBLANES = 8
HEAD_PAIR = LANES // HEAD_DIM
VMEM_LIMIT_BYTES = 56 * 1024 * 1024

IN_PROJ_ROWS = 512
ATTN_Q_ROWS = 256
ATTN_K_ROWS = 256
FFN_ROWS = 512
FFN_COLS = 256
B_COLS = 896

BF16 = jnp.bfloat16
F32 = jnp.float32


def _rms(x, g):
    y = x * lax.rsqrt(jnp.mean(x * x, axis=-1, keepdims=True) + RMS_EPS)
    return y * g


def _dot(a, b):
    return jnp.dot(a, b, preferred_element_type=F32)


def _dot_nt(a, b):
    return lax.dot_general(a, b, (((1,), (1,)), ((), ())), preferred_element_type=F32)


def _params(*semantics):
    return pltpu.CompilerParams(dimension_semantics=semantics, vmem_limit_bytes=VMEM_LIMIT_BYTES)


def _memkv_kernel(mem_ref, g_ref, w_ref, mk_ref, mv_ref):
    h = _rms(mem_ref[...], g_ref[...]).astype(BF16)
    kv = _dot(h, w_ref[...])
    mk_ref[...] = kv[:, :MEM_WIDTH]
    mv_ref[...] = kv[:, MEM_WIDTH:]


def _memory_kv(mem, g_mem, w_mem_kv_bf):
    depth = g_mem.shape[0]
    batch = mem.shape[0]
    out = jax.ShapeDtypeStruct((depth, batch, N_MEM, MEM_WIDTH), F32)
    return pl.pallas_call(
        _memkv_kernel,
        grid=(depth, batch),
        in_specs=[
            pl.BlockSpec((None, N_MEM, D_MODEL), lambda l, b: (b, 0, 0)),
            pl.BlockSpec((None, 1, D_MODEL), lambda l, b: (l, 0, 0)),
            pl.BlockSpec((None, D_MODEL, 2 * MEM_WIDTH), lambda l, b: (l, 0, 0)),
        ],
        out_specs=[
            pl.BlockSpec((None, None, N_MEM, MEM_WIDTH), lambda l, b: (l, b, 0, 0)),
            pl.BlockSpec((None, None, N_MEM, MEM_WIDTH), lambda l, b: (l, b, 0, 0)),
        ],
        out_shape=[out, out],
        compiler_params=_params("arbitrary", "arbitrary"),
        name="memory_kv",
    )(mem, g_mem.reshape(depth, 1, D_MODEL), w_mem_kv_bf)


def _inproj_kernel(x_ref, gpre_ref, wqkv_ref, wb_ref, bfg_ref, ws_ref, bt_ref, gsgu_ref, gg_ref,
                   mk_ref, mv_ref, *out_refs, transposed):
    tm = x_ref.shape[0]
    n_mem_blocks = mk_ref.shape[0]
    h = _rms(x_ref[...], gpre_ref[...]).astype(BF16)

    yq = _dot(h, wqkv_ref[...])
    q = yq[:, :FOX_WIDTH] * QK_SCALE
    k = yq[:, FOX_WIDTH:2 * FOX_WIDTH]
    v = yq[:, 2 * FOX_WIDTH:]
    if transposed:
        qt_ref, k_ref, v_ref, kb_ref, vt_ref, lf_ref, rest_ref = out_refs
        vrows_ref = None
        qt_ref[...] = q.T.astype(BF16)
        vt = v.T.astype(BF16)
        tk = vt_ref.shape[-1]
        for c in range(tm // tk):
            vt_ref[c] = vt[:, c * tk:(c + 1) * tk]
    else:
        q_ref, k_ref, v_ref, kb_ref, vb_ref, lf_ref, rest_ref, vrows_ref = out_refs
        q_ref[...] = q.astype(BF16)
        vb_ref[...] = v.astype(BF16)
    k_ref[...] = k
    v_ref[...] = v
    kb_ref[...] = k.astype(BF16)

    yb = _dot(h, wb_ref[...])
    fg = yb[:, 2 * SGU_WIDTH + MEM_WIDTH:2 * SGU_WIDTH + MEM_WIDTH + FOX_HEADS]
    lf_ref[...] = jax.nn.log_sigmoid(fg + bfg_ref[...])

    z = jax.nn.gelu(yb[:, :2 * SGU_WIDTH])
    u = z[:, :SGU_WIDTH]
    vv = _rms(z[:, SGU_WIDTH:], gsgu_ref[...])
    if vrows_ref is not None:
        vrows_ref[...] = vv
    vvb = vv.astype(BF16)
    wrow = lax.broadcasted_iota(jnp.int32, (GMLP_CHUNK, SGU_GROUPS * GMLP_CHUNK), 0)
    wcol = lax.broadcasted_iota(jnp.int32, (GMLP_CHUNK, SGU_GROUPS * GMLP_CHUNK), 1)
    wmask = ((wcol % GMLP_CHUNK) // CHUNK) <= (wrow // CHUNK)
    wcat = jnp.where(wmask, ws_ref[...], 0.0).astype(BF16)
    glane = lax.broadcasted_iota(jnp.int32, (GMLP_CHUNK, SGU_WIDTH), 1) // (SGU_WIDTH // SGU_GROUPS)
    zero_chunk = jnp.zeros((GMLP_CHUNK, SGU_WIDTH), BF16)
    sgu_parts = []
    for c in range(tm // GMLP_CHUNK):
        vc = vvb[c * GMLP_CHUNK:(c + 1) * GMLP_CHUNK, :]
        rhs = jnp.concatenate([jnp.where(glane == g, vc, zero_chunk) for g in range(SGU_GROUPS)], axis=0)
        mixed = _dot(wcat, rhs) + bt_ref[...]
        sgu_parts.append(u[c * GMLP_CHUNK:(c + 1) * GMLP_CHUNK, :] * mixed)
    sgu = jnp.concatenate(sgu_parts, axis=0)
    rest_ref[:, :SGU_WIDTH] = _rms(sgu, gg_ref[:, :SGU_WIDTH]).astype(BF16)

    qmb = (yb[:, 2 * SGU_WIDTH:2 * SGU_WIDTH + MEM_WIDTH] * QK_SCALE).astype(BF16)
    rows = tm // n_mem_blocks
    hlane = lax.broadcasted_iota(jnp.int32, (rows, MEM_WIDTH), 1) // HEAD_DIM
    zero_q = jnp.zeros((rows, MEM_WIDTH), BF16)
    mem_parts = []
    for bi in range(n_mem_blocks):
        qb = qmb[bi * rows:(bi + 1) * rows, :]
        mkb = mk_ref[bi].astype(BF16)
        mvb = mv_ref[bi].astype(BF16)
        out = jnp.zeros((rows, MEM_WIDTH), F32)
        for hd in range(MEM_HEADS):
            hm = hlane == hd
            s = _dot_nt(jnp.where(hm, qb, zero_q), mkb)
            e = jnp.exp(s - jnp.max(s, axis=-1, keepdims=True))
            o = _dot(e.astype(BF16), mvb) / jnp.sum(e, axis=-1, keepdims=True)
            out = jnp.where(hm, o, out)
        mem_parts.append(out)
    mem = mem_parts[0] if n_mem_blocks == 1 else jnp.concatenate(mem_parts, axis=0)
    rest_ref[:, SGU_WIDTH:] = _rms(mem, gg_ref[:, SGU_WIDTH:]).astype(BF16)


def _in_proj(x, mem_k, mem_v, mem_first, mem_per_tile, lw, tm, spatial_w, spatial_b, transposed):
    nb, t, _ = x.shape
    nt = t // tm
    if transposed:
        assert mem_per_tile == 1
        mem_index = lambda b, i: (mem_first + b, 0, 0)
    else:
        mem_index = lambda b, i: (mem_first // mem_per_tile + b * nt + i, 0, 0)
    row = lambda w: pl.BlockSpec((None, tm, w), lambda b, i: (b, i, 0))
    const = lambda a: pl.BlockSpec(a.shape, lambda b, i: (0,) * a.ndim)
    memspec = pl.BlockSpec((mem_per_tile, N_MEM, MEM_WIDTH), mem_index)
    f32 = lambda w: jax.ShapeDtypeStruct((nb, t, w), F32)
    bf = lambda w: jax.ShapeDtypeStruct((nb, t, w), BF16)
    consts = (lw["g_pre_mix"], lw["w_qkv"], lw["w_b"], lw["b_forget"], spatial_w, spatial_b,
              lw["g_sgu"], lw["g_rest"])
    if transposed:
        tk = ATTN_K_ROWS
        out_specs = [pl.BlockSpec((None, FOX_WIDTH, tm), lambda b, i: (b, 0, i)),
                     row(FOX_WIDTH), row(FOX_WIDTH), row(FOX_WIDTH),
                     pl.BlockSpec((None, tm // tk, FOX_WIDTH, tk), lambda b, i: (b, i, 0, 0)),
                     row(FOX_HEADS), row(SGU_WIDTH + MEM_WIDTH)]
        out_shape = [jax.ShapeDtypeStruct((nb, FOX_WIDTH, t), BF16), f32(FOX_WIDTH), f32(FOX_WIDTH),
                     bf(FOX_WIDTH), jax.ShapeDtypeStruct((nb, t // tk, FOX_WIDTH, tk), BF16),
                     f32(FOX_HEADS), bf(SGU_WIDTH + MEM_WIDTH)]
    else:
        out_specs = [row(FOX_WIDTH)] * 5 + [row(FOX_HEADS), row(SGU_WIDTH + MEM_WIDTH), row(SGU_WIDTH)]
        out_shape = [bf(FOX_WIDTH), f32(FOX_WIDTH), f32(FOX_WIDTH), bf(FOX_WIDTH), bf(FOX_WIDTH),
                     f32(FOX_HEADS), bf(SGU_WIDTH + MEM_WIDTH), f32(SGU_WIDTH)]
    return pl.pallas_call(
        functools.partial(_inproj_kernel, transposed=transposed),
        grid=(nb, nt),
        in_specs=[row(D_MODEL)] + [const(a) for a in consts] + [memspec, memspec],
        out_specs=out_specs,
        out_shape=out_shape,
        compiler_params=_params("parallel", "arbitrary"),
        name="in_proj",
    )(x, *consts, mem_k, mem_v)


def _cumsum_kernel(x_ref, o_ref, *, minus_total):
    x = x_ref[...]
    n = x.shape[-1]
    lane = lax.broadcasted_iota(jnp.int32, x.shape, 1)
    shift = 1
    while shift < n:
        x = x + jnp.where(lane >= shift, pltpu.roll(x, shift, 1), 0.0)
        shift *= 2
    if minus_total:
        x = x - x[:, n - 1:n]
    o_ref[...] = x


def _lane_cumsum(x, minus_total=False):
    return pl.pallas_call(
        functools.partial(_cumsum_kernel, minus_total=minus_total),
        out_shape=jax.ShapeDtypeStruct(x.shape, F32),
        name="lane_cumsum",
    )(x)


def _attend_tile(qh, kp, vp, bq, bk, m, l, mask):
    s = _dot_nt(qh, kp) + bq - bk
    if mask is not None:
        s = jnp.where(mask, s, NEG_INF)
    m_new = jnp.maximum(m, jnp.max(s, axis=-1, keepdims=True))
    alpha = jnp.exp(m - m_new)
    e = jnp.exp(s - m_new)
    l_new = alpha * l + jnp.sum(e, axis=-1, keepdims=True)
    return m_new, l_new, alpha, _dot(e.astype(BF16), vp)


def _attend_pair_tile(q0, q1, kp, vp, bq0, bq1, bk0, bk1, carry, lo, mask):
    m0, l0, m1, l1, acc = carry
    m0, l0, a0, pv0 = _attend_tile(q0, kp, vp, bq0, bk0, m0, l0, mask)
    m1, l1, a1, pv1 = _attend_tile(q1, kp, vp, bq1, bk1, m1, l1, mask)
    acc = jnp.where(lo, a0 * acc + pv0, a1 * acc + pv1)
    return m0, l0, m1, l1, acc


def _pair_init(tq):
    neg = jnp.full((tq, 1), NEG_INF, F32)
    zero = jnp.zeros((tq, 1), F32)
    return neg, zero, neg, zero, jnp.zeros((tq, LANES), F32)


def _mix_out(fox, rest, x, wout_ref, gfox_ref, gpost_ref):
    fox_n = _rms(fox, gfox_ref[...]).astype(BF16)
    y = _dot(fox_n, wout_ref[:FOX_WIDTH, :]) + _dot(rest, wout_ref[FOX_WIDTH:, :])
    return x + _rms(y, gpost_ref[...])


def _split3(c):
    hi = c.astype(BF16).astype(F32)
    r = c - hi
    mid = r.astype(BF16).astype(F32)
    lo = (r - mid).astype(BF16).astype(F32)
    return hi, mid, lo


_BIAS_ONES = 6


def _fox_prompt_kernel(qt_ref, kb_ref, vt_ref, ccol_ref, crow_ref, rest_ref, x_ref, wout_ref, gfox_ref,
                       gpost_ref, o_ref, kaug_scr, qa_scr, sa_scr, sb_scr, m_scr, l_scr, acc_scr):
    tq = qt_ref.shape[1]
    tk = vt_ref.shape[-1]
    s_len = kb_ref.shape[0]
    slab = 2 * LANES
    i = pl.program_id(1)

    @pl.when(i == 0)
    def _():
        lane = lax.broadcasted_iota(jnp.int32, (tk, LANES), 1)
        ones = jnp.where((lane >= _BIAS_ONES) & (lane < _BIAS_ONES + 3), 1.0, 0.0)
        for r in range(s_len // tk):
            rows = slice(r * tk, (r + 1) * tk)
            cc = ccol_ref[rows, :]
            for p in range(FOX_HEADS // HEAD_PAIR):
                kaug_scr[rows, p * slab:p * slab + LANES] = kb_ref[rows, p * LANES:(p + 1) * LANES]
                blk = ones
                pieces = _split3(cc[:, 2 * p:2 * p + 1]) + _split3(cc[:, 2 * p + 1:2 * p + 2])
                for n, piece in enumerate(pieces):
                    blk = jnp.where(lane == n, piece, blk)
                kaug_scr[rows, p * slab + LANES:(p + 1) * slab] = blk.astype(BF16)

    rowi = lax.broadcasted_iota(jnp.int32, (LANES, tq), 0)
    diag = lax.broadcasted_iota(jnp.int32, (tk, tq), 0) <= lax.broadcasted_iota(jnp.int32, (tk, tq), 1)
    crow = crow_ref[...]
    zero_q = jnp.zeros((LANES, tq), BF16)
    for p in range(FOX_HEADS // HEAD_PAIR):
        qtp = qt_ref[p * LANES:(p + 1) * LANES, :]
        for hh in range(HEAD_PAIR):
            own = (rowi >= hh * HEAD_DIM) & (rowi < (hh + 1) * HEAD_DIM)
            chi, cmid, clo = _split3(crow[2 * p + hh:2 * p + hh + 1, :])
            br = jnp.where(rowi == _BIAS_ONES, chi,
                           jnp.where(rowi == _BIAS_ONES + 1, cmid, jnp.where(rowi == _BIAS_ONES + 2, clo, 0.0)))
            br = jnp.where((rowi >= 3 * hh) & (rowi < 3 * hh + 3), -1.0, br)
            qa_scr[2 * p + hh, :LANES, :] = jnp.where(own, qtp, zero_q)
            qa_scr[2 * p + hh, LANES:, :] = br.astype(BF16)

    m_scr[...] = jnp.full(m_scr.shape, NEG_INF, F32)
    l_scr[...] = jnp.zeros(l_scr.shape, F32)
    acc_scr[...] = jnp.zeros(acc_scr.shape, F32)

    def scores_to(buf_ref, j):
        off = pl.multiple_of(j * tk, tk)
        for h in range(FOX_HEADS):
            ka = kaug_scr[pl.ds(off, tk), (h // HEAD_PAIR) * slab:(h // HEAD_PAIR + 1) * slab]
            buf_ref[h] = _dot(ka, qa_scr[h])

    def absorb_from(buf_ref, j, mask):
        vt = vt_ref[j]
        for h in range(FOX_HEADS):
            feat = slice(h * HEAD_DIM, (h + 1) * HEAD_DIM)
            s = buf_ref[h]
            if mask is not None:
                s = jnp.where(mask, s, NEG_INF)
            m = m_scr[h:h + 1, :]
            m_new = jnp.maximum(m, jnp.max(s, axis=0, keepdims=True))
            alpha = jnp.exp(m - m_new)
            e = jnp.exp(s - m_new)
            m_scr[h:h + 1, :] = m_new
            l_scr[h:h + 1, :] = alpha * l_scr[h:h + 1, :] + jnp.sum(e, axis=0, keepdims=True)
            acc_scr[feat, :] = alpha * acc_scr[feat, :] + _dot(vt[feat, :], e.astype(BF16))

    odd = i % 2

    @pl.when(odd == 1)
    def _():
        scores_to(sb_scr, 0)
        scores_to(sa_scr, 1)
        absorb_from(sb_scr, 0, None)

    @pl.when(odd == 0)
    def _():
        scores_to(sa_scr, 0)

    def step(jj, _):
        t = odd + 2 * jj
        scores_to(sb_scr, t + 1)
        absorb_from(sa_scr, t, None)
        scores_to(sa_scr, t + 2)
        absorb_from(sb_scr, t + 1, None)
        return 0

    lax.fori_loop(0, i // 2, step, 0)
    absorb_from(sa_scr, i, diag)
    fox_t = jnp.concatenate([acc_scr[h * HEAD_DIM:(h + 1) * HEAD_DIM, :] / l_scr[h:h + 1, :]
                             for h in range(FOX_HEADS)], axis=0)
    fox = fox_t.T
    o_ref[...] = _mix_out(fox, rest_ref[...], x_ref[...], wout_ref, gfox_ref, gpost_ref)


def _fox_prompt(qt, kb, vt, c_col, c_row, rest, x, lw):
    nb, s, _ = x.shape
    tq, tk = ATTN_Q_ROWS, ATTN_K_ROWS
    assert tq == tk and vt.shape[-1] == tk
    nq = s // tq
    qrow = lambda w: pl.BlockSpec((None, tq, w), lambda b, i: (b, i, 0))
    qcol = lambda r: pl.BlockSpec((None, r, tq), lambda b, i: (b, 0, i))
    full = lambda a: pl.BlockSpec((None,) + a.shape[1:], lambda b, i: (b,) + (0,) * (a.ndim - 1))
    const = lambda a: pl.BlockSpec(a.shape, lambda b, i: (0,) * a.ndim)
    return pl.pallas_call(
        _fox_prompt_kernel,
        grid=(nb, nq),
        in_specs=[qcol(FOX_WIDTH), full(kb), full(vt), full(c_col), qcol(FOX_HEADS),
                  qrow(SGU_WIDTH + MEM_WIDTH), qrow(D_MODEL),
                  const(lw["w_out"]), const(lw["g_fox"]), const(lw["g_post_mix"])],
        out_specs=qrow(D_MODEL),
        out_shape=jax.ShapeDtypeStruct(x.shape, F32),
        scratch_shapes=[pltpu.VMEM((s, 2 * FOX_WIDTH), BF16),
                        pltpu.VMEM((FOX_HEADS, 2 * LANES, tq), BF16),
                        pltpu.VMEM((FOX_HEADS, tk, tq), F32), pltpu.VMEM((FOX_HEADS, tk, tq), F32),
                        pltpu.VMEM((FOX_HEADS, tq), F32), pltpu.VMEM((FOX_HEADS, tq), F32),
                        pltpu.VMEM((FOX_WIDTH, tq), F32)],
        compiler_params=_params("parallel", "arbitrary"),
        name="fox_mix_prompt",
    )(qt, kb, vt, c_col, c_row, rest, x, lw["w_out"], lw["g_fox"], lw["g_post_mix"])


def _fox_sample_kernel(q_ref, kn_ref, vn_ref, hk_ref, hv_ref, hrow_ref, ncol_ref, nrow_ref, rest_ref, x_ref,
                       wout_ref, gfox_ref, gpost_ref, o_ref, fox_scr):
    t = q_ref.shape[0]
    past = hk_ref.shape[0]
    tk = ATTN_K_ROWS
    b = pl.program_id(0)
    lo = lax.broadcasted_iota(jnp.int32, (t, LANES), 1) < HEAD_DIM
    zero_q = jnp.zeros((t, LANES), BF16)
    causal = lax.broadcasted_iota(jnp.int32, (t, t), 1) <= lax.broadcasted_iota(jnp.int32, (t, t), 0)
    fox_parts = []
    for p in range(FOX_HEADS // HEAD_PAIR):
        lanes = slice(p * LANES, (p + 1) * LANES)
        qp = q_ref[:, lanes]
        q0 = jnp.where(lo, qp, zero_q)
        q1 = jnp.where(lo, zero_q, qp)
        bq0 = ncol_ref[:, 2 * p:2 * p + 1]
        bq1 = ncol_ref[:, 2 * p + 1:2 * p + 2]
        carry = _pair_init(t)
        for j in range(past // tk):
            rows = slice(j * tk, (j + 1) * tk)
            carry = _attend_pair_tile(q0, q1, hk_ref[rows, lanes].astype(BF16), hv_ref[rows, lanes].astype(BF16),
                                      bq0, bq1, hrow_ref[2 * p:2 * p + 1, rows],
                                      hrow_ref[2 * p + 1:2 * p + 2, rows], carry, lo, None)
        _, l0, _, l1, acc = _attend_pair_tile(q0, q1, kn_ref[:, lanes], vn_ref[:, lanes], bq0, bq1,
                                              nrow_ref[2 * p:2 * p + 1, :t], nrow_ref[2 * p + 1:2 * p + 2, :t],
                                              carry, lo, causal)
        fox_parts.append(acc / jnp.where(lo, l0, l1))
    fox_scr[pl.ds(pl.multiple_of(b * t, t), t), :] = jnp.concatenate(fox_parts, axis=1)

    @pl.when(b == pl.num_programs(0) - 1)
    def _():
        o_ref[...] = _mix_out(fox_scr[...], rest_ref[...], x_ref[...], wout_ref, gfox_ref, gpost_ref)


def _fox_sample(q, kb, vb, hk, hv, layer, hist_row, new_col, new_row, rest, x, lw):
    nb, t, _ = q.shape
    past = hk.shape[2]
    per_b = lambda a: pl.BlockSpec((None,) + a.shape[1:], lambda b: (b,) + (0,) * (a.ndim - 1))
    cache = pl.BlockSpec((None, None, past, FOX_WIDTH), lambda b: (layer, b, 0, 0))
    const = lambda a: pl.BlockSpec(a.shape, lambda b: (0,) * a.ndim)
    return pl.pallas_call(
        _fox_sample_kernel,
        grid=(nb,),
        in_specs=[per_b(q), per_b(kb), per_b(vb), cache, cache, per_b(hist_row), per_b(new_col),
                  per_b(new_row), const(rest), const(x), const(lw["w_out"]), const(lw["g_fox"]),
                  const(lw["g_post_mix"])],
        out_specs=const(x),
        out_shape=jax.ShapeDtypeStruct(x.shape, F32),
        scratch_shapes=[pltpu.VMEM((nb * t, FOX_WIDTH), F32)],
        compiler_params=_params("arbitrary"),
        name="fox_mix_sample",
    )(q, kb, vb, hk, hv, hist_row, new_col, new_row, rest, x, lw["w_out"], lw["g_fox"], lw["g_post_mix"])


def _ffn_body(x_ref, gpre_ref, wup_ref, wdw_ref, bdw_ref, wd_ref, gpost_ref, o_ref, g_scr, shifted, emit):
    h2 = _rms(x_ref[...], gpre_ref[...]).astype(BF16)
    nf = FFN_DIM // FFN_COLS

    def up(c):
        conv_cols = slice(c * FFN_COLS, (c + 1) * FFN_COLS)
        lin_cols = slice(FFN_DIM + c * FFN_COLS, FFN_DIM + (c + 1) * FFN_COLS)
        return _dot(h2, wup_ref[:, conv_cols]), _dot(h2, wup_ref[:, lin_cols])

    nxt = up(0)
    for c in range(nf):
        a, lin = nxt
        if c + 1 < nf:
            nxt = up(c + 1)
        cols = slice(c * FFN_COLS, (c + 1) * FFN_COLS)
        a1, a2 = shifted(a, c)
        conv = bdw_ref[:, cols] + wdw_ref[0:1, cols] * a2
        conv = conv + wdw_ref[1:2, cols] * a1
        conv = conv + wdw_ref[2:3, cols] * a
        g_scr[:, cols] = (jax.nn.silu(conv) * lin).astype(BF16)
        emit(a, c)
    o_ref[...] = x_ref[...] + _rms(_dot(g_scr[...], wd_ref[...]), gpost_ref[...])


def _ffn_prompt_kernel(x_ref, gpre_ref, wup_ref, wdw_ref, bdw_ref, wd_ref, gpost_ref, hist_ref,
                       o_ref, tail_ref, g_scr, carry_scr, work_scr):
    tm = x_ref.shape[0]
    head = SUBLANES

    @pl.when(pl.program_id(1) == 0)
    def _():
        carry_scr[0:head - (CONV_WIDTH - 1), :] = jnp.zeros((head - (CONV_WIDTH - 1), FFN_DIM), F32)
        carry_scr[head - (CONV_WIDTH - 1):head, :] = hist_ref[...]

    def shifted(a, c):
        cols = slice(c * FFN_COLS, (c + 1) * FFN_COLS)
        work = work_scr.at[c % 2]
        work[0:head, :] = carry_scr[:, cols]
        work[head:head + tm, :] = a
        return work[head - 1:head - 1 + tm, :], work[head - 2:head - 2 + tm, :]

    def emit(a, c):
        cols = slice(c * FFN_COLS, (c + 1) * FFN_COLS)
        carry_scr[:, cols] = a[tm - head:, :]
        tail_ref[:, cols] = a[tm - head:, :]

    _ffn_body(x_ref, gpre_ref, wup_ref, wdw_ref, bdw_ref, wd_ref, gpost_ref, o_ref, g_scr, shifted, emit)


def _ffn_sample_kernel(x_ref, gpre_ref, wup_ref, wdw_ref, bdw_ref, wd_ref, gpost_ref, e1_ref, e2_ref,
                       o_ref, a_ref, g_scr, work_scr, *, seg):
    tm = x_ref.shape[0]
    head = SUBLANES
    rmod = lax.broadcasted_iota(jnp.int32, (tm, FFN_COLS), 0) % seg

    def shifted(a, c):
        cols = slice(c * FFN_COLS, (c + 1) * FFN_COLS)
        work = work_scr.at[c % 2]
        work[0:head, :] = jnp.zeros((head, FFN_COLS), F32)
        work[head:head + tm, :] = a
        a1 = jnp.where(rmod >= 1, work[head - 1:head - 1 + tm, :], e1_ref[:, cols])
        a2 = jnp.where(rmod >= 2, work[head - 2:head - 2 + tm, :], e2_ref[:, cols])
        return a1, a2

    def emit(a, c):
        a_ref[:, c * FFN_COLS:(c + 1) * FFN_COLS] = a

    _ffn_body(x_ref, gpre_ref, wup_ref, wdw_ref, bdw_ref, wd_ref, gpost_ref, o_ref, g_scr, shifted, emit)


def _ffn_weight_specs():
    zeros = lambda nd: (lambda *_: (0,) * nd)
    once = lambda shape: pl.BlockSpec(shape, zeros(len(shape)), pipeline_mode=pl.Buffered(1))
    return [
        once((1, D_MODEL)),
        once((D_MODEL, 2 * FFN_DIM)),
        once((CONV_WIDTH, FFN_DIM)),
        once((1, FFN_DIM)),
        once((FFN_DIM, D_MODEL)),
        once((1, D_MODEL)),
    ]


def _ffn_weights(lw):
    return (lw["g_pre_ffn"], lw["w_up"], lw["w_dwconv"], lw["b_dwconv"], lw["w_down"], lw["g_post_ffn"])


def _ffn_prompt(x, hist, lw):
    nb, s, _ = x.shape
    tm = FFN_ROWS
    nt = s // tm
    return pl.pallas_call(
        _ffn_prompt_kernel,
        grid=(nb, nt),
        in_specs=[pl.BlockSpec((None, tm, D_MODEL), lambda b, i: (b, i, 0))] + _ffn_weight_specs() + [
            pl.BlockSpec((None, CONV_WIDTH - 1, FFN_DIM), lambda b, i: (b, 0, 0))],
        out_specs=[pl.BlockSpec((None, tm, D_MODEL), lambda b, i: (b, i, 0)),
                   pl.BlockSpec((None, None, SUBLANES, FFN_DIM), lambda b, i: (b, i, 0, 0))],
        out_shape=[jax.ShapeDtypeStruct(x.shape, F32),
                   jax.ShapeDtypeStruct((nb, nt, SUBLANES, FFN_DIM), F32)],
        scratch_shapes=[pltpu.VMEM((tm, FFN_DIM), BF16),
                        pltpu.VMEM((SUBLANES, FFN_DIM), F32),
                        pltpu.VMEM((2, tm + SUBLANES, FFN_COLS), F32)],
        compiler_params=_params("parallel", "arbitrary"),
        name="conv_ffn_prompt",
    )(x, *_ffn_weights(lw), hist)


def _ffn_sample(x, e1, e2, lw, seg):
    rows, _ = x.shape
    whole = lambda w: pl.BlockSpec((rows, w), lambda i: (0, 0))
    return pl.pallas_call(
        functools.partial(_ffn_sample_kernel, seg=seg),
        grid=(1,),
        in_specs=[whole(D_MODEL)] + _ffn_weight_specs() + [whole(FFN_DIM), whole(FFN_DIM)],
        out_specs=[whole(D_MODEL), whole(FFN_DIM)],
        out_shape=[jax.ShapeDtypeStruct(x.shape, F32), jax.ShapeDtypeStruct((rows, FFN_DIM), F32)],
        scratch_shapes=[pltpu.VMEM((rows, FFN_DIM), BF16),
                        pltpu.VMEM((2, rows + SUBLANES, FFN_COLS), F32)],
        compiler_params=_params("arbitrary"),
        name="conv_ffn_sample",
    )(x, *_ffn_weights(lw), e1, e2)


def _layer_weights(l, g_pre_mix, w_in, b_forget, w_spatial, b_spatial, g_sgu, g_group_out, w_out, g_post_mix,
                   g_pre_ffn, w_up, w_dwconv, b_dwconv, w_down, g_post_ffn, dec_seq):
    row = lambda a: a[l].reshape(1, -1)
    wi = w_in[l]
    c0 = 3 * FOX_WIDTH
    c1 = c0 + FOX_HEADS
    c2 = c1 + 2 * SGU_WIDTH
    w_b = jnp.concatenate([wi[:, c1:c2], wi[:, c2:], wi[:, c0:c1],
                           jnp.zeros((D_MODEL, B_COLS - (2 * SGU_WIDTH + MEM_WIDTH + FOX_HEADS)), F32)], axis=1)
    ws = w_spatial[l]
    bs = b_spatial[l]
    group_dim = SGU_WIDTH // SGU_GROUPS
    ws_prompt = jnp.concatenate([ws[g] for g in range(SGU_GROUPS)], axis=1)
    bt_prompt = jnp.repeat(bs.T, group_dim, axis=1)
    reps = GMLP_CHUNK // dec_seq
    blk = (jnp.arange(GMLP_CHUNK)[:, None] // dec_seq) == (jnp.arange(GMLP_CHUNK)[None, :] // dec_seq)
    ws_sample = jnp.concatenate(
        [jnp.where(blk, jnp.tile(ws[g, :dec_seq, :dec_seq], (reps, reps)), 0.0) for g in range(SGU_GROUPS)], axis=1)
    bt_sample = jnp.tile(jnp.repeat(bs[:, :dec_seq].T, group_dim, axis=1), (reps, 1))
    return {
        "g_pre_mix": row(g_pre_mix), "w_qkv": wi[:, :c0].astype(BF16), "w_b": w_b.astype(BF16),
        "b_forget": row(b_forget), "g_sgu": row(g_sgu),
        "g_fox": row(g_group_out)[:, :FOX_WIDTH], "g_rest": row(g_group_out)[:, FOX_WIDTH:],
        "w_out": w_out[l].astype(BF16), "g_post_mix": row(g_post_mix), "g_pre_ffn": row(g_pre_ffn),
        "w_up": w_up[l].astype(BF16), "w_dwconv": w_dwconv[l], "b_dwconv": row(b_dwconv),
        "w_down": w_down[l].astype(BF16), "g_post_ffn": row(g_post_ffn),
        "ws_prompt": ws_prompt, "bt_prompt": bt_prompt, "ws_sample": ws_sample, "bt_sample": bt_sample,
    }


def kernel(x_prompt, x_sample, mem_prompt, cache_fox_k, cache_fox_v, cache_fox_logf, cache_mem_k, cache_mem_v,
           cache_ffn_conv, g_pre_mix, w_in, b_forget, w_spatial, b_spatial, g_sgu, g_mem, w_mem_kv, g_group_out,
           w_out, g_post_mix, g_pre_ffn, w_up, w_dwconv, b_dwconv, w_down, g_post_ffn):
    depth = w_in.shape[0]
    batch, seq, _ = x_prompt.shape
    dec_batch, dec_seq, _ = x_sample.shape
    past = cache_fox_k.shape[2]
    dec_rows = dec_batch * dec_seq

    mem_k_all, mem_v_all = _memory_kv(mem_prompt, g_mem, w_mem_kv.astype(BF16))
    flat_mem = lambda a: a.reshape(-1, N_MEM, MEM_WIDTH)
    pmk, pmv = flat_mem(mem_k_all), flat_mem(mem_v_all)
    smk, smv = flat_mem(cache_mem_k), flat_mem(cache_mem_v)
    hk = cache_fox_k.reshape(depth, dec_batch, past, FOX_WIDTH)
    hv = cache_fox_v.reshape(depth, dec_batch, past, FOX_WIDTH)

    yp = x_prompt
    ys = x_sample.reshape(1, dec_rows, D_MODEL)
    zeros_hist = jnp.zeros((batch, CONV_WIDTH - 1, FFN_DIM), F32)
    outs = [[] for _ in range(11)]
    for l in range(depth):
        lw = _layer_weights(l, g_pre_mix, w_in, b_forget, w_spatial, b_spatial, g_sgu, g_group_out, w_out,
                            g_post_mix, g_pre_ffn, w_up, w_dwconv, b_dwconv, w_down, g_post_ffn, dec_seq)

        qt, k, v, kb, vt, logf, rest = _in_proj(yp, pmk, pmv, l * batch, 1, lw, IN_PROJ_ROWS,
                                                lw["ws_prompt"], lw["bt_prompt"], True)
        logf_row = jnp.swapaxes(logf, 1, 2).reshape(batch * FOX_HEADS, seq)
        c_row = _lane_cumsum(logf_row).reshape(batch, FOX_HEADS, seq)
        c_col = jnp.swapaxes(c_row, 1, 2)
        yp = _fox_prompt(qt, kb, vt, c_col, c_row, rest, yp, lw)
        yp, tail = _ffn_prompt(yp, zeros_hist, lw)
        outs[0].append(k.reshape(batch, seq, FOX_HEADS, HEAD_DIM))
        outs[1].append(v.reshape(batch, seq, FOX_HEADS, HEAD_DIM))
        outs[2].append(logf)
        outs[3].append(mem_k_all[l].reshape(batch, N_MEM, MEM_HEADS, HEAD_DIM))
        outs[4].append(mem_v_all[l].reshape(batch, N_MEM, MEM_HEADS, HEAD_DIM))
        outs[5].append(tail[:, -1, SUBLANES - (CONV_WIDTH - 1):, :])

        q, k, v, kb, vb, logf, rest, vrows = _in_proj(ys, smk, smv, l * dec_batch, dec_batch, lw, dec_rows,
                                                      lw["ws_sample"], lw["bt_sample"], False)
        per_b = lambda a: a.reshape(dec_batch, dec_seq, a.shape[-1])
        hist_row = _lane_cumsum(jnp.swapaxes(cache_fox_logf[l], 1, 2).reshape(dec_batch * FOX_HEADS, past),
                                minus_total=True).reshape(dec_batch, FOX_HEADS, past)
        logf_new_row = jnp.swapaxes(per_b(logf), 1, 2).reshape(dec_batch * FOX_HEADS, dec_seq)
        new_row = _lane_cumsum(jnp.pad(logf_new_row, ((0, 0), (0, LANES - dec_seq)))
                               ).reshape(dec_batch, FOX_HEADS, LANES)
        new_col = jnp.swapaxes(new_row[:, :, :dec_seq], 1, 2)
        ys2 = _fox_sample(per_b(q[0]), per_b(kb[0]), per_b(vb[0]), hk, hv, l, hist_row, new_col, new_row,
                          rest[0], ys[0], lw)
        conv_hist = cache_ffn_conv[l]
        pad_rows = lambda a: jnp.pad(a, ((0, 0), (0, dec_seq - a.shape[1]), (0, 0))).reshape(dec_rows, FFN_DIM)
        e1 = pad_rows(conv_hist[:, 1:2, :])
        e2 = pad_rows(conv_hist)
        ys_flat, a_all = _ffn_sample(ys2, e1, e2, lw, dec_seq)
        ys = ys_flat.reshape(1, dec_rows, D_MODEL)
        outs[6].append(per_b(k[0]).reshape(dec_batch, dec_seq, FOX_HEADS, HEAD_DIM))
        outs[7].append(per_b(v[0]).reshape(dec_batch, dec_seq, FOX_HEADS, HEAD_DIM))
        outs[8].append(per_b(logf[0]))
        outs[9].append(per_b(vrows[0]))
        outs[10].append(a_all.reshape(dec_batch, dec_seq, FFN_DIM)[:, dec_seq - (CONV_WIDTH - 1):, :])

    stacked = [jnp.stack(o) for o in outs]
    return (yp, ys.reshape(dec_batch, dec_seq, D_MODEL), *stacked)
```

```python
import functools

import jax
import jax.numpy as jnp
from jax import lax
from jax.experimental import pallas as pl
from jax.experimental.pallas import tpu as pltpu

D_MODEL = 1024
HEAD_DIM = 64
FOX_WIDTH = 512
FOX_HEADS = 8
SGU_WIDTH = 256
SGU_GROUPS = 4
GMLP_CHUNK = 128
CHUNK = 64
MEM_WIDTH = 256
MEM_HEADS = 4
N_MEM = 256
FFN_DIM = 2816
CONV_WIDTH = 3
RMS_EPS = 1e-6
NEG_INF = -1e30
QK_SCALE = HEAD_DIM ** -0.5

LANES = 128
SUBLANES = 8
HEAD_PAIR = LANES // HEAD_DIM
VMEM_LIMIT_BYTES = 56 * 1024 * 1024

IN_PROJ_ROWS = 512
ATTN_Q_ROWS = 256
ATTN_K_ROWS = 256
FFN_ROWS = 512
FFN_COLS = 256
B_COLS = 896

BF16 = jnp.bfloat16
F32 = jnp.float32


def _rms(x, g):
    y = x * lax.rsqrt(jnp.mean(x * x, axis=-1, keepdims=True) + RMS_EPS)
    return y * g


def _dot(a, b):
    return jnp.dot(a, b, preferred_element_type=F32)


def _dot_nt(a, b):
    return lax.dot_general(a, b, (((1,), (1,)), ((), ())), preferred_element_type=F32)


def _params(*semantics):
    return pltpu.CompilerParams(dimension_semantics=semantics, vmem_limit_bytes=VMEM_LIMIT_BYTES)


def _memkv_kernel(mem_ref, g_ref, w_ref, mk_ref, mv_ref):
    h = _rms(mem_ref[...], g_ref[...]).astype(BF16)
    kv = _dot(h, w_ref[...])
    mk_ref[...] = kv[:, :MEM_WIDTH]
    mv_ref[...] = kv[:, MEM_WIDTH:]


def _memory_kv(mem, g_mem, w_mem_kv_bf):
    depth = g_mem.shape[0]
    batch = mem.shape[0]
    out = jax.ShapeDtypeStruct((depth, batch, N_MEM, MEM_WIDTH), F32)
    return pl.pallas_call(
        _memkv_kernel,
        grid=(depth, batch),
        in_specs=[
            pl.BlockSpec((None, N_MEM, D_MODEL), lambda l, b: (b, 0, 0)),
            pl.BlockSpec((None, 1, D_MODEL), lambda l, b: (l, 0, 0)),
            pl.BlockSpec((None, D_MODEL, 2 * MEM_WIDTH), lambda l, b: (l, 0, 0)),
        ],
        out_specs=[
            pl.BlockSpec((None, None, N_MEM, MEM_WIDTH), lambda l, b: (l, b, 0, 0)),
            pl.BlockSpec((None, None, N_MEM, MEM_WIDTH), lambda l, b: (l, b, 0, 0)),
        ],
        out_shape=[out, out],
        compiler_params=_params("arbitrary", "arbitrary"),
        name="memory_kv",
    )(mem, g_mem.reshape(depth, 1, D_MODEL), w_mem_kv_bf)


def _inproj_kernel(x_ref, gpre_ref, wqkv_ref, wb_ref, bfg_ref, ws_ref, bt_ref, gsgu_ref, gg_ref,
                   mk_ref, mv_ref, *out_refs, transposed, n_stack_in):
    out_refs = out_refs[n_stack_in:]
    tm = x_ref.shape[0]
    n_mem_blocks = mk_ref.shape[0]
    if transposed:
        qt_ref, k_ref, v_ref, kb_ref, vt_ref, lf_ref, rest_ref = out_refs
        vrows_ref = None
    else:
        q_ref, k_ref, v_ref, kb_ref, vb_ref, lf_ref, rest_ref, vrows_ref = out_refs
    h = _rms(x_ref[...], gpre_ref[...]).astype(BF16)

    yb = _dot(h, wb_ref[...])
    fg = yb[:, 2 * SGU_WIDTH + MEM_WIDTH:2 * SGU_WIDTH + MEM_WIDTH + FOX_HEADS]
    lf_ref[...] = jax.nn.log_sigmoid(fg + bfg_ref[...])

    qmb = (yb[:, 2 * SGU_WIDTH:2 * SGU_WIDTH + MEM_WIDTH] * QK_SCALE).astype(BF16)
    rows = tm // n_mem_blocks
    hlane = lax.broadcasted_iota(jnp.int32, (rows, MEM_WIDTH), 1) // HEAD_DIM
    zero_q = jnp.zeros((rows, MEM_WIDTH), BF16)
    mem_scores = []
    for bi in range(n_mem_blocks):
        qb = qmb[bi * rows:(bi + 1) * rows, :]
        mkb = mk_ref[bi].astype(BF16)
        mem_scores.append([_dot_nt(jnp.where(hlane == hd, qb, zero_q), mkb) for hd in range(MEM_HEADS)])

    q = _dot(h, wqkv_ref[:, :FOX_WIDTH]) * QK_SCALE

    z = jax.nn.gelu(yb[:, :2 * SGU_WIDTH])
    u = z[:, :SGU_WIDTH]
    vv = _rms(z[:, SGU_WIDTH:], gsgu_ref[...])
    if vrows_ref is not None:
        vrows_ref[...] = vv
    vvb = vv.astype(BF16)
    mem_exp = [[jnp.exp(s - jnp.max(s, axis=-1, keepdims=True)) for s in per_block] for per_block in mem_scores]

    wrow = lax.broadcasted_iota(jnp.int32, (GMLP_CHUNK, SGU_GROUPS * GMLP_CHUNK), 0)
    wcol = lax.broadcasted_iota(jnp.int32, (GMLP_CHUNK, SGU_GROUPS * GMLP_CHUNK), 1)
    wmask = ((wcol % GMLP_CHUNK) // CHUNK) <= (wrow // CHUNK)
    wcat = jnp.where(wmask, ws_ref[...], 0.0).astype(BF16)
    glane = lax.broadcasted_iota(jnp.int32, (GMLP_CHUNK, SGU_WIDTH), 1) // (SGU_WIDTH // SGU_GROUPS)
    zero_chunk = jnp.zeros((GMLP_CHUNK, SGU_WIDTH), BF16)
    mixed_parts = []
    for c in range(tm // GMLP_CHUNK):
        vc = vvb[c * GMLP_CHUNK:(c + 1) * GMLP_CHUNK, :]
        rhs = jnp.concatenate([jnp.where(glane == g, vc, zero_chunk) for g in range(SGU_GROUPS)], axis=0)
        mixed_parts.append(_dot(wcat, rhs))
    mem_pv = []
    for bi in range(n_mem_blocks):
        mvb = mv_ref[bi].astype(BF16)
        mem_pv.append([_dot(e.astype(BF16), mvb) for e in mem_exp[bi]])

    k = _dot(h, wqkv_ref[:, FOX_WIDTH:2 * FOX_WIDTH])

    if transposed:
        qt_ref[...] = q.T.astype(BF16)
    else:
        q_ref[...] = q.astype(BF16)
    sgu = jnp.concatenate([u[c * GMLP_CHUNK:(c + 1) * GMLP_CHUNK, :] * (mixed + bt_ref[...])
                           for c, mixed in enumerate(mixed_parts)], axis=0)
    rest_ref[:, :SGU_WIDTH] = _rms(sgu, gg_ref[:, :SGU_WIDTH]).astype(BF16)
    mem_parts = []
    for bi in range(n_mem_blocks):
        out = jnp.zeros((rows, MEM_WIDTH), F32)
        for hd in range(MEM_HEADS):
            o = mem_pv[bi][hd] / jnp.sum(mem_exp[bi][hd], axis=-1, keepdims=True)
            out = jnp.where(hlane == hd, o, out)
        mem_parts.append(out)
    mem = mem_parts[0] if n_mem_blocks == 1 else jnp.concatenate(mem_parts, axis=0)
    rest_ref[:, SGU_WIDTH:] = _rms(mem, gg_ref[:, SGU_WIDTH:]).astype(BF16)

    v = _dot(h, wqkv_ref[:, 2 * FOX_WIDTH:])

    k_ref[...] = k
    kb_ref[...] = k.astype(BF16)
    v_ref[...] = v
    if transposed:
        vt = v.T.astype(BF16)
        tk = vt_ref.shape[-1]
        for c in range(tm // tk):
            vt_ref[c] = vt[:, c * tk:(c + 1) * tk]
    else:
        vb_ref[...] = v.astype(BF16)


def _in_proj(x, mem_k, mem_v, mem_first, mem_per_tile, lw, tm, spatial_w, spatial_b, transposed,
             layer=0, depth=1, stacked_kv=()):
    nb, t, _ = x.shape
    nt = t // tm
    if transposed:
        assert mem_per_tile == 1
        mem_index = lambda b, i: (mem_first + b, 0, 0)
    else:
        mem_index = lambda b, i: (mem_first // mem_per_tile + b * nt + i, 0, 0)
    row = lambda w: pl.BlockSpec((None, tm, w), lambda b, i: (b, i, 0))
    const = lambda a: pl.BlockSpec(a.shape, lambda b, i: (0,) * a.ndim)
    memspec = pl.BlockSpec((mem_per_tile, N_MEM, MEM_WIDTH), mem_index)
    f32 = lambda w: jax.ShapeDtypeStruct((nb, t, w), F32)
    bf = lambda w: jax.ShapeDtypeStruct((nb, t, w), BF16)
    consts = (lw["g_pre_mix"], lw["w_qkv"], lw["w_b"], lw["b_forget"], spatial_w, spatial_b,
              lw["g_sgu"], lw["g_rest"])
    if transposed:
        tk = ATTN_K_ROWS
        stack_spec = pl.BlockSpec((None, None, tm, FOX_WIDTH), lambda b, i: (layer, b, i, 0))
        stack_shape = jax.ShapeDtypeStruct((depth, nb, t, FOX_WIDTH), F32)
        out_specs = [pl.BlockSpec((None, FOX_WIDTH, tm), lambda b, i: (b, 0, i)),
                     stack_spec, stack_spec, row(FOX_WIDTH),
                     pl.BlockSpec((None, tm // tk, FOX_WIDTH, tk), lambda b, i: (b, i, 0, 0)),
                     row(FOX_HEADS), row(SGU_WIDTH + MEM_WIDTH)]
        out_shape = [jax.ShapeDtypeStruct((nb, FOX_WIDTH, t), BF16), stack_shape, stack_shape,
                     bf(FOX_WIDTH), jax.ShapeDtypeStruct((nb, t // tk, FOX_WIDTH, tk), BF16),
                     f32(FOX_HEADS), bf(SGU_WIDTH + MEM_WIDTH)]
    else:
        out_specs = [row(FOX_WIDTH)] * 5 + [row(FOX_HEADS), row(SGU_WIDTH + MEM_WIDTH), row(SGU_WIDTH)]
        out_shape = [bf(FOX_WIDTH), f32(FOX_WIDTH), f32(FOX_WIDTH), bf(FOX_WIDTH), bf(FOX_WIDTH),
                     f32(FOX_HEADS), bf(SGU_WIDTH + MEM_WIDTH), f32(SGU_WIDTH)]
    in_specs = [row(D_MODEL)] + [const(a) for a in consts] + [memspec, memspec]
    aliases = {len(in_specs) + n: 1 + n for n in range(len(stacked_kv))}
    in_specs += [pl.BlockSpec(memory_space=pl.ANY)] * len(stacked_kv)
    return pl.pallas_call(
        functools.partial(_inproj_kernel, transposed=transposed, n_stack_in=len(stacked_kv)),
        grid=(nb, nt),
        in_specs=in_specs,
        out_specs=out_specs,
        out_shape=out_shape,
        input_output_aliases=aliases,
        compiler_params=_params("parallel", "arbitrary"),
        name="in_proj",
    )(x, *consts, mem_k, mem_v, *stacked_kv)


def _cumsum_kernel(x_ref, o_ref, *, minus_total):
    x = x_ref[...]
    n = x.shape[-1]
    lane = lax.broadcasted_iota(jnp.int32, x.shape, 1)
    shift = 1
    while shift < n:
        x = x + jnp.where(lane >= shift, pltpu.roll(x, shift, 1), 0.0)
        shift *= 2
    if minus_total:
        x = x - x[:, n - 1:n]
    o_ref[...] = x


def _lane_cumsum(x, minus_total=False):
    return pl.pallas_call(
        functools.partial(_cumsum_kernel, minus_total=minus_total),
        out_shape=jax.ShapeDtypeStruct(x.shape, F32),
        name="lane_cumsum",
    )(x)


def _mix_out(fox, rest, x, wout_ref, gfox_ref, gpost_ref):
    fox_n = _rms(fox, gfox_ref[...]).astype(BF16)
    y = _dot(fox_n, wout_ref[:FOX_WIDTH, :]) + _dot(rest, wout_ref[FOX_WIDTH:, :])
    return x + _rms(y, gpost_ref[...])


def _split3(c):
    hi = c.astype(BF16).astype(F32)
    r = c - hi
    mid = r.astype(BF16).astype(F32)
    lo = (r - mid).astype(BF16).astype(F32)
    return hi, mid, lo


_BIAS_ONES = 6


def _fox_prompt_kernel(qt_ref, kb_ref, vt_ref, ccol_ref, crow_ref, rest_ref, x_ref, wout_ref, gfox_ref,
                       gpost_ref, o_ref, kaug_scr, qa_scr, sa_scr, sb_scr, m_scr, l_scr, acc_scr):
    tq = qt_ref.shape[1]
    tk = vt_ref.shape[-1]
    s_len = kb_ref.shape[0]
    slab = 2 * LANES
    i = pl.program_id(1)

    @pl.when(i == 0)
    def _():
        lane = lax.broadcasted_iota(jnp.int32, (tk, LANES), 1)
        ones = jnp.where((lane >= _BIAS_ONES) & (lane < _BIAS_ONES + 3), 1.0, 0.0)
        for r in range(s_len // tk):
            rows = slice(r * tk, (r + 1) * tk)
            cc = ccol_ref[rows, :]
            for p in range(FOX_HEADS // HEAD_PAIR):
                kaug_scr[rows, p * slab:p * slab + LANES] = kb_ref[rows, p * LANES:(p + 1) * LANES]
                blk = ones
                pieces = _split3(cc[:, 2 * p:2 * p + 1]) + _split3(cc[:, 2 * p + 1:2 * p + 2])
                for n, piece in enumerate(pieces):
                    blk = jnp.where(lane == n, piece, blk)
                kaug_scr[rows, p * slab + LANES:(p + 1) * slab] = blk.astype(BF16)

    rowi = lax.broadcasted_iota(jnp.int32, (LANES, tq), 0)
    diag = lax.broadcasted_iota(jnp.int32, (tk, tq), 0) <= lax.broadcasted_iota(jnp.int32, (tk, tq), 1)
    crow = crow_ref[...]
    zero_q = jnp.zeros((LANES, tq), BF16)
    for p in range(FOX_HEADS // HEAD_PAIR):
        qtp = qt_ref[p * LANES:(p + 1) * LANES, :]
        for hh in range(HEAD_PAIR):
            own = (rowi >= hh * HEAD_DIM) & (rowi < (hh + 1) * HEAD_DIM)
            chi, cmid, clo = _split3(crow[2 * p + hh:2 * p + hh + 1, :])
            br = jnp.where(rowi == _BIAS_ONES, chi,
                           jnp.where(rowi == _BIAS_ONES + 1, cmid, jnp.where(rowi == _BIAS_ONES + 2, clo, 0.0)))
            br = jnp.where((rowi >= 3 * hh) & (rowi < 3 * hh + 3), -1.0, br)
            qa_scr[2 * p + hh, :LANES, :] = jnp.where(own, qtp, zero_q)
            qa_scr[2 * p + hh, LANES:, :] = br.astype(BF16)

    m_scr[...] = jnp.full(m_scr.shape, NEG_INF, F32)
    l_scr[...] = jnp.zeros(l_scr.shape, F32)
    acc_scr[...] = jnp.zeros(acc_scr.shape, F32)

    def scores_to(buf_ref, j):
        off = pl.multiple_of(j * tk, tk)
        for h in range(FOX_HEADS):
            ka = kaug_scr[pl.ds(off, tk), (h // HEAD_PAIR) * slab:(h // HEAD_PAIR + 1) * slab]
            buf_ref[h] = _dot(ka, qa_scr[h])

    def absorb_from(buf_ref, j, mask):
        vt = vt_ref[j]
        for h in range(FOX_HEADS):
            feat = slice(h * HEAD_DIM, (h + 1) * HEAD_DIM)
            s = buf_ref[h]
            if mask is not None:
                s = jnp.where(mask, s, NEG_INF)
            m = m_scr[h:h + 1, :]
            m_new = jnp.maximum(m, jnp.max(s, axis=0, keepdims=True))
            alpha = jnp.exp(m - m_new)
            e = jnp.exp(s - m_new)
            m_scr[h:h + 1, :] = m_new
            l_scr[h:h + 1, :] = alpha * l_scr[h:h + 1, :] + jnp.sum(e, axis=0, keepdims=True)
            acc_scr[feat, :] = alpha * acc_scr[feat, :] + _dot(vt[feat, :], e.astype(BF16))

    odd = i % 2

    @pl.when(odd == 1)
    def _():
        scores_to(sb_scr, 0)
        scores_to(sa_scr, 1)
        absorb_from(sb_scr, 0, None)

    @pl.when(odd == 0)
    def _():
        scores_to(sa_scr, 0)

    def step(jj, _):
        t = odd + 2 * jj
        scores_to(sb_scr, t + 1)
        absorb_from(sa_scr, t, None)
        scores_to(sa_scr, t + 2)
        absorb_from(sb_scr, t + 1, None)
        return 0

    lax.fori_loop(0, i // 2, step, 0)
    absorb_from(sa_scr, i, diag)
    fox_t = jnp.concatenate([acc_scr[h * HEAD_DIM:(h + 1) * HEAD_DIM, :] / l_scr[h:h + 1, :]
                             for h in range(FOX_HEADS)], axis=0)
    fox = fox_t.T
    o_ref[...] = _mix_out(fox, rest_ref[...], x_ref[...], wout_ref, gfox_ref, gpost_ref)


def _fox_prompt(qt, kb, vt, c_col, c_row, rest, x, lw):
    nb, s, _ = x.shape
    tq, tk = ATTN_Q_ROWS, ATTN_K_ROWS
    assert tq == tk and vt.shape[-1] == tk
    nq = s // tq
    qrow = lambda w: pl.BlockSpec((None, tq, w), lambda b, i: (b, i, 0))
    qcol = lambda r: pl.BlockSpec((None, r, tq), lambda b, i: (b, 0, i))
    full = lambda a: pl.BlockSpec((None,) + a.shape[1:], lambda b, i: (b,) + (0,) * (a.ndim - 1))
    const = lambda a: pl.BlockSpec(a.shape, lambda b, i: (0,) * a.ndim)
    return pl.pallas_call(
        _fox_prompt_kernel,
        grid=(nb, nq),
        in_specs=[qcol(FOX_WIDTH), full(kb), full(vt), full(c_col), qcol(FOX_HEADS),
                  qrow(SGU_WIDTH + MEM_WIDTH), qrow(D_MODEL),
                  const(lw["w_out"]), const(lw["g_fox"]), const(lw["g_post_mix"])],
        out_specs=qrow(D_MODEL),
        out_shape=jax.ShapeDtypeStruct(x.shape, F32),
        scratch_shapes=[pltpu.VMEM((s, 2 * FOX_WIDTH), BF16),
                        pltpu.VMEM((FOX_HEADS, 2 * LANES, tq), BF16),
                        pltpu.VMEM((FOX_HEADS, tk, tq), F32), pltpu.VMEM((FOX_HEADS, tk, tq), F32),
                        pltpu.VMEM((FOX_HEADS, tq), F32), pltpu.VMEM((FOX_HEADS, tq), F32),
                        pltpu.VMEM((FOX_WIDTH, tq), F32)],
        compiler_params=_params("parallel", "arbitrary"),
        name="fox_mix_prompt",
    )(qt, kb, vt, c_col, c_row, rest, x, lw["w_out"], lw["g_fox"], lw["g_post_mix"])


def _fox_sample_kernel(q_ref, kn_ref, vn_ref, hk_ref, hv_ref, hrow_ref, ncol_ref, nrow_ref, rest_ref, x_ref,
                       wout_ref, gfox_ref, gpost_ref, o_ref, fox_scr):
    t = q_ref.shape[0]
    b = pl.program_id(0)
    rows = FOX_HEADS * t
    qt = jnp.concatenate([q_ref[...]] * FOX_HEADS, axis=0)
    row_head = lax.broadcasted_iota(jnp.int32, (rows, FOX_WIDTH), 0) // t
    lane_head = lax.broadcasted_iota(jnp.int32, (rows, FOX_WIDTH), 1) // HEAD_DIM
    own = row_head == lane_head
    qb = jnp.where(own, qt, jnp.zeros_like(qt))
    per_head = lambda a: jnp.concatenate(
        [jnp.broadcast_to(a[h:h + 1, :], (t, a.shape[1])) for h in range(FOX_HEADS)], axis=0)
    bq = ncol_ref[...]
    s_hist = _dot_nt(qb, hk_ref[...].astype(BF16)) + bq - per_head(hrow_ref[...])
    s_new = _dot_nt(qb, kn_ref[...]) + bq - per_head(nrow_ref[:, :t])
    causal = (lax.broadcasted_iota(jnp.int32, (rows, t), 1)
              <= lax.broadcasted_iota(jnp.int32, (rows, t), 0) % t)
    s_new = jnp.where(causal, s_new, NEG_INF)
    m = jnp.maximum(jnp.max(s_hist, axis=-1, keepdims=True), jnp.max(s_new, axis=-1, keepdims=True))
    e_hist = jnp.exp(s_hist - m)
    e_new = jnp.exp(s_new - m)
    l = jnp.sum(e_hist, axis=-1, keepdims=True) + jnp.sum(e_new, axis=-1, keepdims=True)
    o = (_dot(e_hist.astype(BF16), hv_ref[...].astype(BF16)) + _dot(e_new.astype(BF16), vn_ref[...])) / l
    o = jnp.where(own, o, 0.0)
    fox = o[0:t, :]
    for h in range(1, FOX_HEADS):
        fox = fox + o[h * t:(h + 1) * t, :]
    fox_scr[pl.ds(pl.multiple_of(b * t, t), t), :] = fox

    @pl.when(b == pl.num_programs(0) - 1)
    def _():
        o_ref[...] = _mix_out(fox_scr[...], rest_ref[...], x_ref[...], wout_ref, gfox_ref, gpost_ref)


def _fox_sample(q, kb, vb, hk, hv, layer, hist_row, new_col, new_row, rest, x, lw):
    nb, t, _ = q.shape
    past = hk.shape[2]
    per_b = lambda a: pl.BlockSpec((None,) + a.shape[1:], lambda b: (b,) + (0,) * (a.ndim - 1))
    cache = pl.BlockSpec((None, None, past, FOX_WIDTH), lambda b: (layer, b, 0, 0))
    const = lambda a: pl.BlockSpec(a.shape, lambda b: (0,) * a.ndim)
    return pl.pallas_call(
        _fox_sample_kernel,
        grid=(nb,),
        in_specs=[per_b(q), per_b(kb), per_b(vb), cache, cache, per_b(hist_row), per_b(new_col),
                  per_b(new_row), const(rest), const(x), const(lw["w_out"]), const(lw["g_fox"]),
                  const(lw["g_post_mix"])],
        out_specs=const(x),
        out_shape=jax.ShapeDtypeStruct(x.shape, F32),
        scratch_shapes=[pltpu.VMEM((nb * t, FOX_WIDTH), F32)],
        compiler_params=_params("arbitrary"),
        name="fox_mix_sample",
    )(q, kb, vb, hk, hv, hist_row, new_col, new_row, rest, x, lw["w_out"], lw["g_fox"], lw["g_post_mix"])


def _ffn_body(x_ref, gpre_ref, wup_ref, wdw_ref, bdw_ref, wd_ref, gpost_ref, o_ref, g_scr, shifted, emit):
    h2 = _rms(x_ref[...], gpre_ref[...]).astype(BF16)
    nf = FFN_DIM // FFN_COLS

    def up(c):
        conv_cols = slice(c * FFN_COLS, (c + 1) * FFN_COLS)
        lin_cols = slice(FFN_DIM + c * FFN_COLS, FFN_DIM + (c + 1) * FFN_COLS)
        return _dot(h2, wup_ref[:, conv_cols]), _dot(h2, wup_ref[:, lin_cols])

    nxt = up(0)
    for c in range(nf):
        a, lin = nxt
        if c + 1 < nf:
            nxt = up(c + 1)
        cols = slice(c * FFN_COLS, (c + 1) * FFN_COLS)
        a1, a2 = shifted(a, c)
        conv = bdw_ref[:, cols] + wdw_ref[0:1, cols] * a2
        conv = conv + wdw_ref[1:2, cols] * a1
        conv = conv + wdw_ref[2:3, cols] * a
        g_scr[:, cols] = (jax.nn.silu(conv) * lin).astype(BF16)
        emit(a, c)
    o_ref[...] = x_ref[...] + _rms(_dot(g_scr[...], wd_ref[...]), gpost_ref[...])


def _ffn_prompt_kernel(x_ref, gpre_ref, wup_ref, wdw_ref, bdw_ref, wd_ref, gpost_ref, hist_ref,
                       o_ref, tail_ref, g_scr, carry_scr, work_scr):
    tm = x_ref.shape[0]
    head = SUBLANES

    @pl.when(pl.program_id(1) == 0)
    def _():
        carry_scr[0:head - (CONV_WIDTH - 1), :] = jnp.zeros((head - (CONV_WIDTH - 1), FFN_DIM), F32)
        carry_scr[head - (CONV_WIDTH - 1):head, :] = hist_ref[...]

    def shifted(a, c):
        cols = slice(c * FFN_COLS, (c + 1) * FFN_COLS)
        work = work_scr.at[c % 2]
        work[0:head, :] = carry_scr[:, cols]
        work[head:head + tm, :] = a
        return work[head - 1:head - 1 + tm, :], work[head - 2:head - 2 + tm, :]

    def emit(a, c):
        cols = slice(c * FFN_COLS, (c + 1) * FFN_COLS)
        carry_scr[:, cols] = a[tm - head:, :]
        tail_ref[:, cols] = a[tm - head:, :]

    _ffn_body(x_ref, gpre_ref, wup_ref, wdw_ref, bdw_ref, wd_ref, gpost_ref, o_ref, g_scr, shifted, emit)


def _ffn_sample_kernel(x_ref, gpre_ref, wup_ref, wdw_ref, bdw_ref, wd_ref, gpost_ref, e1_ref, e2_ref,
                       o_ref, a_ref, g_scr, work_scr, *, seg):
    tm = x_ref.shape[0]
    head = SUBLANES
    rmod = lax.broadcasted_iota(jnp.int32, (tm, FFN_COLS), 0) % seg

    def shifted(a, c):
        cols = slice(c * FFN_COLS, (c + 1) * FFN_COLS)
        work = work_scr.at[c % 2]
        work[0:head, :] = jnp.zeros((head, FFN_COLS), F32)
        work[head:head + tm, :] = a
        a1 = jnp.where(rmod >= 1, work[head - 1:head - 1 + tm, :], e1_ref[:, cols])
        a2 = jnp.where(rmod >= 2, work[head - 2:head - 2 + tm, :], e2_ref[:, cols])
        return a1, a2

    def emit(a, c):
        a_ref[:, c * FFN_COLS:(c + 1) * FFN_COLS] = a

    _ffn_body(x_ref, gpre_ref, wup_ref, wdw_ref, bdw_ref, wd_ref, gpost_ref, o_ref, g_scr, shifted, emit)


def _ffn_weight_specs():
    zeros = lambda nd: (lambda *_: (0,) * nd)
    once = lambda shape: pl.BlockSpec(shape, zeros(len(shape)), pipeline_mode=pl.Buffered(1))
    return [
        once((1, D_MODEL)),
        once((D_MODEL, 2 * FFN_DIM)),
        once((CONV_WIDTH, FFN_DIM)),
        once((1, FFN_DIM)),
        once((FFN_DIM, D_MODEL)),
        once((1, D_MODEL)),
    ]


def _ffn_weights(lw):
    return (lw["g_pre_ffn"], lw["w_up"], lw["w_dwconv"], lw["b_dwconv"], lw["w_down"], lw["g_post_ffn"])


def _ffn_prompt(x, hist, lw):
    nb, s, _ = x.shape
    tm = FFN_ROWS
    nt = s // tm
    return pl.pallas_call(
        _ffn_prompt_kernel,
        grid=(nb, nt),
        in_specs=[pl.BlockSpec((None, tm, D_MODEL), lambda b, i: (b, i, 0))] + _ffn_weight_specs() + [
            pl.BlockSpec((None, CONV_WIDTH - 1, FFN_DIM), lambda b, i: (b, 0, 0))],
        out_specs=[pl.BlockSpec((None, tm, D_MODEL), lambda b, i: (b, i, 0)),
                   pl.BlockSpec((None, None, SUBLANES, FFN_DIM), lambda b, i: (b, i, 0, 0))],
        out_shape=[jax.ShapeDtypeStruct(x.shape, F32),
                   jax.ShapeDtypeStruct((nb, nt, SUBLANES, FFN_DIM), F32)],
        scratch_shapes=[pltpu.VMEM((tm, FFN_DIM), BF16),
                        pltpu.VMEM((SUBLANES, FFN_DIM), F32),
                        pltpu.VMEM((2, tm + SUBLANES, FFN_COLS), F32)],
        compiler_params=_params("parallel", "arbitrary"),
        name="conv_ffn_prompt",
    )(x, *_ffn_weights(lw), hist)


def _ffn_sample(x, e1, e2, lw, seg):
    rows, _ = x.shape
    whole = lambda w: pl.BlockSpec((rows, w), lambda i: (0, 0))
    return pl.pallas_call(
        functools.partial(_ffn_sample_kernel, seg=seg),
        grid=(1,),
        in_specs=[whole(D_MODEL)] + _ffn_weight_specs() + [whole(FFN_DIM), whole(FFN_DIM)],
        out_specs=[whole(D_MODEL), whole(FFN_DIM)],
        out_shape=[jax.ShapeDtypeStruct(x.shape, F32), jax.ShapeDtypeStruct((rows, FFN_DIM), F32)],
        scratch_shapes=[pltpu.VMEM((rows, FFN_DIM), BF16),
                        pltpu.VMEM((2, rows + SUBLANES, FFN_COLS), F32)],
        compiler_params=_params("arbitrary"),
        name="conv_ffn_sample",
    )(x, *_ffn_weights(lw), e1, e2)


def _layer_weights(l, g_pre_mix, w_in, b_forget, w_spatial, b_spatial, g_sgu, g_group_out, w_out, g_post_mix,
                   g_pre_ffn, w_up, w_dwconv, b_dwconv, w_down, g_post_ffn, dec_seq):
    row = lambda a: a[l].reshape(1, -1)
    wi = w_in[l]
    c0 = 3 * FOX_WIDTH
    c1 = c0 + FOX_HEADS
    c2 = c1 + 2 * SGU_WIDTH
    w_b = jnp.concatenate([wi[:, c1:c2], wi[:, c2:], wi[:, c0:c1],
                           jnp.zeros((D_MODEL, B_COLS - (2 * SGU_WIDTH + MEM_WIDTH + FOX_HEADS)), F32)], axis=1)
    ws = w_spatial[l]
    bs = b_spatial[l]
    group_dim = SGU_WIDTH // SGU_GROUPS
    ws_prompt = jnp.concatenate([ws[g] for g in range(SGU_GROUPS)], axis=1)
    bt_prompt = jnp.repeat(bs.T, group_dim, axis=1)
    reps = GMLP_CHUNK // dec_seq
    blk = (jnp.arange(GMLP_CHUNK)[:, None] // dec_seq) == (jnp.arange(GMLP_CHUNK)[None, :] // dec_seq)
    ws_sample = jnp.concatenate(
        [jnp.where(blk, jnp.tile(ws[g, :dec_seq, :dec_seq], (reps, reps)), 0.0) for g in range(SGU_GROUPS)], axis=1)
    bt_sample = jnp.tile(jnp.repeat(bs[:, :dec_seq].T, group_dim, axis=1), (reps, 1))
    return {
        "g_pre_mix": row(g_pre_mix), "w_qkv": wi[:, :c0].astype(BF16), "w_b": w_b.astype(BF16),
        "b_forget": row(b_forget), "g_sgu": row(g_sgu),
        "g_fox": row(g_group_out)[:, :FOX_WIDTH], "g_rest": row(g_group_out)[:, FOX_WIDTH:],
        "w_out": w_out[l].astype(BF16), "g_post_mix": row(g_post_mix), "g_pre_ffn": row(g_pre_ffn),
        "w_up": w_up[l].astype(BF16), "w_dwconv": w_dwconv[l], "b_dwconv": row(b_dwconv),
        "w_down": w_down[l].astype(BF16), "g_post_ffn": row(g_post_ffn),
        "ws_prompt": ws_prompt, "bt_prompt": bt_prompt, "ws_sample": ws_sample, "bt_sample": bt_sample,
    }


def kernel(x_prompt, x_sample, mem_prompt, cache_fox_k, cache_fox_v, cache_fox_logf, cache_mem_k, cache_mem_v,
           cache_ffn_conv, g_pre_mix, w_in, b_forget, w_spatial, b_spatial, g_sgu, g_mem, w_mem_kv, g_group_out,
           w_out, g_post_mix, g_pre_ffn, w_up, w_dwconv, b_dwconv, w_down, g_post_ffn):
    depth = w_in.shape[0]
    batch, seq, _ = x_prompt.shape
    dec_batch, dec_seq, _ = x_sample.shape
    past = cache_fox_k.shape[2]
    dec_rows = dec_batch * dec_seq

    mem_k_all, mem_v_all = _memory_kv(mem_prompt, g_mem, w_mem_kv.astype(BF16))
    flat_mem = lambda a: a.reshape(-1, N_MEM, MEM_WIDTH)
    pmk, pmv = flat_mem(mem_k_all), flat_mem(mem_v_all)
    smk, smv = flat_mem(cache_mem_k), flat_mem(cache_mem_v)
    hk = cache_fox_k.reshape(depth, dec_batch, past, FOX_WIDTH)
    hv = cache_fox_v.reshape(depth, dec_batch, past, FOX_WIDTH)

    yp = x_prompt
    ys = x_sample.reshape(1, dec_rows, D_MODEL)
    zeros_hist = jnp.zeros((batch, CONV_WIDTH - 1, FFN_DIM), F32)
    outs = [[] for _ in range(11)]
    stacked_kv = ()
    for l in range(depth):
        lw = _layer_weights(l, g_pre_mix, w_in, b_forget, w_spatial, b_spatial, g_sgu, g_group_out, w_out,
                            g_post_mix, g_pre_ffn, w_up, w_dwconv, b_dwconv, w_down, g_post_ffn, dec_seq)

        qt, k_all, v_all, kb, vt, logf, rest = _in_proj(yp, pmk, pmv, l * batch, 1, lw, IN_PROJ_ROWS,
                                                        lw["ws_prompt"], lw["bt_prompt"], True,
                                                        layer=l, depth=depth, stacked_kv=stacked_kv)
        stacked_kv = (k_all, v_all)
        logf_row = jnp.swapaxes(logf, 1, 2).reshape(batch * FOX_HEADS, seq)
        c_row = _lane_cumsum(logf_row).reshape(batch, FOX_HEADS, seq)
        c_col = jnp.swapaxes(c_row, 1, 2)
        yp = _fox_prompt(qt, kb, vt, c_col, c_row, rest, yp, lw)
        yp, tail = _ffn_prompt(yp, zeros_hist, lw)
        outs[2].append(logf)
        outs[3].append(mem_k_all[l].reshape(batch, N_MEM, MEM_HEADS, HEAD_DIM))
        outs[4].append(mem_v_all[l].reshape(batch, N_MEM, MEM_HEADS, HEAD_DIM))
        outs[5].append(tail[:, -1, SUBLANES - (CONV_WIDTH - 1):, :])

        q, k, v, kb, vb, logf, rest, vrows = _in_proj(ys, smk, smv, l * dec_batch, dec_batch, lw, dec_rows,
                                                      lw["ws_sample"], lw["bt_sample"], False)
        per_b = lambda a: a.reshape(dec_batch, dec_seq, a.shape[-1])
        hist_row = _lane_cumsum(jnp.swapaxes(cache_fox_logf[l], 1, 2).reshape(dec_batch * FOX_HEADS, past),
                                minus_total=True).reshape(dec_batch, FOX_HEADS, past)
        logf_new_row = jnp.swapaxes(per_b(logf), 1, 2).reshape(dec_batch * FOX_HEADS, dec_seq)
        new_row = _lane_cumsum(jnp.pad(logf_new_row, ((0, 0), (0, LANES - dec_seq)))
                               ).reshape(dec_batch, FOX_HEADS, LANES)
        new_col = new_row[:, :, :dec_seq].reshape(dec_batch, FOX_HEADS * dec_seq, 1)
        ys2 = _fox_sample(per_b(q[0]), per_b(kb[0]), per_b(vb[0]), hk, hv, l, hist_row, new_col, new_row,
                          rest[0], ys[0], lw)
        conv_hist = cache_ffn_conv[l]
        pad_rows = lambda a: jnp.pad(a, ((0, 0), (0, dec_seq - a.shape[1]), (0, 0))).reshape(dec_rows, FFN_DIM)
        e1 = pad_rows(conv_hist[:, 1:2, :])
        e2 = pad_rows(conv_hist)
        ys_flat, a_all = _ffn_sample(ys2, e1, e2, lw, dec_seq)
        ys = ys_flat.reshape(1, dec_rows, D_MODEL)
        outs[6].append(per_b(k[0]).reshape(dec_batch, dec_seq, FOX_HEADS, HEAD_DIM))
        outs[7].append(per_b(v[0]).reshape(dec_batch, dec_seq, FOX_HEADS, HEAD_DIM))
        outs[8].append(per_b(logf[0]))
        outs[9].append(per_b(vrows[0]))
        outs[10].append(a_all.reshape(dec_batch, dec_seq, FFN_DIM)[:, dec_seq - (CONV_WIDTH - 1):, :])

    heads = lambda a: a.reshape(depth, batch, seq, FOX_HEADS, HEAD_DIM)
    stacked = [heads(stacked_kv[0]), heads(stacked_kv[1])] + [jnp.stack(o) for o in outs[2:]]
    return (yp, ys.reshape(dec_batch, dec_seq, D_MODEL), *stacked)
```

```python
import functools

import jax
import jax.numpy as jnp
from jax import lax
from jax.experimental import pallas as pl
from jax.experimental.pallas import tpu as pltpu

D_MODEL = 1024
HEAD_DIM = 64
FOX_WIDTH = 512
FOX_HEADS = 8
SGU_WIDTH = 256
SGU_GROUPS = 4
GMLP_CHUNK = 128
CHUNK = 64
MEM_WIDTH = 256
MEM_HEADS = 4
N_MEM = 256
FFN_DIM = 2816
CONV_WIDTH = 3
RMS_EPS = 1e-6
NEG_INF = -1e30
QK_SCALE = HEAD_DIM ** -0.5

LANES = 128
SUBLANES = 8
HEAD_PAIR = LANES // HEAD_DIM
VMEM_LIMIT_BYTES = 56 * 1024 * 1024

IN_PROJ_ROWS = 512
ATTN_Q_ROWS = 256
ATTN_K_ROWS = 256
FFN_ROWS = 512
FFN_COLS = 256
B_COLS = 896

BF16 = jnp.bfloat16
F32 = jnp.float32


def _rms(x, g):
    y = x * lax.rsqrt(jnp.mean(x * x, axis=-1, keepdims=True) + RMS_EPS)
    return y * g


def _dot(a, b):
    return jnp.dot(a, b, preferred_element_type=F32)


def _dot_nt(a, b):
    return lax.dot_general(a, b, (((1,), (1,)), ((), ())), preferred_element_type=F32)


def _params(*semantics):
    return pltpu.CompilerParams(dimension_semantics=semantics, vmem_limit_bytes=VMEM_LIMIT_BYTES)


def _layer_spec(a, layer, **kwargs):
    return pl.BlockSpec((None,) + a.shape[1:], lambda *_: (layer,) + (0,) * (a.ndim - 1), **kwargs)


def _memkv_kernel(mem_ref, g_ref, w_ref, mk_ref, mv_ref):
    h = _rms(mem_ref[...], g_ref[...]).astype(BF16)
    kv = _dot(h, w_ref[...])
    mk_ref[...] = kv[:, :MEM_WIDTH]
    mv_ref[...] = kv[:, MEM_WIDTH:]


def _memory_kv(mem, g_mem, w_mem_kv_bf):
    depth = g_mem.shape[0]
    batch = mem.shape[0]
    out = jax.ShapeDtypeStruct((depth, batch, N_MEM, MEM_WIDTH), F32)
    return pl.pallas_call(
        _memkv_kernel,
        grid=(depth, batch),
        in_specs=[
            pl.BlockSpec((None, N_MEM, D_MODEL), lambda l, b: (b, 0, 0)),
            pl.BlockSpec((None, 1, D_MODEL), lambda l, b: (l, 0, 0)),
            pl.BlockSpec((None, D_MODEL, 2 * MEM_WIDTH), lambda l, b: (l, 0, 0)),
        ],
        out_specs=[
            pl.BlockSpec((None, None, N_MEM, MEM_WIDTH), lambda l, b: (l, b, 0, 0)),
            pl.BlockSpec((None, None, N_MEM, MEM_WIDTH), lambda l, b: (l, b, 0, 0)),
        ],
        out_shape=[out, out],
        compiler_params=_params("arbitrary", "arbitrary"),
        name="memory_kv",
    )(mem, g_mem.reshape(depth, 1, D_MODEL), w_mem_kv_bf)


def _inproj_kernel(x_ref, gpre_ref, wqkv_ref, wb_ref, bfg_ref, ws_ref, bt_ref, gsgu_ref, gg_ref,
                   mk_ref, mv_ref, *out_refs, transposed, n_stack_in):
    out_refs = out_refs[n_stack_in:]
    tm = x_ref.shape[0]
    n_mem_blocks = mk_ref.shape[0]
    if transposed:
        qt_ref, k_ref, v_ref, kb_ref, vt_ref, lf_ref, rest_ref = out_refs
        vrows_ref = None
    else:
        q_ref, k_ref, v_ref, kb_ref, vb_ref, lf_ref, rest_ref, vrows_ref = out_refs
    h = _rms(x_ref[...], gpre_ref[...]).astype(BF16)

    yb = _dot(h, wb_ref[...])
    fg = yb[:, 2 * SGU_WIDTH + MEM_WIDTH:2 * SGU_WIDTH + MEM_WIDTH + FOX_HEADS]
    lf_ref[...] = jax.nn.log_sigmoid(fg + bfg_ref[...])

    qmb = (yb[:, 2 * SGU_WIDTH:2 * SGU_WIDTH + MEM_WIDTH] * QK_SCALE).astype(BF16)
    rows = tm // n_mem_blocks
    hlane = lax.broadcasted_iota(jnp.int32, (rows, MEM_WIDTH), 1) // HEAD_DIM
    zero_q = jnp.zeros((rows, MEM_WIDTH), BF16)
    mem_scores = []
    for bi in range(n_mem_blocks):
        qb = qmb[bi * rows:(bi + 1) * rows, :]
        mkb = mk_ref[bi].astype(BF16)
        mem_scores.append([_dot_nt(jnp.where(hlane == hd, qb, zero_q), mkb) for hd in range(MEM_HEADS)])

    q = _dot(h, wqkv_ref[:, :FOX_WIDTH]) * QK_SCALE

    z = jax.nn.gelu(yb[:, :2 * SGU_WIDTH])
    u = z[:, :SGU_WIDTH]
    vv = _rms(z[:, SGU_WIDTH:], gsgu_ref[...])
    if vrows_ref is not None:
        vrows_ref[...] = vv
    vvb = vv.astype(BF16)
    mem_exp = [[jnp.exp(s - jnp.max(s, axis=-1, keepdims=True)) for s in per_block] for per_block in mem_scores]

    wrow = lax.broadcasted_iota(jnp.int32, (GMLP_CHUNK, SGU_GROUPS * GMLP_CHUNK), 0)
    wcol = lax.broadcasted_iota(jnp.int32, (GMLP_CHUNK, SGU_GROUPS * GMLP_CHUNK), 1)
    wmask = ((wcol % GMLP_CHUNK) // CHUNK) <= (wrow // CHUNK)
    wcat = jnp.where(wmask, ws_ref[...], 0.0).astype(BF16)
    glane = lax.broadcasted_iota(jnp.int32, (GMLP_CHUNK, SGU_WIDTH), 1) // (SGU_WIDTH // SGU_GROUPS)
    zero_chunk = jnp.zeros((GMLP_CHUNK, SGU_WIDTH), BF16)
    mixed_parts = []
    for c in range(tm // GMLP_CHUNK):
        vc = vvb[c * GMLP_CHUNK:(c + 1) * GMLP_CHUNK, :]
        rhs = jnp.concatenate([jnp.where(glane == g, vc, zero_chunk) for g in range(SGU_GROUPS)], axis=0)
        mixed_parts.append(_dot(wcat, rhs))
    mem_pv = []
    for bi in range(n_mem_blocks):
        mvb = mv_ref[bi].astype(BF16)
        mem_pv.append([_dot(e.astype(BF16), mvb) for e in mem_exp[bi]])

    k = _dot(h, wqkv_ref[:, FOX_WIDTH:2 * FOX_WIDTH])

    if transposed:
        qt_ref[...] = q.T.astype(BF16)
    else:
        q_ref[...] = q.astype(BF16)
    sgu = jnp.concatenate([u[c * GMLP_CHUNK:(c + 1) * GMLP_CHUNK, :] * (mixed + bt_ref[...])
                           for c, mixed in enumerate(mixed_parts)], axis=0)
    rest_ref[:, :SGU_WIDTH] = _rms(sgu, gg_ref[:, :SGU_WIDTH]).astype(BF16)
    mem_parts = []
    for bi in range(n_mem_blocks):
        out = jnp.zeros((rows, MEM_WIDTH), F32)
        for hd in range(MEM_HEADS):
            o = mem_pv[bi][hd] / jnp.sum(mem_exp[bi][hd], axis=-1, keepdims=True)
            out = jnp.where(hlane == hd, o, out)
        mem_parts.append(out)
    mem = mem_parts[0] if n_mem_blocks == 1 else jnp.concatenate(mem_parts, axis=0)
    rest_ref[:, SGU_WIDTH:] = _rms(mem, gg_ref[:, SGU_WIDTH:]).astype(BF16)

    v = _dot(h, wqkv_ref[:, 2 * FOX_WIDTH:])

    k_ref[...] = k
    kb_ref[...] = k.astype(BF16)
    v_ref[...] = v
    if transposed:
        vt = v.T.astype(BF16)
        tk = vt_ref.shape[-1]
        for c in range(tm // tk):
            vt_ref[c] = vt[:, c * tk:(c + 1) * tk]
    else:
        vb_ref[...] = v.astype(BF16)


def _in_proj(x, mem_k, mem_v, mem_first, mem_per_tile, lw, tm, spatial_w, spatial_b, transposed,
             layer, depth, stacked_kv=()):
    nb, t, _ = x.shape
    nt = t // tm
    if transposed:
        assert mem_per_tile == 1
        mem_index = lambda b, i: (mem_first + b, 0, 0)
    else:
        mem_index = lambda b, i: (mem_first // mem_per_tile + b * nt + i, 0, 0)
    row = lambda w: pl.BlockSpec((None, tm, w), lambda b, i: (b, i, 0))
    memspec = pl.BlockSpec((mem_per_tile, N_MEM, MEM_WIDTH), mem_index)
    f32 = lambda w: jax.ShapeDtypeStruct((nb, t, w), F32)
    bf = lambda w: jax.ShapeDtypeStruct((nb, t, w), BF16)
    consts = (lw["g_pre_mix"], lw["w_qkv"], lw["w_b"], lw["b_forget"], spatial_w, spatial_b,
              lw["g_sgu"], lw["g_rest"])
    if transposed:
        tk = ATTN_K_ROWS
        stack_spec = pl.BlockSpec((None, None, tm, FOX_WIDTH), lambda b, i: (layer, b, i, 0))
        stack_shape = jax.ShapeDtypeStruct((depth, nb, t, FOX_WIDTH), F32)
        out_specs = [pl.BlockSpec((None, FOX_WIDTH, tm), lambda b, i: (b, 0, i)),
                     stack_spec, stack_spec, row(FOX_WIDTH),
                     pl.BlockSpec((None, tm // tk, FOX_WIDTH, tk), lambda b, i: (b, i, 0, 0)),
                     row(FOX_HEADS), row(SGU_WIDTH + MEM_WIDTH)]
        out_shape = [jax.ShapeDtypeStruct((nb, FOX_WIDTH, t), BF16), stack_shape, stack_shape,
                     bf(FOX_WIDTH), jax.ShapeDtypeStruct((nb, t // tk, FOX_WIDTH, tk), BF16),
                     f32(FOX_HEADS), bf(SGU_WIDTH + MEM_WIDTH)]
    else:
        out_specs = [row(FOX_WIDTH)] * 5 + [row(FOX_HEADS), row(SGU_WIDTH + MEM_WIDTH), row(SGU_WIDTH)]
        out_shape = [bf(FOX_WIDTH), f32(FOX_WIDTH), f32(FOX_WIDTH), bf(FOX_WIDTH), bf(FOX_WIDTH),
                     f32(FOX_HEADS), bf(SGU_WIDTH + MEM_WIDTH), f32(SGU_WIDTH)]
    in_specs = [row(D_MODEL)] + [_layer_spec(a, layer) for a in consts] + [memspec, memspec]
    aliases = {len(in_specs) + n: 1 + n for n in range(len(stacked_kv))}
    in_specs += [pl.BlockSpec(memory_space=pl.ANY)] * len(stacked_kv)
    return pl.pallas_call(
        functools.partial(_inproj_kernel, transposed=transposed, n_stack_in=len(stacked_kv)),
        grid=(nb, nt),
        in_specs=in_specs,
        out_specs=out_specs,
        out_shape=out_shape,
        input_output_aliases=aliases,
        compiler_params=_params("parallel", "arbitrary"),
        name="in_proj",
    )(x, *consts, mem_k, mem_v, *stacked_kv)


def _cumsum_kernel(x_ref, o_ref, *, minus_total):
    x = x_ref[...]
    n = x.shape[-1]
    lane = lax.broadcasted_iota(jnp.int32, x.shape, 1)
    shift = 1
    while shift < n:
        x = x + jnp.where(lane >= shift, pltpu.roll(x, shift, 1), 0.0)
        shift *= 2
    if minus_total:
        x = x - x[:, n - 1:n]
    o_ref[...] = x


def _lane_cumsum(x, minus_total=False):
    return pl.pallas_call(
        functools.partial(_cumsum_kernel, minus_total=minus_total),
        out_shape=jax.ShapeDtypeStruct(x.shape, F32),
        name="lane_cumsum",
    )(x)


def _mix_out(fox, rest, x, wout_ref, gfox_ref, gpost_ref, gffn_ref, o_ref, h2_ref):
    fox_n = _rms(fox, gfox_ref[...]).astype(BF16)
    y = _dot(fox_n, wout_ref[:FOX_WIDTH, :]) + _dot(rest, wout_ref[FOX_WIDTH:, :])
    x_mid = x + _rms(y, gpost_ref[...])
    o_ref[...] = x_mid
    h2_ref[...] = _rms(x_mid, gffn_ref[...]).astype(BF16)


def _split3(c):
    hi = c.astype(BF16).astype(F32)
    r = c - hi
    mid = r.astype(BF16).astype(F32)
    lo = (r - mid).astype(BF16).astype(F32)
    return hi, mid, lo


_BIAS_ONES = 6


def _fox_prompt_kernel(qt_ref, kb_ref, vt_ref, ccol_ref, crow_ref, rest_ref, x_ref, wout_ref, gfox_ref,
                       gpost_ref, gffn_ref, o_ref, h2_ref, kaug_scr, qa_scr, sa_scr, sb_scr, m_scr, l_scr, acc_scr):
    tq = qt_ref.shape[1]
    tk = vt_ref.shape[-1]
    s_len = kb_ref.shape[0]
    slab = 2 * LANES
    i = pl.program_id(1)

    @pl.when(i == 0)
    def _():
        n_slab = FOX_HEADS // HEAD_PAIR
        src = lax.broadcasted_iota(jnp.int32, (LANES, n_slab * LANES), 0)
        dst = lax.broadcasted_iota(jnp.int32, (LANES, n_slab * LANES), 1)
        head, piece = src % FOX_HEADS, src // FOX_HEADS
        piece_dst = (head // HEAD_PAIR) * LANES + 3 * (head % HEAD_PAIR) + piece
        is_piece = (src < 3 * FOX_HEADS) & (dst == piece_dst)
        is_one = (src == 3 * FOX_HEADS) & (dst % LANES >= _BIAS_ONES) & (dst % LANES < _BIAS_ONES + 3)
        place = jnp.where(is_piece | is_one, 1.0, 0.0).astype(BF16)
        pad = jnp.concatenate([jnp.ones((tk, 1), F32), jnp.zeros((tk, LANES - 3 * FOX_HEADS - 1), F32)], axis=1)
        for r in range(s_len // tk):
            rows = slice(r * tk, (r + 1) * tk)
            pieces = jnp.concatenate(_split3(ccol_ref[rows, :]) + (pad,), axis=1)
            bias = _dot(pieces.astype(BF16), place)
            for p in range(n_slab):
                kaug_scr[rows, p * slab:p * slab + LANES] = kb_ref[rows, p * LANES:(p + 1) * LANES]
                kaug_scr[rows, p * slab + LANES:(p + 1) * slab] = bias[:, p * LANES:(p + 1) * LANES].astype(BF16)

    rowi = lax.broadcasted_iota(jnp.int32, (LANES, tq), 0)
    diag = lax.broadcasted_iota(jnp.int32, (tk, tq), 0) <= lax.broadcasted_iota(jnp.int32, (tk, tq), 1)
    crow = crow_ref[...]
    zero_q = jnp.zeros((LANES, tq), BF16)
    for p in range(FOX_HEADS // HEAD_PAIR):
        qtp = qt_ref[p * LANES:(p + 1) * LANES, :]
        for hh in range(HEAD_PAIR):
            own = (rowi >= hh * HEAD_DIM) & (rowi < (hh + 1) * HEAD_DIM)
            chi, cmid, clo = _split3(crow[2 * p + hh:2 * p + hh + 1, :])
            br = jnp.where(rowi == _BIAS_ONES, chi,
                           jnp.where(rowi == _BIAS_ONES + 1, cmid, jnp.where(rowi == _BIAS_ONES + 2, clo, 0.0)))
            br = jnp.where((rowi >= 3 * hh) & (rowi < 3 * hh + 3), -1.0, br)
            qa_scr[2 * p + hh, :LANES, :] = jnp.where(own, qtp, zero_q)
            qa_scr[2 * p + hh, LANES:, :] = br.astype(BF16)

    m_scr[...] = jnp.full(m_scr.shape, NEG_INF, F32)
    l_scr[...] = jnp.zeros(l_scr.shape, F32)
    acc_scr[...] = jnp.zeros(acc_scr.shape, F32)

    def scores_to(buf_ref, j):
        off = pl.multiple_of(j * tk, tk)
        for h in range(FOX_HEADS):
            ka = kaug_scr[pl.ds(off, tk), (h // HEAD_PAIR) * slab:(h // HEAD_PAIR + 1) * slab]
            buf_ref[h] = _dot(ka, qa_scr[h])

    def absorb_from(buf_ref, j, mask):
        vt = vt_ref[j]
        for h in range(FOX_HEADS):
            feat = slice(h * HEAD_DIM, (h + 1) * HEAD_DIM)
            s = buf_ref[h]
            if mask is not None:
                s = jnp.where(mask, s, NEG_INF)
            m = m_scr[h:h + 1, :]
            m_new = jnp.maximum(m, jnp.max(s, axis=0, keepdims=True))
            alpha = jnp.exp(m - m_new)
            e = jnp.exp(s - m_new)
            m_scr[h:h + 1, :] = m_new
            l_scr[h:h + 1, :] = alpha * l_scr[h:h + 1, :] + jnp.sum(e, axis=0, keepdims=True)
            acc_scr[feat, :] = alpha * acc_scr[feat, :] + _dot(vt[feat, :], e.astype(BF16))

    odd = i % 2

    @pl.when(odd == 1)
    def _():
        scores_to(sb_scr, 0)
        scores_to(sa_scr, 1)
        absorb_from(sb_scr, 0, None)

    @pl.when(odd == 0)
    def _():
        scores_to(sa_scr, 0)

    def step(jj, _):
        t = odd + 2 * jj
        scores_to(sb_scr, t + 1)
        absorb_from(sa_scr, t, None)
        scores_to(sa_scr, t + 2)
        absorb_from(sb_scr, t + 1, None)
        return 0

    lax.fori_loop(0, i // 2, step, 0)
    absorb_from(sa_scr, i, diag)
    fox_t = jnp.concatenate([acc_scr[h * HEAD_DIM:(h + 1) * HEAD_DIM, :] / l_scr[h:h + 1, :]
                             for h in range(FOX_HEADS)], axis=0)
    fox = fox_t.T
    _mix_out(fox, rest_ref[...], x_ref[...], wout_ref, gfox_ref, gpost_ref, gffn_ref, o_ref, h2_ref)


def _mix_weights(lw):
    return (lw["w_out"], lw["g_fox"], lw["g_post_mix"], lw["g_pre_ffn"])


def _fox_prompt(qt, kb, vt, c_col, c_row, rest, x, lw, layer):
    nb, s, _ = x.shape
    tq, tk = ATTN_Q_ROWS, ATTN_K_ROWS
    assert tq == tk and vt.shape[-1] == tk
    nq = s // tq
    qrow = lambda w: pl.BlockSpec((None, tq, w), lambda b, i: (b, i, 0))
    qcol = lambda r: pl.BlockSpec((None, r, tq), lambda b, i: (b, 0, i))
    full = lambda a: pl.BlockSpec((None,) + a.shape[1:], lambda b, i: (b,) + (0,) * (a.ndim - 1))
    return pl.pallas_call(
        _fox_prompt_kernel,
        grid=(nb, nq),
        in_specs=[qcol(FOX_WIDTH), full(kb), full(vt), full(c_col), qcol(FOX_HEADS),
                  qrow(SGU_WIDTH + MEM_WIDTH), qrow(D_MODEL)] + [_layer_spec(a, layer) for a in _mix_weights(lw)],
        out_specs=[qrow(D_MODEL), qrow(D_MODEL)],
        out_shape=[jax.ShapeDtypeStruct(x.shape, F32), jax.ShapeDtypeStruct(x.shape, BF16)],
        scratch_shapes=[pltpu.VMEM((s, 2 * FOX_WIDTH), BF16),
                        pltpu.VMEM((FOX_HEADS, 2 * LANES, tq), BF16),
                        pltpu.VMEM((FOX_HEADS, tk, tq), F32), pltpu.VMEM((FOX_HEADS, tk, tq), F32),
                        pltpu.VMEM((FOX_HEADS, tq), F32), pltpu.VMEM((FOX_HEADS, tq), F32),
                        pltpu.VMEM((FOX_WIDTH, tq), F32)],
        compiler_params=_params("parallel", "arbitrary"),
        name="fox_mix_prompt",
    )(qt, kb, vt, c_col, c_row, rest, x, *_mix_weights(lw))


def _fox_sample_kernel(q_ref, kn_ref, vn_ref, hk_ref, hv_ref, hrow_ref, ncol_ref, nrow_ref, rest_ref, x_ref,
                       wout_ref, gfox_ref, gpost_ref, gffn_ref, o_ref, h2_ref, fox_scr):
    t = q_ref.shape[0]
    b = pl.program_id(0)
    rows = FOX_HEADS * t
    qt = jnp.concatenate([q_ref[...]] * FOX_HEADS, axis=0)
    row_head = lax.broadcasted_iota(jnp.int32, (rows, FOX_WIDTH), 0) // t
    lane_head = lax.broadcasted_iota(jnp.int32, (rows, FOX_WIDTH), 1) // HEAD_DIM
    own = row_head == lane_head
    qb = jnp.where(own, qt, jnp.zeros_like(qt))
    per_head = lambda a: jnp.concatenate(
        [jnp.broadcast_to(a[h:h + 1, :], (t, a.shape[1])) for h in range(FOX_HEADS)], axis=0)
    bq = ncol_ref[...]
    s_hist = _dot_nt(qb, hk_ref[...].astype(BF16)) + bq - per_head(hrow_ref[...])
    s_new = _dot_nt(qb, kn_ref[...]) + bq - per_head(nrow_ref[:, :t])
    causal = (lax.broadcasted_iota(jnp.int32, (rows, t), 1)
              <= lax.broadcasted_iota(jnp.int32, (rows, t), 0) % t)
    s_new = jnp.where(causal, s_new, NEG_INF)
    m = jnp.maximum(jnp.max(s_hist, axis=-1, keepdims=True), jnp.max(s_new, axis=-1, keepdims=True))
    e_hist = jnp.exp(s_hist - m)
    e_new = jnp.exp(s_new - m)
    l = jnp.sum(e_hist, axis=-1, keepdims=True) + jnp.sum(e_new, axis=-1, keepdims=True)
    o = (_dot(e_hist.astype(BF16), hv_ref[...].astype(BF16)) + _dot(e_new.astype(BF16), vn_ref[...])) / l
    o = jnp.where(own, o, 0.0)
    fox = o[0:t, :]
    for h in range(1, FOX_HEADS):
        fox = fox + o[h * t:(h + 1) * t, :]
    fox_scr[pl.ds(pl.multiple_of(b * t, t), t), :] = fox

    @pl.when(b == pl.num_programs(0) - 1)
    def _():
        _mix_out(fox_scr[...], rest_ref[...], x_ref[...], wout_ref, gfox_ref, gpost_ref, gffn_ref, o_ref, h2_ref)


def _fox_sample(q, kb, vb, hk, hv, layer, hist_row, new_col, new_row, rest, x, lw):
    nb, t, _ = q.shape
    past = hk.shape[2]
    per_b = lambda a: pl.BlockSpec((None,) + a.shape[1:], lambda b: (b,) + (0,) * (a.ndim - 1))
    cache = pl.BlockSpec((None, None, past, FOX_WIDTH), lambda b: (layer, b, 0, 0))
    const = lambda a: pl.BlockSpec(a.shape, lambda b: (0,) * a.ndim)
    return pl.pallas_call(
        _fox_sample_kernel,
        grid=(nb,),
        in_specs=[per_b(q), per_b(kb), per_b(vb), cache, cache, per_b(hist_row), per_b(new_col),
                  per_b(new_row), const(rest), const(x)] + [_layer_spec(a, layer) for a in _mix_weights(lw)],
        out_specs=[const(x), const(x)],
        out_shape=[jax.ShapeDtypeStruct(x.shape, F32), jax.ShapeDtypeStruct(x.shape, BF16)],
        scratch_shapes=[pltpu.VMEM((nb * t, FOX_WIDTH), F32)],
        compiler_params=_params("arbitrary"),
        name="fox_mix_sample",
    )(q, kb, vb, hk, hv, hist_row, new_col, new_row, rest, x, *_mix_weights(lw))


def _ffn_body(x_ref, h2_ref, wup_ref, wdw_ref, bdw_ref, wd_ref, gpost_ref, o_ref, g_scr, shifted, emit):
    h2 = h2_ref[...]
    nf = FFN_DIM // FFN_COLS

    def up(c):
        conv_cols = slice(c * FFN_COLS, (c + 1) * FFN_COLS)
        lin_cols = slice(FFN_DIM + c * FFN_COLS, FFN_DIM + (c + 1) * FFN_COLS)
        return _dot(h2, wup_ref[:, conv_cols]), _dot(h2, wup_ref[:, lin_cols])

    nxt = up(0)
    for c in range(nf):
        a, lin = nxt
        if c + 1 < nf:
            nxt = up(c + 1)
        cols = slice(c * FFN_COLS, (c + 1) * FFN_COLS)
        a1, a2 = shifted(a, c)
        conv = bdw_ref[:, cols] + wdw_ref[0:1, cols] * a2
        conv = conv + wdw_ref[1:2, cols] * a1
        conv = conv + wdw_ref[2:3, cols] * a
        g_scr[:, cols] = (jax.nn.silu(conv) * lin).astype(BF16)
        emit(a, c)
    o_ref[...] = x_ref[...] + _rms(_dot(g_scr[...], wd_ref[...]), gpost_ref[...])


def _ffn_prompt_kernel(x_ref, h2_ref, wup_ref, wdw_ref, bdw_ref, wd_ref, gpost_ref, hist_ref,
                       o_ref, tail_ref, g_scr, carry_scr, work_scr):
    tm = x_ref.shape[0]
    head = SUBLANES

    @pl.when(pl.program_id(1) == 0)
    def _():
        carry_scr[0:head - (CONV_WIDTH - 1), :] = jnp.zeros((head - (CONV_WIDTH - 1), FFN_DIM), F32)
        carry_scr[head - (CONV_WIDTH - 1):head, :] = hist_ref[...]

    def shifted(a, c):
        cols = slice(c * FFN_COLS, (c + 1) * FFN_COLS)
        work = work_scr.at[c % 2]
        work[0:head, :] = carry_scr[:, cols]
        work[head:head + tm, :] = a
        return work[head - 1:head - 1 + tm, :], work[head - 2:head - 2 + tm, :]

    def emit(a, c):
        cols = slice(c * FFN_COLS, (c + 1) * FFN_COLS)
        carry_scr[:, cols] = a[tm - head:, :]
        tail_ref[:, cols] = a[tm - head:, :]

    _ffn_body(x_ref, h2_ref, wup_ref, wdw_ref, bdw_ref, wd_ref, gpost_ref, o_ref, g_scr, shifted, emit)


def _ffn_sample_kernel(x_ref, h2_ref, wup_ref, wdw_ref, bdw_ref, wd_ref, gpost_ref, e1_ref, e2_ref,
                       o_ref, a_ref, g_scr, work_scr, *, seg):
    tm = x_ref.shape[0]
    head = SUBLANES
    rmod = lax.broadcasted_iota(jnp.int32, (tm, FFN_COLS), 0) % seg

    def shifted(a, c):
        cols = slice(c * FFN_COLS, (c + 1) * FFN_COLS)
        work = work_scr.at[c % 2]
        work[0:head, :] = jnp.zeros((head, FFN_COLS), F32)
        work[head:head + tm, :] = a
        a1 = jnp.where(rmod >= 1, work[head - 1:head - 1 + tm, :], e1_ref[:, cols])
        a2 = jnp.where(rmod >= 2, work[head - 2:head - 2 + tm, :], e2_ref[:, cols])
        return a1, a2

    def emit(a, c):
        a_ref[:, c * FFN_COLS:(c + 1) * FFN_COLS] = a

    _ffn_body(x_ref, h2_ref, wup_ref, wdw_ref, bdw_ref, wd_ref, gpost_ref, o_ref, g_scr, shifted, emit)


def _ffn_weights(lw):
    return (lw["w_up"], lw["w_dwconv"], lw["b_dwconv"], lw["w_down"], lw["g_post_ffn"])


def _ffn_weight_specs(lw, layer):
    return [_layer_spec(a, layer, pipeline_mode=pl.Buffered(1)) for a in _ffn_weights(lw)]


def _ffn_prompt(x, h2, hist, lw, layer):
    nb, s, _ = x.shape
    tm = FFN_ROWS
    nt = s // tm
    row = pl.BlockSpec((None, tm, D_MODEL), lambda b, i: (b, i, 0))
    return pl.pallas_call(
        _ffn_prompt_kernel,
        grid=(nb, nt),
        in_specs=[row, row] + _ffn_weight_specs(lw, layer) + [
            pl.BlockSpec((None, CONV_WIDTH - 1, FFN_DIM), lambda b, i: (b, 0, 0))],
        out_specs=[pl.BlockSpec((None, tm, D_MODEL), lambda b, i: (b, i, 0)),
                   pl.BlockSpec((None, None, SUBLANES, FFN_DIM), lambda b, i: (b, i, 0, 0))],
        out_shape=[jax.ShapeDtypeStruct(x.shape, F32),
                   jax.ShapeDtypeStruct((nb, nt, SUBLANES, FFN_DIM), F32)],
        scratch_shapes=[pltpu.VMEM((tm, FFN_DIM), BF16),
                        pltpu.VMEM((SUBLANES, FFN_DIM), F32),
                        pltpu.VMEM((2, tm + SUBLANES, FFN_COLS), F32)],
        compiler_params=_params("parallel", "arbitrary"),
        name="conv_ffn_prompt",
    )(x, h2, *_ffn_weights(lw), hist)


def _ffn_sample(x, h2, e1, e2, lw, layer, seg):
    rows, _ = x.shape
    whole = lambda w: pl.BlockSpec((rows, w), lambda i: (0, 0))
    return pl.pallas_call(
        functools.partial(_ffn_sample_kernel, seg=seg),
        grid=(1,),
        in_specs=[whole(D_MODEL), whole(D_MODEL)] + _ffn_weight_specs(lw, layer) + [whole(FFN_DIM), whole(FFN_DIM)],
        out_specs=[whole(D_MODEL), whole(FFN_DIM)],
        out_shape=[jax.ShapeDtypeStruct(x.shape, F32), jax.ShapeDtypeStruct((rows, FFN_DIM), F32)],
        scratch_shapes=[pltpu.VMEM((rows, FFN_DIM), BF16),
                        pltpu.VMEM((2, rows + SUBLANES, FFN_COLS), F32)],
        compiler_params=_params("arbitrary"),
        name="conv_ffn_sample",
    )(x, h2, *_ffn_weights(lw), e1, e2)


def _prepare_weights(g_pre_mix, w_in, b_forget, w_spatial, b_spatial, g_sgu, g_group_out, w_out, g_post_mix,
                     g_pre_ffn, w_up, w_dwconv, b_dwconv, w_down, g_post_ffn, dec_seq):
    depth = w_in.shape[0]
    row = lambda a: a.reshape(depth, 1, -1)
    c0 = 3 * FOX_WIDTH
    c1 = c0 + FOX_HEADS
    c2 = c1 + 2 * SGU_WIDTH
    w_b = jnp.concatenate([w_in[:, :, c1:c2], w_in[:, :, c2:], w_in[:, :, c0:c1],
                           jnp.zeros((depth, D_MODEL, B_COLS - (2 * SGU_WIDTH + MEM_WIDTH + FOX_HEADS)), F32)],
                          axis=2)
    group_dim = SGU_WIDTH // SGU_GROUPS
    ws_prompt = jnp.concatenate([w_spatial[:, g] for g in range(SGU_GROUPS)], axis=2)
    bt_prompt = jnp.repeat(jnp.swapaxes(b_spatial, 1, 2), group_dim, axis=2)
    reps = GMLP_CHUNK // dec_seq
    blk = (jnp.arange(GMLP_CHUNK)[:, None] // dec_seq) == (jnp.arange(GMLP_CHUNK)[None, :] // dec_seq)
    ws_sample = jnp.concatenate(
        [jnp.where(blk, jnp.tile(w_spatial[:, g, :dec_seq, :dec_seq], (1, reps, reps)), 0.0)
         for g in range(SGU_GROUPS)], axis=2)
    bt_sample = jnp.tile(jnp.repeat(jnp.swapaxes(b_spatial[:, :, :dec_seq], 1, 2), group_dim, axis=2),
                         (1, reps, 1))
    return {
        "g_pre_mix": row(g_pre_mix), "w_qkv": w_in[:, :, :c0].astype(BF16), "w_b": w_b.astype(BF16),
        "b_forget": row(b_forget), "g_sgu": row(g_sgu),
        "g_fox": row(g_group_out[:, :FOX_WIDTH]), "g_rest": row(g_group_out[:, FOX_WIDTH:]),
        "w_out": w_out.astype(BF16), "g_post_mix": row(g_post_mix), "g_pre_ffn": row(g_pre_ffn),
        "w_up": w_up.astype(BF16), "w_dwconv": w_dwconv, "b_dwconv": row(b_dwconv),
        "w_down": w_down.astype(BF16), "g_post_ffn": row(g_post_ffn),
        "ws_prompt": ws_prompt, "bt_prompt": bt_prompt, "ws_sample": ws_sample, "bt_sample": bt_sample,
    }


def kernel(x_prompt, x_sample, mem_prompt, cache_fox_k, cache_fox_v, cache_fox_logf, cache_mem_k, cache_mem_v,
           cache_ffn_conv, g_pre_mix, w_in, b_forget, w_spatial, b_spatial, g_sgu, g_mem, w_mem_kv, g_group_out,
           w_out, g_post_mix, g_pre_ffn, w_up, w_dwconv, b_dwconv, w_down, g_post_ffn):
    depth = w_in.shape[0]
    batch, seq, _ = x_prompt.shape
    dec_batch, dec_seq, _ = x_sample.shape
    past = cache_fox_k.shape[2]
    dec_rows = dec_batch * dec_seq

    mem_k_all, mem_v_all = _memory_kv(mem_prompt, g_mem, w_mem_kv.astype(BF16))
    flat_mem = lambda a: a.reshape(-1, N_MEM, MEM_WIDTH)
    pmk, pmv = flat_mem(mem_k_all), flat_mem(mem_v_all)
    smk, smv = flat_mem(cache_mem_k), flat_mem(cache_mem_v)
    hk = cache_fox_k.reshape(depth, dec_batch, past, FOX_WIDTH)
    hv = cache_fox_v.reshape(depth, dec_batch, past, FOX_WIDTH)

    lw = _prepare_weights(g_pre_mix, w_in, b_forget, w_spatial, b_spatial, g_sgu, g_group_out, w_out, g_post_mix,
                          g_pre_ffn, w_up, w_dwconv, b_dwconv, w_down, g_post_ffn, dec_seq)
    hist_rows = _lane_cumsum(jnp.swapaxes(cache_fox_logf, 2, 3).reshape(depth * dec_batch * FOX_HEADS, past),
                             minus_total=True).reshape(depth, dec_batch, FOX_HEADS, past)
    pad_rows = lambda a: jnp.pad(a, ((0, 0), (0, 0), (0, dec_seq - a.shape[2]), (0, 0))
                                 ).reshape(depth, dec_rows, FFN_DIM)
    e1_all = pad_rows(cache_ffn_conv[:, :, 1:2, :])
    e2_all = pad_rows(cache_ffn_conv)
    zeros_hist = jnp.zeros((batch, CONV_WIDTH - 1, FFN_DIM), F32)

    yp = x_prompt
    ys = x_sample.reshape(1, dec_rows, D_MODEL)
    outs = {name: [] for name in ("logf_p", "conv_p", "k_s", "v_s", "logf_s", "gv_s", "conv_s")}
    stacked_kv = ()
    per_b = lambda a: a.reshape(dec_batch, dec_seq, a.shape[-1])
    for l in range(depth):
        qt, k_all, v_all, kb, vt, logf, rest = _in_proj(yp, pmk, pmv, l * batch, 1, lw, IN_PROJ_ROWS,
                                                        lw["ws_prompt"], lw["bt_prompt"], True,
                                                        layer=l, depth=depth, stacked_kv=stacked_kv)
        stacked_kv = (k_all, v_all)
        logf_row = jnp.swapaxes(logf, 1, 2).reshape(batch * FOX_HEADS, seq)
        c_row = _lane_cumsum(logf_row).reshape(batch, FOX_HEADS, seq)
        c_col = jnp.swapaxes(c_row, 1, 2)
        yp, h2 = _fox_prompt(qt, kb, vt, c_col, c_row, rest, yp, lw, l)
        yp, tail = _ffn_prompt(yp, h2, zeros_hist, lw, l)
        outs["logf_p"].append(logf)
        outs["conv_p"].append(tail[:, -1, SUBLANES - (CONV_WIDTH - 1):, :])

        q, k, v, kb, vb, logf, rest, vrows = _in_proj(ys, smk, smv, l * dec_batch, dec_batch, lw, dec_rows,
                                                      lw["ws_sample"], lw["bt_sample"], False,
                                                      layer=l, depth=depth)
        logf_new_row = jnp.swapaxes(per_b(logf), 1, 2).reshape(dec_batch * FOX_HEADS, dec_seq)
        new_row = _lane_cumsum(jnp.pad(logf_new_row, ((0, 0), (0, LANES - dec_seq)))
                               ).reshape(dec_batch, FOX_HEADS, LANES)
        new_col = new_row[:, :, :dec_seq].reshape(dec_batch, FOX_HEADS * dec_seq, 1)
        ys2, h2s = _fox_sample(per_b(q[0]), per_b(kb[0]), per_b(vb[0]), hk, hv, l, hist_rows[l], new_col, new_row,
                               rest[0], ys[0], lw)
        ys_flat, a_all = _ffn_sample(ys2, h2s, e1_all[l], e2_all[l], lw, l, dec_seq)
        ys = ys_flat.reshape(1, dec_rows, D_MODEL)
        outs["k_s"].append(per_b(k[0]).reshape(dec_batch, dec_seq, FOX_HEADS, HEAD_DIM))
        outs["v_s"].append(per_b(v[0]).reshape(dec_batch, dec_seq, FOX_HEADS, HEAD_DIM))
        outs["logf_s"].append(per_b(logf[0]))
        outs["gv_s"].append(per_b(vrows[0]))
        outs["conv_s"].append(a_all.reshape(dec_batch, dec_seq, FFN_DIM)[:, dec_seq - (CONV_WIDTH - 1):, :])

    heads = lambda a: a.reshape(depth, batch, seq, FOX_HEADS, HEAD_DIM)
    mem_heads = lambda a: a.reshape(depth, batch, N_MEM, MEM_HEADS, HEAD_DIM)
    st = {name: jnp.stack(vals) for name, vals in outs.items()}
    return (yp, ys.reshape(dec_batch, dec_seq, D_MODEL),
            heads(stacked_kv[0]), heads(stacked_kv[1]), st["logf_p"], mem_heads(mem_k_all), mem_heads(mem_v_all),
            st["conv_p"], st["k_s"], st["v_s"], st["logf_s"], st["gv_s"], st["conv_s"])
```

```python
import functools

import jax
import jax.numpy as jnp
from jax import lax
from jax.experimental import pallas as pl
from jax.experimental.pallas import tpu as pltpu

D_MODEL = 1024
HEAD_DIM = 64
FOX_WIDTH = 512
FOX_HEADS = 8
SGU_WIDTH = 256
SGU_GROUPS = 4
GMLP_CHUNK = 128
CHUNK = 64
MEM_WIDTH = 256
MEM_HEADS = 4
N_MEM = 256
FFN_DIM = 2816
CONV_WIDTH = 3
RMS_EPS = 1e-6
NEG_INF = -1e30
QK_SCALE = HEAD_DIM ** -0.5

LANES = 128
SUBLANES = 8
HEAD_PAIR = LANES // HEAD_DIM
VMEM_LIMIT_BYTES = 56 * 1024 * 1024

IN_PROJ_ROWS = 512
ATTN_Q_ROWS = 256
ATTN_K_ROWS = 256
FFN_ROWS = 512
FFN_COLS = 256
B_COLS = 896

BF16 = jnp.bfloat16
F32 = jnp.float32


def _rms(x, g):
    y = x * lax.rsqrt(jnp.mean(x * x, axis=-1, keepdims=True) + RMS_EPS)
    return y * g


def _dot(a, b):
    return jnp.dot(a, b, preferred_element_type=F32)


def _dot_nt(a, b):
    return lax.dot_general(a, b, (((1,), (1,)), ((), ())), preferred_element_type=F32)


def _params(*semantics):
    return pltpu.CompilerParams(dimension_semantics=semantics, vmem_limit_bytes=VMEM_LIMIT_BYTES)


def _layer_spec(a, layer, **kwargs):
    return pl.BlockSpec((None,) + a.shape[1:], lambda *_: (layer,) + (0,) * (a.ndim - 1), **kwargs)


def _memkv_kernel(mem_ref, g_ref, w_ref, mk_ref, mv_ref):
    h = _rms(mem_ref[...], g_ref[...]).astype(BF16)
    kv = _dot(h, w_ref[...])
    for hd in range(MEM_HEADS):
        rows = pl.ds(hd, N_MEM, stride=MEM_HEADS)
        mk_ref[rows, :] = kv[:, hd * HEAD_DIM:(hd + 1) * HEAD_DIM]
        mv_ref[rows, :] = kv[:, MEM_WIDTH + hd * HEAD_DIM:MEM_WIDTH + (hd + 1) * HEAD_DIM]


def _memory_kv(mem, g_mem, w_mem_kv_bf):
    depth = g_mem.shape[0]
    batch = mem.shape[0]
    out = jax.ShapeDtypeStruct((depth, batch, N_MEM * MEM_HEADS, HEAD_DIM), F32)
    return pl.pallas_call(
        _memkv_kernel,
        grid=(depth, batch),
        in_specs=[
            pl.BlockSpec((None, N_MEM, D_MODEL), lambda l, b: (b, 0, 0)),
            pl.BlockSpec((None, 1, D_MODEL), lambda l, b: (l, 0, 0)),
            pl.BlockSpec((None, D_MODEL, 2 * MEM_WIDTH), lambda l, b: (l, 0, 0)),
        ],
        out_specs=[
            pl.BlockSpec((None, None, N_MEM * MEM_HEADS, HEAD_DIM), lambda l, b: (l, b, 0, 0)),
            pl.BlockSpec((None, None, N_MEM * MEM_HEADS, HEAD_DIM), lambda l, b: (l, b, 0, 0)),
        ],
        out_shape=[out, out],
        compiler_params=_params("arbitrary", "arbitrary"),
        name="memory_kv",
    )(mem, g_mem.reshape(depth, 1, D_MODEL), w_mem_kv_bf)


def _inproj_kernel(x_ref, gpre_ref, wqkv_ref, wb_ref, bfg_ref, ws_ref, bt_ref, gsgu_ref, gg_ref,
                   mk_ref, mv_ref, *out_refs, transposed, n_stack_in):
    out_refs = out_refs[n_stack_in:]
    tm = x_ref.shape[0]
    n_mem_blocks = mk_ref.shape[0]
    if transposed:
        qt_ref, k_ref, v_ref, kb_ref, vt_ref, lf_ref, rest_ref = out_refs
        vrows_ref = None
    else:
        q_ref, k_ref, v_ref, kb_ref, vb_ref, lf_ref, rest_ref, vrows_ref = out_refs
    h = _rms(x_ref[...], gpre_ref[...]).astype(BF16)

    yb = _dot(h, wb_ref[...])
    fg = yb[:, 2 * SGU_WIDTH + MEM_WIDTH:2 * SGU_WIDTH + MEM_WIDTH + FOX_HEADS]
    lf_ref[...] = jax.nn.log_sigmoid(fg + bfg_ref[...])

    qmb = (yb[:, 2 * SGU_WIDTH:2 * SGU_WIDTH + MEM_WIDTH] * QK_SCALE).astype(BF16)
    rows = tm // n_mem_blocks
    mem_head = lambda ref, bi, hd: ref[bi, pl.ds(hd, N_MEM, stride=MEM_HEADS), :].astype(BF16)
    mem_scores = []
    for bi in range(n_mem_blocks):
        qb = qmb[bi * rows:(bi + 1) * rows, :]
        mem_scores.append([_dot_nt(qb[:, hd * HEAD_DIM:(hd + 1) * HEAD_DIM], mem_head(mk_ref, bi, hd))
                           for hd in range(MEM_HEADS)])

    q = _dot(h, wqkv_ref[:, :FOX_WIDTH]) * QK_SCALE

    z = jax.nn.gelu(yb[:, :2 * SGU_WIDTH])
    u = z[:, :SGU_WIDTH]
    vv = _rms(z[:, SGU_WIDTH:], gsgu_ref[...])
    if vrows_ref is not None:
        vrows_ref[...] = vv
    vvb = vv.astype(BF16)
    mem_exp = [[jnp.exp(s - jnp.max(s, axis=-1, keepdims=True)) for s in per_block] for per_block in mem_scores]

    wrow = lax.broadcasted_iota(jnp.int32, (GMLP_CHUNK, SGU_GROUPS * GMLP_CHUNK), 0)
    wcol = lax.broadcasted_iota(jnp.int32, (GMLP_CHUNK, SGU_GROUPS * GMLP_CHUNK), 1)
    wmask = ((wcol % GMLP_CHUNK) // CHUNK) <= (wrow // CHUNK)
    wcat = jnp.where(wmask, ws_ref[...], 0.0).astype(BF16)
    glane = lax.broadcasted_iota(jnp.int32, (GMLP_CHUNK, SGU_WIDTH), 1) // (SGU_WIDTH // SGU_GROUPS)
    zero_chunk = jnp.zeros((GMLP_CHUNK, SGU_WIDTH), BF16)
    mixed_parts = []
    for c in range(tm // GMLP_CHUNK):
        vc = vvb[c * GMLP_CHUNK:(c + 1) * GMLP_CHUNK, :]
        rhs = jnp.concatenate([jnp.where(glane == g, vc, zero_chunk) for g in range(SGU_GROUPS)], axis=0)
        mixed_parts.append(_dot(wcat, rhs))
    mem_pv = [[_dot(e.astype(BF16), mem_head(mv_ref, bi, hd)) for hd, e in enumerate(mem_exp[bi])]
              for bi in range(n_mem_blocks)]

    k = _dot(h, wqkv_ref[:, FOX_WIDTH:2 * FOX_WIDTH])

    if transposed:
        qt_ref[...] = q.T.astype(BF16)
    else:
        q_ref[...] = q.astype(BF16)
    sgu = jnp.concatenate([u[c * GMLP_CHUNK:(c + 1) * GMLP_CHUNK, :] * (mixed + bt_ref[...])
                           for c, mixed in enumerate(mixed_parts)], axis=0)
    rest_ref[:, :SGU_WIDTH] = _rms(sgu, gg_ref[:, :SGU_WIDTH]).astype(BF16)
    mem_parts = [jnp.concatenate([mem_pv[bi][hd] / jnp.sum(mem_exp[bi][hd], axis=-1, keepdims=True)
                                  for hd in range(MEM_HEADS)], axis=1) for bi in range(n_mem_blocks)]
    mem = mem_parts[0] if n_mem_blocks == 1 else jnp.concatenate(mem_parts, axis=0)
    rest_ref[:, SGU_WIDTH:] = _rms(mem, gg_ref[:, SGU_WIDTH:]).astype(BF16)

    v = _dot(h, wqkv_ref[:, 2 * FOX_WIDTH:])

    k_ref[...] = k
    kb_ref[...] = k.astype(BF16)
    v_ref[...] = v
    if transposed:
        vt = v.T.astype(BF16)
        tk = vt_ref.shape[-1]
        for c in range(tm // tk):
            vt_ref[c] = vt[:, c * tk:(c + 1) * tk]
    else:
        vb_ref[...] = v.astype(BF16)


def _in_proj(x, mem_k, mem_v, mem_first, mem_per_tile, lw, tm, spatial_w, spatial_b, transposed,
             layer, depth, stacked_kv=()):
    nb, t, _ = x.shape
    nt = t // tm
    if transposed:
        assert mem_per_tile == 1
        mem_index = lambda b, i: (mem_first + b, 0, 0)
    else:
        mem_index = lambda b, i: (mem_first // mem_per_tile + b * nt + i, 0, 0)
    row = lambda w: pl.BlockSpec((None, tm, w), lambda b, i: (b, i, 0))
    mem_mode = {} if transposed else {"pipeline_mode": pl.Buffered(1)}
    memspec = pl.BlockSpec((mem_per_tile, N_MEM * MEM_HEADS, HEAD_DIM), mem_index, **mem_mode)
    f32 = lambda w: jax.ShapeDtypeStruct((nb, t, w), F32)
    bf = lambda w: jax.ShapeDtypeStruct((nb, t, w), BF16)
    consts = (lw["g_pre_mix"], lw["w_qkv"], lw["w_b"], lw["b_forget"], spatial_w, spatial_b,
              lw["g_sgu"], lw["g_rest"])
    if transposed:
        tk = ATTN_K_ROWS
        stack_spec = pl.BlockSpec((None, None, tm, FOX_WIDTH), lambda b, i: (layer, b, i, 0))
        stack_shape = jax.ShapeDtypeStruct((depth, nb, t, FOX_WIDTH), F32)
        out_specs = [pl.BlockSpec((None, FOX_WIDTH, tm), lambda b, i: (b, 0, i)),
                     stack_spec, stack_spec, row(FOX_WIDTH),
                     pl.BlockSpec((None, tm // tk, FOX_WIDTH, tk), lambda b, i: (b, i, 0, 0)),
                     row(FOX_HEADS), row(SGU_WIDTH + MEM_WIDTH)]
        out_shape = [jax.ShapeDtypeStruct((nb, FOX_WIDTH, t), BF16), stack_shape, stack_shape,
                     bf(FOX_WIDTH), jax.ShapeDtypeStruct((nb, t // tk, FOX_WIDTH, tk), BF16),
                     f32(FOX_HEADS), bf(SGU_WIDTH + MEM_WIDTH)]
    else:
        out_specs = [row(FOX_WIDTH)] * 5 + [row(FOX_HEADS), row(SGU_WIDTH + MEM_WIDTH), row(SGU_WIDTH)]
        out_shape = [bf(FOX_WIDTH), f32(FOX_WIDTH), f32(FOX_WIDTH), bf(FOX_WIDTH), bf(FOX_WIDTH),
                     f32(FOX_HEADS), bf(SGU_WIDTH + MEM_WIDTH), f32(SGU_WIDTH)]
    in_specs = [row(D_MODEL)] + [_layer_spec(a, layer) for a in consts] + [memspec, memspec]
    aliases = {len(in_specs) + n: 1 + n for n in range(len(stacked_kv))}
    in_specs += [pl.BlockSpec(memory_space=pl.ANY)] * len(stacked_kv)
    return pl.pallas_call(
        functools.partial(_inproj_kernel, transposed=transposed, n_stack_in=len(stacked_kv)),
        grid=(nb, nt),
        in_specs=in_specs,
        out_specs=out_specs,
        out_shape=out_shape,
        input_output_aliases=aliases,
        compiler_params=_params("parallel", "arbitrary"),
        name="in_proj",
    )(x, *consts, mem_k, mem_v, *stacked_kv)


def _cumsum_kernel(x_ref, o_ref, *, minus_total):
    x = x_ref[...]
    n = x.shape[-1]
    lane = lax.broadcasted_iota(jnp.int32, x.shape, 1)
    shift = 1
    while shift < n:
        x = x + jnp.where(lane >= shift, pltpu.roll(x, shift, 1), 0.0)
        shift *= 2
    if minus_total:
        x = x - x[:, n - 1:n]
    o_ref[...] = x


def _lane_cumsum(x, minus_total=False):
    return pl.pallas_call(
        functools.partial(_cumsum_kernel, minus_total=minus_total),
        out_shape=jax.ShapeDtypeStruct(x.shape, F32),
        name="lane_cumsum",
    )(x)


def _mix_out(fox, rest, x, wout_ref, gfox_ref, gpost_ref, gffn_ref, o_ref, h2_ref):
    fox_n = _rms(fox, gfox_ref[...]).astype(BF16)
    y = _dot(fox_n, wout_ref[:FOX_WIDTH, :]) + _dot(rest, wout_ref[FOX_WIDTH:, :])
    x_mid = x + _rms(y, gpost_ref[...])
    o_ref[...] = x_mid
    h2_ref[...] = _rms(x_mid, gffn_ref[...]).astype(BF16)


def _split3(c):
    hi = c.astype(BF16).astype(F32)
    r = c - hi
    mid = r.astype(BF16).astype(F32)
    lo = (r - mid).astype(BF16).astype(F32)
    return hi, mid, lo


_BIAS_ONES = 6


def _fox_prompt_kernel(qt_ref, kb_ref, vt_ref, ccol_ref, crow_ref, rest_ref, x_ref, wout_ref, gfox_ref,
                       gpost_ref, gffn_ref, o_ref, h2_ref, kaug_scr, qa_scr, sa_scr, sb_scr, m_scr, l_scr, acc_scr):
    tq = qt_ref.shape[1]
    tk = vt_ref.shape[-1]
    s_len = kb_ref.shape[0]
    slab = 2 * LANES
    i = pl.program_id(1)

    @pl.when(i == 0)
    def _():
        n_slab = FOX_HEADS // HEAD_PAIR
        src = lax.broadcasted_iota(jnp.int32, (LANES, n_slab * LANES), 0)
        dst = lax.broadcasted_iota(jnp.int32, (LANES, n_slab * LANES), 1)
        head, piece = src % FOX_HEADS, src // FOX_HEADS
        piece_dst = (head // HEAD_PAIR) * LANES + 3 * (head % HEAD_PAIR) + piece
        is_piece = (src < 3 * FOX_HEADS) & (dst == piece_dst)
        is_one = (src == 3 * FOX_HEADS) & (dst % LANES >= _BIAS_ONES) & (dst % LANES < _BIAS_ONES + 3)
        place = jnp.where(is_piece | is_one, 1.0, 0.0).astype(BF16)
        pad = jnp.concatenate([jnp.ones((tk, 1), F32), jnp.zeros((tk, LANES - 3 * FOX_HEADS - 1), F32)], axis=1)
        for r in range(s_len // tk):
            rows = slice(r * tk, (r + 1) * tk)
            pieces = jnp.concatenate(_split3(ccol_ref[rows, :]) + (pad,), axis=1)
            bias = _dot(pieces.astype(BF16), place)
            for p in range(n_slab):
                kaug_scr[rows, p * slab:p * slab + LANES] = kb_ref[rows, p * LANES:(p + 1) * LANES]
                kaug_scr[rows, p * slab + LANES:(p + 1) * slab] = bias[:, p * LANES:(p + 1) * LANES].astype(BF16)

    rowi = lax.broadcasted_iota(jnp.int32, (LANES, tq), 0)
    diag = lax.broadcasted_iota(jnp.int32, (tk, tq), 0) <= lax.broadcasted_iota(jnp.int32, (tk, tq), 1)
    crow = crow_ref[...]
    zero_q = jnp.zeros((LANES, tq), BF16)
    for p in range(FOX_HEADS // HEAD_PAIR):
        qtp = qt_ref[p * LANES:(p + 1) * LANES, :]
        for hh in range(HEAD_PAIR):
            own = (rowi >= hh * HEAD_DIM) & (rowi < (hh + 1) * HEAD_DIM)
            chi, cmid, clo = _split3(crow[2 * p + hh:2 * p + hh + 1, :])
            br = jnp.where(rowi == _BIAS_ONES, chi,
                           jnp.where(rowi == _BIAS_ONES + 1, cmid, jnp.where(rowi == _BIAS_ONES + 2, clo, 0.0)))
            br = jnp.where((rowi >= 3 * hh) & (rowi < 3 * hh + 3), -1.0, br)
            qa_scr[2 * p + hh, :LANES, :] = jnp.where(own, qtp, zero_q)
            qa_scr[2 * p + hh, LANES:, :] = br.astype(BF16)

    m_scr[...] = jnp.full(m_scr.shape, NEG_INF, F32)
    l_scr[...] = jnp.zeros(l_scr.shape, F32)
    acc_scr[...] = jnp.zeros(acc_scr.shape, F32)

    def scores_to(buf_ref, j):
        off = pl.multiple_of(j * tk, tk)
        for h in range(FOX_HEADS):
            ka = kaug_scr[pl.ds(off, tk), (h // HEAD_PAIR) * slab:(h // HEAD_PAIR + 1) * slab]
            buf_ref[h] = _dot(ka, qa_scr[h])

    def absorb_from(buf_ref, j, mask):
        vt = vt_ref[j]
        for h in range(FOX_HEADS):
            feat = slice(h * HEAD_DIM, (h + 1) * HEAD_DIM)
            s = buf_ref[h]
            if mask is not None:
                s = jnp.where(mask, s, NEG_INF)
            m = m_scr[h:h + 1, :]
            m_new = jnp.maximum(m, jnp.max(s, axis=0, keepdims=True))
            alpha = jnp.exp(m - m_new)
            e = jnp.exp(s - m_new)
            m_scr[h:h + 1, :] = m_new
            l_scr[h:h + 1, :] = alpha * l_scr[h:h + 1, :] + jnp.sum(e, axis=0, keepdims=True)
            acc_scr[feat, :] = alpha * acc_scr[feat, :] + _dot(vt[feat, :], e.astype(BF16))

    odd = i % 2

    @pl.when(odd == 1)
    def _():
        scores_to(sb_scr, 0)
        scores_to(sa_scr, 1)
        absorb_from(sb_scr, 0, None)

    @pl.when(odd == 0)
    def _():
        scores_to(sa_scr, 0)

    def step(jj, _):
        t = odd + 2 * jj
        scores_to(sb_scr, t + 1)
        absorb_from(sa_scr, t, None)
        scores_to(sa_scr, t + 2)
        absorb_from(sb_scr, t + 1, None)
        return 0

    lax.fori_loop(0, i // 2, step, 0)
    absorb_from(sa_scr, i, diag)
    fox_t = jnp.concatenate([acc_scr[h * HEAD_DIM:(h + 1) * HEAD_DIM, :] / l_scr[h:h + 1, :]
                             for h in range(FOX_HEADS)], axis=0)
    fox = fox_t.T
    _mix_out(fox, rest_ref[...], x_ref[...], wout_ref, gfox_ref, gpost_ref, gffn_ref, o_ref, h2_ref)


def _mix_weights(lw):
    return (lw["w_out"], lw["g_fox"], lw["g_post_mix"], lw["g_pre_ffn"])


def _fox_prompt(qt, kb, vt, c_col, c_row, rest, x, lw, layer):
    nb, s, _ = x.shape
    tq, tk = ATTN_Q_ROWS, ATTN_K_ROWS
    assert tq == tk and vt.shape[-1] == tk
    nq = s // tq
    qrow = lambda w: pl.BlockSpec((None, tq, w), lambda b, i: (b, i, 0))
    qcol = lambda r: pl.BlockSpec((None, r, tq), lambda b, i: (b, 0, i))
    full = lambda a: pl.BlockSpec((None,) + a.shape[1:], lambda b, i: (b,) + (0,) * (a.ndim - 1))
    return pl.pallas_call(
        _fox_prompt_kernel,
        grid=(nb, nq),
        in_specs=[qcol(FOX_WIDTH), full(kb), full(vt), full(c_col), qcol(FOX_HEADS),
                  qrow(SGU_WIDTH + MEM_WIDTH), qrow(D_MODEL)] + [_layer_spec(a, layer) for a in _mix_weights(lw)],
        out_specs=[qrow(D_MODEL), qrow(D_MODEL)],
        out_shape=[jax.ShapeDtypeStruct(x.shape, F32), jax.ShapeDtypeStruct(x.shape, BF16)],
        scratch_shapes=[pltpu.VMEM((s, 2 * FOX_WIDTH), BF16),
                        pltpu.VMEM((FOX_HEADS, 2 * LANES, tq), BF16),
                        pltpu.VMEM((FOX_HEADS, tk, tq), F32), pltpu.VMEM((FOX_HEADS, tk, tq), F32),
                        pltpu.VMEM((FOX_HEADS, tq), F32), pltpu.VMEM((FOX_HEADS, tq), F32),
                        pltpu.VMEM((FOX_WIDTH, tq), F32)],
        compiler_params=_params("parallel", "arbitrary"),
        name="fox_mix_prompt",
    )(qt, kb, vt, c_col, c_row, rest, x, *_mix_weights(lw))


def _fox_sample_kernel(q_ref, kn_ref, vn_ref, hk_ref, hv_ref, hrow_ref, ncol_ref, nrow_ref, rest_ref, x_ref,
                       wout_ref, gfox_ref, gpost_ref, gffn_ref, o_ref, h2_ref, fox_scr):
    t = q_ref.shape[0]
    past = hk_ref.shape[0] // FOX_HEADS
    b = pl.program_id(0)
    causal = lax.broadcasted_iota(jnp.int32, (t, t), 1) <= lax.broadcasted_iota(jnp.int32, (t, t), 0)
    feat = lambda h: slice(h * HEAD_DIM, (h + 1) * HEAD_DIM)
    head_rows = lambda ref, h: ref[pl.ds(h, past, stride=FOX_HEADS), :].astype(BF16)
    scores = []
    for h in range(FOX_HEADS):
        q_h = q_ref[:, feat(h)]
        scores.append((_dot_nt(q_h, head_rows(hk_ref, h)), _dot_nt(q_h, kn_ref[:, feat(h)])))
    probs = []
    for h, (s_hist, s_new) in enumerate(scores):
        bq = ncol_ref[h * t:(h + 1) * t, :]
        s_hist = s_hist + bq - hrow_ref[h:h + 1, :]
        s_new = jnp.where(causal, s_new + bq - nrow_ref[h:h + 1, :t], NEG_INF)
        m = jnp.maximum(jnp.max(s_hist, axis=-1, keepdims=True), jnp.max(s_new, axis=-1, keepdims=True))
        e_hist = jnp.exp(s_hist - m)
        e_new = jnp.exp(s_new - m)
        l = jnp.sum(e_hist, axis=-1, keepdims=True) + jnp.sum(e_new, axis=-1, keepdims=True)
        probs.append((e_hist.astype(BF16), e_new.astype(BF16), l))
    fox_parts = [(_dot(e_hist, head_rows(hv_ref, h)) + _dot(e_new, vn_ref[:, feat(h)])) / l
                 for h, (e_hist, e_new, l) in enumerate(probs)]
    fox_scr[pl.ds(pl.multiple_of(b * t, t), t), :] = jnp.concatenate(fox_parts, axis=1)

    @pl.when(b == pl.num_programs(0) - 1)
    def _():
        _mix_out(fox_scr[...], rest_ref[...], x_ref[...], wout_ref, gfox_ref, gpost_ref, gffn_ref, o_ref, h2_ref)


def _fox_sample(q, kb, vb, hk, hv, layer, hist_row, new_col, new_row, rest, x, lw):
    nb, t, _ = q.shape
    per_b = lambda a: pl.BlockSpec((None,) + a.shape[1:], lambda b: (b,) + (0,) * (a.ndim - 1))
    cache = pl.BlockSpec((None, None) + hk.shape[2:], lambda b: (layer, b, 0, 0))
    const = lambda a: pl.BlockSpec(a.shape, lambda b: (0,) * a.ndim)
    return pl.pallas_call(
        _fox_sample_kernel,
        grid=(nb,),
        in_specs=[per_b(q), per_b(kb), per_b(vb), cache, cache, per_b(hist_row), per_b(new_col),
                  per_b(new_row), const(rest), const(x)] + [_layer_spec(a, layer) for a in _mix_weights(lw)],
        out_specs=[const(x), const(x)],
        out_shape=[jax.ShapeDtypeStruct(x.shape, F32), jax.ShapeDtypeStruct(x.shape, BF16)],
        scratch_shapes=[pltpu.VMEM((nb * t, FOX_WIDTH), F32)],
        compiler_params=_params("arbitrary"),
        name="fox_mix_sample",
    )(q, kb, vb, hk, hv, hist_row, new_col, new_row, rest, x, *_mix_weights(lw))


def _ffn_body(x_ref, h2_ref, wup_ref, wdw_ref, bdw_ref, wd_ref, gpost_ref, o_ref, g_scr, shifted, emit):
    h2 = h2_ref[...]
    nf = FFN_DIM // FFN_COLS

    def up(c):
        conv_cols = slice(c * FFN_COLS, (c + 1) * FFN_COLS)
        lin_cols = slice(FFN_DIM + c * FFN_COLS, FFN_DIM + (c + 1) * FFN_COLS)
        return _dot(h2, wup_ref[:, conv_cols]), _dot(h2, wup_ref[:, lin_cols])

    nxt = up(0)
    for c in range(nf):
        a, lin = nxt
        if c + 1 < nf:
            nxt = up(c + 1)
        cols = slice(c * FFN_COLS, (c + 1) * FFN_COLS)
        a1, a2 = shifted(a, c)
        conv = bdw_ref[:, cols] + wdw_ref[0:1, cols] * a2
        conv = conv + wdw_ref[1:2, cols] * a1
        conv = conv + wdw_ref[2:3, cols] * a
        g_scr[:, cols] = (jax.nn.silu(conv) * lin).astype(BF16)
        emit(a, c)
    o_ref[...] = x_ref[...] + _rms(_dot(g_scr[...], wd_ref[...]), gpost_ref[...])


def _ffn_prompt_kernel(x_ref, h2_ref, wup_ref, wdw_ref, bdw_ref, wd_ref, gpost_ref, hist_ref,
                       o_ref, tail_ref, g_scr, carry_scr, work_scr):
    tm = x_ref.shape[0]
    head = SUBLANES

    @pl.when(pl.program_id(1) == 0)
    def _():
        carry_scr[0:head - (CONV_WIDTH - 1), :] = jnp.zeros((head - (CONV_WIDTH - 1), FFN_DIM), F32)
        carry_scr[head - (CONV_WIDTH - 1):head, :] = hist_ref[...]

    def shifted(a, c):
        cols = slice(c * FFN_COLS, (c + 1) * FFN_COLS)
        work = work_scr.at[c % 2]
        work[0:head, :] = carry_scr[:, cols]
        work[head:head + tm, :] = a
        return work[head - 1:head - 1 + tm, :], work[head - 2:head - 2 + tm, :]

    def emit(a, c):
        cols = slice(c * FFN_COLS, (c + 1) * FFN_COLS)
        carry_scr[:, cols] = a[tm - head:, :]
        tail_ref[:, cols] = a[tm - head:, :]

    _ffn_body(x_ref, h2_ref, wup_ref, wdw_ref, bdw_ref, wd_ref, gpost_ref, o_ref, g_scr, shifted, emit)


def _ffn_sample_kernel(x_ref, h2_ref, wup_ref, wdw_ref, bdw_ref, wd_ref, gpost_ref, e1_ref, e2_ref,
                       o_ref, a_ref, g_scr, work_scr, *, seg):
    tm = x_ref.shape[0]
    head = SUBLANES
    rmod = lax.broadcasted_iota(jnp.int32, (tm, FFN_COLS), 0) % seg

    def shifted(a, c):
        cols = slice(c * FFN_COLS, (c + 1) * FFN_COLS)
        work = work_scr.at[c % 2]
        work[0:head, :] = jnp.zeros((head, FFN_COLS), F32)
        work[head:head + tm, :] = a
        a1 = jnp.where(rmod >= 1, work[head - 1:head - 1 + tm, :], e1_ref[:, cols])
        a2 = jnp.where(rmod >= 2, work[head - 2:head - 2 + tm, :], e2_ref[:, cols])
        return a1, a2

    def emit(a, c):
        a_ref[:, c * FFN_COLS:(c + 1) * FFN_COLS] = a

    _ffn_body(x_ref, h2_ref, wup_ref, wdw_ref, bdw_ref, wd_ref, gpost_ref, o_ref, g_scr, shifted, emit)


def _ffn_weights(lw):
    return (lw["w_up"], lw["w_dwconv"], lw["b_dwconv"], lw["w_down"], lw["g_post_ffn"])


def _ffn_weight_specs(lw, layer):
    return [_layer_spec(a, layer, pipeline_mode=pl.Buffered(1)) for a in _ffn_weights(lw)]


def _ffn_prompt(x, h2, hist, lw, layer):
    nb, s, _ = x.shape
    tm = FFN_ROWS
    nt = s // tm
    row = pl.BlockSpec((None, tm, D_MODEL), lambda b, i: (b, i, 0))
    return pl.pallas_call(
        _ffn_prompt_kernel,
        grid=(nb, nt),
        in_specs=[row, row] + _ffn_weight_specs(lw, layer) + [
            pl.BlockSpec((None, CONV_WIDTH - 1, FFN_DIM), lambda b, i: (b, 0, 0))],
        out_specs=[pl.BlockSpec((None, tm, D_MODEL), lambda b, i: (b, i, 0)),
                   pl.BlockSpec((None, None, SUBLANES, FFN_DIM), lambda b, i: (b, i, 0, 0))],
        out_shape=[jax.ShapeDtypeStruct(x.shape, F32),
                   jax.ShapeDtypeStruct((nb, nt, SUBLANES, FFN_DIM), F32)],
        scratch_shapes=[pltpu.VMEM((tm, FFN_DIM), BF16),
                        pltpu.VMEM((SUBLANES, FFN_DIM), F32),
                        pltpu.VMEM((2, tm + SUBLANES, FFN_COLS), F32)],
        compiler_params=_params("parallel", "arbitrary"),
        name="conv_ffn_prompt",
    )(x, h2, *_ffn_weights(lw), hist)


def _ffn_sample(x, h2, e1, e2, lw, layer, seg):
    rows, _ = x.shape
    whole = lambda w: pl.BlockSpec((rows, w), lambda i: (0, 0))
    return pl.pallas_call(
        functools.partial(_ffn_sample_kernel, seg=seg),
        grid=(1,),
        in_specs=[whole(D_MODEL), whole(D_MODEL)] + _ffn_weight_specs(lw, layer) + [whole(FFN_DIM), whole(FFN_DIM)],
        out_specs=[whole(D_MODEL), whole(FFN_DIM)],
        out_shape=[jax.ShapeDtypeStruct(x.shape, F32), jax.ShapeDtypeStruct((rows, FFN_DIM), F32)],
        scratch_shapes=[pltpu.VMEM((rows, FFN_DIM), BF16),
                        pltpu.VMEM((2, rows + SUBLANES, FFN_COLS), F32)],
        compiler_params=_params("arbitrary"),
        name="conv_ffn_sample",
    )(x, h2, *_ffn_weights(lw), e1, e2)


def _prepare_weights(g_pre_mix, w_in, b_forget, w_spatial, b_spatial, g_sgu, g_group_out, w_out, g_post_mix,
                     g_pre_ffn, w_up, w_dwconv, b_dwconv, w_down, g_post_ffn, dec_seq):
    depth = w_in.shape[0]
    row = lambda a: a.reshape(depth, 1, -1)
    c0 = 3 * FOX_WIDTH
    c1 = c0 + FOX_HEADS
    c2 = c1 + 2 * SGU_WIDTH
    w_b = jnp.concatenate([w_in[:, :, c1:c2], w_in[:, :, c2:], w_in[:, :, c0:c1],
                           jnp.zeros((depth, D_MODEL, B_COLS - (2 * SGU_WIDTH + MEM_WIDTH + FOX_HEADS)), F32)],
                          axis=2)
    group_dim = SGU_WIDTH // SGU_GROUPS
    ws_prompt = jnp.concatenate([w_spatial[:, g] for g in range(SGU_GROUPS)], axis=2)
    bt_prompt = jnp.repeat(jnp.swapaxes(b_spatial, 1, 2), group_dim, axis=2)
    reps = GMLP_CHUNK // dec_seq
    blk = (jnp.arange(GMLP_CHUNK)[:, None] // dec_seq) == (jnp.arange(GMLP_CHUNK)[None, :] // dec_seq)
    ws_sample = jnp.concatenate(
        [jnp.where(blk, jnp.tile(w_spatial[:, g, :dec_seq, :dec_seq], (1, reps, reps)), 0.0)
         for g in range(SGU_GROUPS)], axis=2)
    bt_sample = jnp.tile(jnp.repeat(jnp.swapaxes(b_spatial[:, :, :dec_seq], 1, 2), group_dim, axis=2),
                         (1, reps, 1))
    return {
        "g_pre_mix": row(g_pre_mix), "w_qkv": w_in[:, :, :c0].astype(BF16), "w_b": w_b.astype(BF16),
        "b_forget": row(b_forget), "g_sgu": row(g_sgu),
        "g_fox": row(g_group_out[:, :FOX_WIDTH]), "g_rest": row(g_group_out[:, FOX_WIDTH:]),
        "w_out": w_out.astype(BF16), "g_post_mix": row(g_post_mix), "g_pre_ffn": row(g_pre_ffn),
        "w_up": w_up.astype(BF16), "w_dwconv": w_dwconv, "b_dwconv": row(b_dwconv),
        "w_down": w_down.astype(BF16), "g_post_ffn": row(g_post_ffn),
        "ws_prompt": ws_prompt, "bt_prompt": bt_prompt, "ws_sample": ws_sample, "bt_sample": bt_sample,
    }


def kernel(x_prompt, x_sample, mem_prompt, cache_fox_k, cache_fox_v, cache_fox_logf, cache_mem_k, cache_mem_v,
           cache_ffn_conv, g_pre_mix, w_in, b_forget, w_spatial, b_spatial, g_sgu, g_mem, w_mem_kv, g_group_out,
           w_out, g_post_mix, g_pre_ffn, w_up, w_dwconv, b_dwconv, w_down, g_post_ffn):
    depth = w_in.shape[0]
    batch, seq, _ = x_prompt.shape
    dec_batch, dec_seq, _ = x_sample.shape
    past = cache_fox_k.shape[2]
    dec_rows = dec_batch * dec_seq

    mem_k_all, mem_v_all = _memory_kv(mem_prompt, g_mem, w_mem_kv.astype(BF16))
    flat_mem = lambda a: a.reshape(-1, N_MEM * MEM_HEADS, HEAD_DIM)
    pmk, pmv = flat_mem(mem_k_all), flat_mem(mem_v_all)
    smk, smv = flat_mem(cache_mem_k), flat_mem(cache_mem_v)
    hk = cache_fox_k.reshape(depth, dec_batch, past * FOX_HEADS, HEAD_DIM)
    hv = cache_fox_v.reshape(depth, dec_batch, past * FOX_HEADS, HEAD_DIM)

    lw = _prepare_weights(g_pre_mix, w_in, b_forget, w_spatial, b_spatial, g_sgu, g_group_out, w_out, g_post_mix,
                          g_pre_ffn, w_up, w_dwconv, b_dwconv, w_down, g_post_ffn, dec_seq)
    hist_rows = _lane_cumsum(jnp.swapaxes(cache_fox_logf, 2, 3).reshape(depth * dec_batch * FOX_HEADS, past),
                             minus_total=True).reshape(depth, dec_batch, FOX_HEADS, past)
    pad_rows = lambda a: jnp.pad(a, ((0, 0), (0, 0), (0, dec_seq - a.shape[2]), (0, 0))
                                 ).reshape(depth, dec_rows, FFN_DIM)
    e1_all = pad_rows(cache_ffn_conv[:, :, 1:2, :])
    e2_all = pad_rows(cache_ffn_conv)
    zeros_hist = jnp.zeros((batch, CONV_WIDTH - 1, FFN_DIM), F32)

    yp = x_prompt
    ys = x_sample.reshape(1, dec_rows, D_MODEL)
    outs = {name: [] for name in ("logf_p", "conv_p", "k_s", "v_s", "logf_s", "gv_s", "conv_s")}
    stacked_kv = ()
    per_b = lambda a: a.reshape(dec_batch, dec_seq, a.shape[-1])
    for l in range(depth):
        qt, k_all, v_all, kb, vt, logf, rest = _in_proj(yp, pmk, pmv, l * batch, 1, lw, IN_PROJ_ROWS,
                                                        lw["ws_prompt"], lw["bt_prompt"], True,
                                                        layer=l, depth=depth, stacked_kv=stacked_kv)
        stacked_kv = (k_all, v_all)
        logf_row = jnp.swapaxes(logf, 1, 2).reshape(batch * FOX_HEADS, seq)
        c_row = _lane_cumsum(logf_row).reshape(batch, FOX_HEADS, seq)
        c_col = jnp.swapaxes(c_row, 1, 2)
        yp, h2 = _fox_prompt(qt, kb, vt, c_col, c_row, rest, yp, lw, l)
        yp, tail = _ffn_prompt(yp, h2, zeros_hist, lw, l)
        outs["logf_p"].append(logf)
        outs["conv_p"].append(tail[:, -1, SUBLANES - (CONV_WIDTH - 1):, :])

        q, k, v, kb, vb, logf, rest, vrows = _in_proj(ys, smk, smv, l * dec_batch, dec_batch, lw, dec_rows,
                                                      lw["ws_sample"], lw["bt_sample"], False,
                                                      layer=l, depth=depth)
        logf_new_row = jnp.swapaxes(per_b(logf), 1, 2).reshape(dec_batch * FOX_HEADS, dec_seq)
        new_row = _lane_cumsum(jnp.pad(logf_new_row, ((0, 0), (0, LANES - dec_seq)))
                               ).reshape(dec_batch, FOX_HEADS, LANES)
        new_col = new_row[:, :, :dec_seq].reshape(dec_batch, FOX_HEADS * dec_seq, 1)
        ys2, h2s = _fox_sample(per_b(q[0]), per_b(kb[0]), per_b(vb[0]), hk, hv, l, hist_rows[l], new_col, new_row,
                               rest[0], ys[0], lw)
        ys_flat, a_all = _ffn_sample(ys2, h2s, e1_all[l], e2_all[l], lw, l, dec_seq)
        ys = ys_flat.reshape(1, dec_rows, D_MODEL)
        outs["k_s"].append(per_b(k[0]).reshape(dec_batch, dec_seq, FOX_HEADS, HEAD_DIM))
        outs["v_s"].append(per_b(v[0]).reshape(dec_batch, dec_seq, FOX_HEADS, HEAD_DIM))
        outs["logf_s"].append(per_b(logf[0]))
        outs["gv_s"].append(per_b(vrows[0]))
        outs["conv_s"].append(a_all.reshape(dec_batch, dec_seq, FFN_DIM)[:, dec_seq - (CONV_WIDTH - 1):, :])

    heads = lambda a: a.reshape(depth, batch, seq, FOX_HEADS, HEAD_DIM)
    mem_heads = lambda a: a.reshape(depth, batch, N_MEM, MEM_HEADS, HEAD_DIM)
    st = {name: jnp.stack(vals) for name, vals in outs.items()}
    return (yp, ys.reshape(dec_batch, dec_seq, D_MODEL),
            heads(stacked_kv[0]), heads(stacked_kv[1]), st["logf_p"], mem_heads(mem_k_all), mem_heads(mem_v_all),
            st["conv_p"], st["k_s"], st["v_s"], st["logf_s"], st["gv_s"], st["conv_s"])
```

```python
import functools

import jax
import jax.numpy as jnp
from jax import lax
from jax.experimental import pallas as pl
from jax.experimental.pallas import tpu as pltpu

D_MODEL = 1024
HEAD_DIM = 64
FOX_WIDTH = 512
FOX_HEADS = 8
SGU_WIDTH = 256
SGU_GROUPS = 4
GMLP_CHUNK = 128
CHUNK = 64
MEM_WIDTH = 256
MEM_HEADS = 4
N_MEM = 256
FFN_DIM = 2816
CONV_WIDTH = 3
RMS_EPS = 1e-6
NEG_INF = -1e30
QK_SCALE = HEAD_DIM ** -0.5

LANES = 128
SUBLANES = 8
HEAD_PAIR = LANES // HEAD_DIM
VMEM_LIMIT_BYTES = 56 * 1024 * 1024

IN_PROJ_ROWS = 512
ATTN_Q_ROWS = 256
ATTN_K_ROWS = 256
FFN_ROWS = 512
FFN_COLS = 256
B_COLS = 896

BF16 = jnp.bfloat16
F32 = jnp.float32


def _rms(x, g):
    y = x * lax.rsqrt(jnp.mean(x * x, axis=-1, keepdims=True) + RMS_EPS)
    return y * g


def _dot(a, b):
    return jnp.dot(a, b, preferred_element_type=F32)


def _dot_nt(a, b):
    return lax.dot_general(a, b, (((1,), (1,)), ((), ())), preferred_element_type=F32)


def _params(*semantics):
    return pltpu.CompilerParams(dimension_semantics=semantics, vmem_limit_bytes=VMEM_LIMIT_BYTES)


def _layer_spec(a, layer, **kwargs):
    return pl.BlockSpec((None,) + a.shape[1:], lambda *_: (layer,) + (0,) * (a.ndim - 1), **kwargs)


def _memkv_kernel(mem_ref, g_ref, w_ref, mk_ref, mv_ref):
    h = _rms(mem_ref[...], g_ref[...]).astype(BF16)
    kv = _dot(h, w_ref[...])
    mk_ref[...] = kv[:, :MEM_WIDTH].T
    mv_ref[...] = kv[:, MEM_WIDTH:].T


def _memory_kv(mem, g_mem, w_mem_kv_bf):
    depth = g_mem.shape[0]
    batch = mem.shape[0]
    out = jax.ShapeDtypeStruct((depth, batch, MEM_WIDTH, N_MEM), F32)
    return pl.pallas_call(
        _memkv_kernel,
        grid=(depth, batch),
        in_specs=[
            pl.BlockSpec((None, N_MEM, D_MODEL), lambda l, b: (b, 0, 0)),
            pl.BlockSpec((None, 1, D_MODEL), lambda l, b: (l, 0, 0)),
            pl.BlockSpec((None, D_MODEL, 2 * MEM_WIDTH), lambda l, b: (l, 0, 0)),
        ],
        out_specs=[
            pl.BlockSpec((None, None, MEM_WIDTH, N_MEM), lambda l, b: (l, b, 0, 0)),
            pl.BlockSpec((None, None, MEM_WIDTH, N_MEM), lambda l, b: (l, b, 0, 0)),
        ],
        out_shape=[out, out],
        compiler_params=_params("arbitrary", "arbitrary"),
        name="memory_kv",
    )(mem, g_mem.reshape(depth, 1, D_MODEL), w_mem_kv_bf)


def _inproj_kernel(x_ref, gpre_ref, wqkv_ref, wb_ref, bfg_ref, ws_ref, bt_ref, gsgu_ref, gg_ref,
                   mk_ref, mv_ref, *out_refs, transposed, n_stack_in):
    out_refs = out_refs[n_stack_in:]
    tm = x_ref.shape[0]
    n_mem_blocks = mk_ref.shape[0]
    if transposed:
        qt_ref, k_ref, v_ref, kb_ref, vt_ref, lf_ref, rest_ref = out_refs
        vrows_ref = None
    else:
        q_ref, k_ref, v_ref, kb_ref, vb_ref, lf_ref, rest_ref, vrows_ref = out_refs
    h = _rms(x_ref[...], gpre_ref[...]).astype(BF16)

    yb = _dot(h, wb_ref[...])
    fg = yb[:, 2 * SGU_WIDTH + MEM_WIDTH:2 * SGU_WIDTH + MEM_WIDTH + FOX_HEADS]
    lf_ref[...] = jax.nn.log_sigmoid(fg + bfg_ref[...])

    qmb = (yb[:, 2 * SGU_WIDTH:2 * SGU_WIDTH + MEM_WIDTH] * QK_SCALE).astype(BF16)
    rows = tm // n_mem_blocks
    hlane = lax.broadcasted_iota(jnp.int32, (rows, MEM_WIDTH), 1) // HEAD_DIM
    zero_q = jnp.zeros((rows, MEM_WIDTH), BF16)
    mem_scores = []
    for bi in range(n_mem_blocks):
        qb = qmb[bi * rows:(bi + 1) * rows, :]
        mkt = mk_ref[bi].astype(BF16)
        mem_scores.append([_dot(jnp.where(hlane == hd, qb, zero_q), mkt) for hd in range(MEM_HEADS)])

    q = _dot(h, wqkv_ref[:, :FOX_WIDTH]) * QK_SCALE

    z = jax.nn.gelu(yb[:, :2 * SGU_WIDTH])
    u = z[:, :SGU_WIDTH]
    vv = _rms(z[:, SGU_WIDTH:], gsgu_ref[...])
    if vrows_ref is not None:
        vrows_ref[...] = vv
    vvb = vv.astype(BF16)
    mem_exp = [[jnp.exp(s - jnp.max(s, axis=-1, keepdims=True)) for s in per_block] for per_block in mem_scores]

    wrow = lax.broadcasted_iota(jnp.int32, (GMLP_CHUNK, SGU_GROUPS * GMLP_CHUNK), 0)
    wcol = lax.broadcasted_iota(jnp.int32, (GMLP_CHUNK, SGU_GROUPS * GMLP_CHUNK), 1)
    wmask = ((wcol % GMLP_CHUNK) // CHUNK) <= (wrow // CHUNK)
    wcat = jnp.where(wmask, ws_ref[...], 0.0).astype(BF16)
    glane = lax.broadcasted_iota(jnp.int32, (GMLP_CHUNK, SGU_WIDTH), 1) // (SGU_WIDTH // SGU_GROUPS)
    zero_chunk = jnp.zeros((GMLP_CHUNK, SGU_WIDTH), BF16)
    mixed_parts = []
    for c in range(tm // GMLP_CHUNK):
        vc = vvb[c * GMLP_CHUNK:(c + 1) * GMLP_CHUNK, :]
        rhs = jnp.concatenate([jnp.where(glane == g, vc, zero_chunk) for g in range(SGU_GROUPS)], axis=0)
        mixed_parts.append(_dot(wcat, rhs))
    mem_pv = []
    for bi in range(n_mem_blocks):
        mvt = mv_ref[bi].astype(BF16)
        mem_pv.append([_dot_nt(e.astype(BF16), mvt) for e in mem_exp[bi]])

    k = _dot(h, wqkv_ref[:, FOX_WIDTH:2 * FOX_WIDTH])

    if transposed:
        qt_ref[...] = q.T.astype(BF16)
    else:
        q_ref[...] = q.astype(BF16)
    sgu = jnp.concatenate([u[c * GMLP_CHUNK:(c + 1) * GMLP_CHUNK, :] * (mixed + bt_ref[...])
                           for c, mixed in enumerate(mixed_parts)], axis=0)
    rest_ref[:, :SGU_WIDTH] = _rms(sgu, gg_ref[:, :SGU_WIDTH]).astype(BF16)
    mem_parts = []
    for bi in range(n_mem_blocks):
        out = jnp.zeros((rows, MEM_WIDTH), F32)
        for hd in range(MEM_HEADS):
            o = mem_pv[bi][hd] / jnp.sum(mem_exp[bi][hd], axis=-1, keepdims=True)
            out = jnp.where(hlane == hd, o, out)
        mem_parts.append(out)
    mem = mem_parts[0] if n_mem_blocks == 1 else jnp.concatenate(mem_parts, axis=0)
    rest_ref[:, SGU_WIDTH:] = _rms(mem, gg_ref[:, SGU_WIDTH:]).astype(BF16)

    v = _dot(h, wqkv_ref[:, 2 * FOX_WIDTH:])

    kb_ref[...] = k.astype(BF16)
    if transposed:
        k_ref[...] = k.T
        vt = v.T
        v_ref[...] = vt
        vtb = vt.astype(BF16)
        tk = vt_ref.shape[-1]
        for c in range(tm // tk):
            vt_ref[c] = vtb[:, c * tk:(c + 1) * tk]
    else:
        k_ref[...] = k
        v_ref[...] = v
        vb_ref[...] = v.astype(BF16)


def _in_proj(x, mem_k, mem_v, mem_first, mem_per_tile, lw, tm, spatial_w, spatial_b, transposed,
             layer, depth, stacked_kv=()):
    nb, t, _ = x.shape
    nt = t // tm
    if transposed:
        assert mem_per_tile == 1
        mem_index = lambda b, i: (mem_first + b, 0, 0)
    else:
        mem_index = lambda b, i: (mem_first // mem_per_tile + b * nt + i, 0, 0)
    row = lambda w: pl.BlockSpec((None, tm, w), lambda b, i: (b, i, 0))
    memspec = pl.BlockSpec((mem_per_tile, MEM_WIDTH, N_MEM), mem_index)
    f32 = lambda w: jax.ShapeDtypeStruct((nb, t, w), F32)
    bf = lambda w: jax.ShapeDtypeStruct((nb, t, w), BF16)
    consts = (lw["g_pre_mix"], lw["w_qkv"], lw["w_b"], lw["b_forget"], spatial_w, spatial_b,
              lw["g_sgu"], lw["g_rest"])
    if transposed:
        tk = ATTN_K_ROWS
        stack_spec = pl.BlockSpec((None, None, FOX_WIDTH, tm), lambda b, i: (layer, b, 0, i))
        stack_shape = jax.ShapeDtypeStruct((depth, nb, FOX_WIDTH, t), F32)
        out_specs = [pl.BlockSpec((None, FOX_WIDTH, tm), lambda b, i: (b, 0, i)),
                     stack_spec, stack_spec, row(FOX_WIDTH),
                     pl.BlockSpec((None, tm // tk, FOX_WIDTH, tk), lambda b, i: (b, i, 0, 0)),
                     row(FOX_HEADS), row(SGU_WIDTH + MEM_WIDTH)]
        out_shape = [jax.ShapeDtypeStruct((nb, FOX_WIDTH, t), BF16), stack_shape, stack_shape,
                     bf(FOX_WIDTH), jax.ShapeDtypeStruct((nb, t // tk, FOX_WIDTH, tk), BF16),
                     f32(FOX_HEADS), bf(SGU_WIDTH + MEM_WIDTH)]
    else:
        out_specs = [row(FOX_WIDTH)] * 5 + [row(FOX_HEADS), row(SGU_WIDTH + MEM_WIDTH), row(SGU_WIDTH)]
        out_shape = [bf(FOX_WIDTH), f32(FOX_WIDTH), f32(FOX_WIDTH), bf(FOX_WIDTH), bf(FOX_WIDTH),
                     f32(FOX_HEADS), bf(SGU_WIDTH + MEM_WIDTH), f32(SGU_WIDTH)]
    in_specs = [row(D_MODEL)] + [_layer_spec(a, layer) for a in consts] + [memspec, memspec]
    aliases = {len(in_specs) + n: 1 + n for n in range(len(stacked_kv))}
    in_specs += [pl.BlockSpec(memory_space=pl.ANY)] * len(stacked_kv)
    return pl.pallas_call(
        functools.partial(_inproj_kernel, transposed=transposed, n_stack_in=len(stacked_kv)),
        grid=(nb, nt),
        in_specs=in_specs,
        out_specs=out_specs,
        out_shape=out_shape,
        input_output_aliases=aliases,
        compiler_params=_params("parallel", "arbitrary"),
        name="in_proj",
    )(x, *consts, mem_k, mem_v, *stacked_kv)


def _cumsum_kernel(x_ref, o_ref, *, minus_total):
    x = x_ref[...]
    n = x.shape[-1]
    lane = lax.broadcasted_iota(jnp.int32, x.shape, 1)
    shift = 1
    while shift < n:
        x = x + jnp.where(lane >= shift, pltpu.roll(x, shift, 1), 0.0)
        shift *= 2
    if minus_total:
        x = x - x[:, n - 1:n]
    o_ref[...] = x


def _lane_cumsum(x, minus_total=False):
    return pl.pallas_call(
        functools.partial(_cumsum_kernel, minus_total=minus_total),
        out_shape=jax.ShapeDtypeStruct(x.shape, F32),
        name="lane_cumsum",
    )(x)


def _mix_out(fox, rest, x, wout_ref, gfox_ref, gpost_ref, gffn_ref, o_ref, h2_ref):
    fox_n = _rms(fox, gfox_ref[...]).astype(BF16)
    y = _dot(fox_n, wout_ref[:FOX_WIDTH, :]) + _dot(rest, wout_ref[FOX_WIDTH:, :])
    x_mid = x + _rms(y, gpost_ref[...])
    o_ref[...] = x_mid
    h2_ref[...] = _rms(x_mid, gffn_ref[...]).astype(BF16)


def _split3(c):
    hi = c.astype(BF16).astype(F32)
    r = c - hi
    mid = r.astype(BF16).astype(F32)
    lo = (r - mid).astype(BF16).astype(F32)
    return hi, mid, lo


_BIAS_ONES = 6


def _fox_prompt_kernel(qt_ref, kb_ref, vt_ref, ccol_ref, crow_ref, rest_ref, x_ref, wout_ref, gfox_ref,
                       gpost_ref, gffn_ref, o_ref, h2_ref, kaug_scr, qa_scr, sa_scr, sb_scr, m_scr, l_scr, acc_scr):
    tq = qt_ref.shape[1]
    tk = vt_ref.shape[-1]
    s_len = kb_ref.shape[0]
    slab = 2 * LANES
    i = pl.program_id(1)

    @pl.when(i == 0)
    def _():
        n_slab = FOX_HEADS // HEAD_PAIR
        src = lax.broadcasted_iota(jnp.int32, (LANES, n_slab * LANES), 0)
        dst = lax.broadcasted_iota(jnp.int32, (LANES, n_slab * LANES), 1)
        head, piece = src % FOX_HEADS, src // FOX_HEADS
        piece_dst = (head // HEAD_PAIR) * LANES + 3 * (head % HEAD_PAIR) + piece
        is_piece = (src < 3 * FOX_HEADS) & (dst == piece_dst)
        is_one = (src == 3 * FOX_HEADS) & (dst % LANES >= _BIAS_ONES) & (dst % LANES < _BIAS_ONES + 3)
        place = jnp.where(is_piece | is_one, 1.0, 0.0).astype(BF16)
        pad = jnp.concatenate([jnp.ones((tk, 1), F32), jnp.zeros((tk, LANES - 3 * FOX_HEADS - 1), F32)], axis=1)
        for r in range(s_len // tk):
            rows = slice(r * tk, (r + 1) * tk)
            pieces = jnp.concatenate(_split3(ccol_ref[rows, :]) + (pad,), axis=1)
            bias = _dot(pieces.astype(BF16), place)
            for p in range(n_slab):
                kaug_scr[rows, p * slab:p * slab + LANES] = kb_ref[rows, p * LANES:(p + 1) * LANES]
                kaug_scr[rows, p * slab + LANES:(p + 1) * slab] = bias[:, p * LANES:(p + 1) * LANES].astype(BF16)

    rowi = lax.broadcasted_iota(jnp.int32, (LANES, tq), 0)
    diag = lax.broadcasted_iota(jnp.int32, (tk, tq), 0) <= lax.broadcasted_iota(jnp.int32, (tk, tq), 1)
    crow = crow_ref[...]
    zero_q = jnp.zeros((LANES, tq), BF16)
    for p in range(FOX_HEADS // HEAD_PAIR):
        qtp = qt_ref[p * LANES:(p + 1) * LANES, :]
        for hh in range(HEAD_PAIR):
            own = (rowi >= hh * HEAD_DIM) & (rowi < (hh + 1) * HEAD_DIM)
            chi, cmid, clo = _split3(crow[2 * p + hh:2 * p + hh + 1, :])
            br = jnp.where(rowi == _BIAS_ONES, chi,
                           jnp.where(rowi == _BIAS_ONES + 1, cmid, jnp.where(rowi == _BIAS_ONES + 2, clo, 0.0)))
            br = jnp.where((rowi >= 3 * hh) & (rowi < 3 * hh + 3), -1.0, br)
            qa_scr[2 * p + hh, :LANES, :] = jnp.where(own, qtp, zero_q)
            qa_scr[2 * p + hh, LANES:, :] = br.astype(BF16)

    m_scr[...] = jnp.full(m_scr.shape, NEG_INF, F32)
    l_scr[...] = jnp.zeros(l_scr.shape, F32)
    acc_scr[...] = jnp.zeros(acc_scr.shape, F32)

    def scores_to(buf_ref, j):
        off = pl.multiple_of(j * tk, tk)
        for h in range(FOX_HEADS):
            ka = kaug_scr[pl.ds(off, tk), (h // HEAD_PAIR) * slab:(h // HEAD_PAIR + 1) * slab]
            buf_ref[h] = _dot(ka, qa_scr[h])

    def absorb_from(buf_ref, j, mask):
        vt = vt_ref[j]
        for h in range(FOX_HEADS):
            feat = slice(h * HEAD_DIM, (h + 1) * HEAD_DIM)
            s = buf_ref[h]
            if mask is not None:
                s = jnp.where(mask, s, NEG_INF)
            m = m_scr[h:h + 1, :]
            m_new = jnp.maximum(m, jnp.max(s, axis=0, keepdims=True))
            alpha = jnp.exp(m - m_new)
            e = jnp.exp(s - m_new)
            m_scr[h:h + 1, :] = m_new
            l_scr[h:h + 1, :] = alpha * l_scr[h:h + 1, :] + jnp.sum(e, axis=0, keepdims=True)
            acc_scr[feat, :] = alpha * acc_scr[feat, :] + _dot(vt[feat, :], e.astype(BF16))

    odd = i % 2

    @pl.when(odd == 1)
    def _():
        scores_to(sb_scr, 0)
        scores_to(sa_scr, 1)
        absorb_from(sb_scr, 0, None)

    @pl.when(odd == 0)
    def _():
        scores_to(sa_scr, 0)

    def step(jj, _):
        t = odd + 2 * jj
        scores_to(sb_scr, t + 1)
        absorb_from(sa_scr, t, None)
        scores_to(sa_scr, t + 2)
        absorb_from(sb_scr, t + 1, None)
        return 0

    lax.fori_loop(0, i // 2, step, 0)
    absorb_from(sa_scr, i, diag)
    fox_t = jnp.concatenate([acc_scr[h * HEAD_DIM:(h + 1) * HEAD_DIM, :] / l_scr[h:h + 1, :]
                             for h in range(FOX_HEADS)], axis=0)
    fox = fox_t.T
    _mix_out(fox, rest_ref[...], x_ref[...], wout_ref, gfox_ref, gpost_ref, gffn_ref, o_ref, h2_ref)


def _mix_weights(lw):
    return (lw["w_out"], lw["g_fox"], lw["g_post_mix"], lw["g_pre_ffn"])


def _fox_prompt(qt, kb, vt, c_col, c_row, rest, x, lw, layer):
    nb, s, _ = x.shape
    tq, tk = ATTN_Q_ROWS, ATTN_K_ROWS
    assert tq == tk and vt.shape[-1] == tk
    nq = s // tq
    qrow = lambda w: pl.BlockSpec((None, tq, w), lambda b, i: (b, i, 0))
    qcol = lambda r: pl.BlockSpec((None, r, tq), lambda b, i: (b, 0, i))
    full = lambda a: pl.BlockSpec((None,) + a.shape[1:], lambda b, i: (b,) + (0,) * (a.ndim - 1))
    return pl.pallas_call(
        _fox_prompt_kernel,
        grid=(nb, nq),
        in_specs=[qcol(FOX_WIDTH), full(kb), full(vt), full(c_col), qcol(FOX_HEADS),
                  qrow(SGU_WIDTH + MEM_WIDTH), qrow(D_MODEL)] + [_layer_spec(a, layer) for a in _mix_weights(lw)],
        out_specs=[qrow(D_MODEL), qrow(D_MODEL)],
        out_shape=[jax.ShapeDtypeStruct(x.shape, F32), jax.ShapeDtypeStruct(x.shape, BF16)],
        scratch_shapes=[pltpu.VMEM((s, 2 * FOX_WIDTH), BF16),
                        pltpu.VMEM((FOX_HEADS, 2 * LANES, tq), BF16),
                        pltpu.VMEM((FOX_HEADS, tk, tq), F32), pltpu.VMEM((FOX_HEADS, tk, tq), F32),
                        pltpu.VMEM((FOX_HEADS, tq), F32), pltpu.VMEM((FOX_HEADS, tq), F32),
                        pltpu.VMEM((FOX_WIDTH, tq), F32)],
        compiler_params=_params("parallel", "arbitrary"),
        name="fox_mix_prompt",
    )(qt, kb, vt, c_col, c_row, rest, x, *_mix_weights(lw))


def _fox_sample_kernel(q_ref, kn_ref, vn_ref, hk_ref, hv_ref, hrow_ref, ncol_ref, nrow_ref, rest_ref, x_ref,
                       wout_ref, gfox_ref, gpost_ref, gffn_ref, o_ref, h2_ref, fox_scr):
    t = q_ref.shape[0]
    b = pl.program_id(0)
    rows = FOX_HEADS * t
    qt = jnp.concatenate([q_ref[...]] * FOX_HEADS, axis=0)
    row_head = lax.broadcasted_iota(jnp.int32, (rows, FOX_WIDTH), 0) // t
    lane_head = lax.broadcasted_iota(jnp.int32, (rows, FOX_WIDTH), 1) // HEAD_DIM
    own = row_head == lane_head
    qb = jnp.where(own, qt, jnp.zeros_like(qt))
    per_head = lambda a: jnp.concatenate(
        [jnp.broadcast_to(a[h:h + 1, :], (t, a.shape[1])) for h in range(FOX_HEADS)], axis=0)
    bq = ncol_ref[...]
    s_hist = _dot(qb, hk_ref[...].astype(BF16)) + bq - per_head(hrow_ref[...])
    s_new = _dot_nt(qb, kn_ref[...]) + bq - per_head(nrow_ref[:, :t])
    causal = (lax.broadcasted_iota(jnp.int32, (rows, t), 1)
              <= lax.broadcasted_iota(jnp.int32, (rows, t), 0) % t)
    s_new = jnp.where(causal, s_new, NEG_INF)
    m = jnp.maximum(jnp.max(s_hist, axis=-1, keepdims=True), jnp.max(s_new, axis=-1, keepdims=True))
    e_hist = jnp.exp(s_hist - m)
    e_new = jnp.exp(s_new - m)
    l = jnp.sum(e_hist, axis=-1, keepdims=True) + jnp.sum(e_new, axis=-1, keepdims=True)
    o = (_dot_nt(e_hist.astype(BF16), hv_ref[...].astype(BF16)) + _dot(e_new.astype(BF16), vn_ref[...])) / l
    o = jnp.where(own, o, 0.0)
    fox = o[0:t, :]
    for h in range(1, FOX_HEADS):
        fox = fox + o[h * t:(h + 1) * t, :]
    fox_scr[pl.ds(pl.multiple_of(b * t, t), t), :] = fox

    @pl.when(b == pl.num_programs(0) - 1)
    def _():
        _mix_out(fox_scr[...], rest_ref[...], x_ref[...], wout_ref, gfox_ref, gpost_ref, gffn_ref, o_ref, h2_ref)


def _fox_sample(q, kb, vb, hk, hv, layer, hist_row, new_col, new_row, rest, x, lw):
    nb, t, _ = q.shape
    per_b = lambda a: pl.BlockSpec((None,) + a.shape[1:], lambda b: (b,) + (0,) * (a.ndim - 1))
    cache = pl.BlockSpec((None, None) + hk.shape[2:], lambda b: (layer, b, 0, 0))
    const = lambda a: pl.BlockSpec(a.shape, lambda b: (0,) * a.ndim)
    return pl.pallas_call(
        _fox_sample_kernel,
        grid=(nb,),
        in_specs=[per_b(q), per_b(kb), per_b(vb), cache, cache, per_b(hist_row), per_b(new_col),
                  per_b(new_row), const(rest), const(x)] + [_layer_spec(a, layer) for a in _mix_weights(lw)],
        out_specs=[const(x), const(x)],
        out_shape=[jax.ShapeDtypeStruct(x.shape, F32), jax.ShapeDtypeStruct(x.shape, BF16)],
        scratch_shapes=[pltpu.VMEM((nb * t, FOX_WIDTH), F32)],
        compiler_params=_params("arbitrary"),
        name="fox_mix_sample",
    )(q, kb, vb, hk, hv, hist_row, new_col, new_row, rest, x, *_mix_weights(lw))


def _ffn_body(x_ref, h2_ref, wup_ref, wdw_ref, bdw_ref, wd_ref, gpost_ref, o_ref, g_scr, shifted, emit):
    h2 = h2_ref[...]
    nf = FFN_DIM // FFN_COLS

    def up(c):
        conv_cols = slice(c * FFN_COLS, (c + 1) * FFN_COLS)
        lin_cols = slice(FFN_DIM + c * FFN_COLS, FFN_DIM + (c + 1) * FFN_COLS)
        return _dot(h2, wup_ref[:, conv_cols]), _dot(h2, wup_ref[:, lin_cols])

    nxt = up(0)
    for c in range(nf):
        a, lin = nxt
        if c + 1 < nf:
            nxt = up(c + 1)
        cols = slice(c * FFN_COLS, (c + 1) * FFN_COLS)
        a1, a2 = shifted(a, c)
        conv = bdw_ref[:, cols] + wdw_ref[0:1, cols] * a2
        conv = conv + wdw_ref[1:2, cols] * a1
        conv = conv + wdw_ref[2:3, cols] * a
        g_scr[:, cols] = (jax.nn.silu(conv) * lin).astype(BF16)
        emit(a, c)
    o_ref[...] = x_ref[...] + _rms(_dot(g_scr[...], wd_ref[...]), gpost_ref[...])


def _ffn_prompt_kernel(x_ref, h2_ref, wup_ref, wdw_ref, bdw_ref, wd_ref, gpost_ref, hist_ref,
                       o_ref, tail_ref, g_scr, carry_scr, work_scr):
    tm = x_ref.shape[0]
    head = SUBLANES

    @pl.when(pl.program_id(1) == 0)
    def _():
        carry_scr[0:head - (CONV_WIDTH - 1), :] = jnp.zeros((head - (CONV_WIDTH - 1), FFN_DIM), F32)
        carry_scr[head - (CONV_WIDTH - 1):head, :] = hist_ref[...]

    def shifted(a, c):
        cols = slice(c * FFN_COLS, (c + 1) * FFN_COLS)
        work = work_scr.at[c % 2]
        work[0:head, :] = carry_scr[:, cols]
        work[head:head + tm, :] = a
        return work[head - 1:head - 1 + tm, :], work[head - 2:head - 2 + tm, :]

    def emit(a, c):
        cols = slice(c * FFN_COLS, (c + 1) * FFN_COLS)
        carry_scr[:, cols] = a[tm - head:, :]
        tail_ref[:, cols] = a[tm - head:, :]

    _ffn_body(x_ref, h2_ref, wup_ref, wdw_ref, bdw_ref, wd_ref, gpost_ref, o_ref, g_scr, shifted, emit)


def _ffn_sample_kernel(x_ref, h2_ref, wup_ref, wdw_ref, bdw_ref, wd_ref, gpost_ref, e1_ref, e2_ref,
                       o_ref, a_ref, g_scr, work_scr, *, seg):
    tm = x_ref.shape[0]
    head = SUBLANES
    rmod = lax.broadcasted_iota(jnp.int32, (tm, FFN_COLS), 0) % seg

    def shifted(a, c):
        cols = slice(c * FFN_COLS, (c + 1) * FFN_COLS)
        work = work_scr.at[c % 2]
        work[0:head, :] = jnp.zeros((head, FFN_COLS), F32)
        work[head:head + tm, :] = a
        a1 = jnp.where(rmod >= 1, work[head - 1:head - 1 + tm, :], e1_ref[:, cols])
        a2 = jnp.where(rmod >= 2, work[head - 2:head - 2 + tm, :], e2_ref[:, cols])
        return a1, a2

    def emit(a, c):
        a_ref[:, c * FFN_COLS:(c + 1) * FFN_COLS] = a

    _ffn_body(x_ref, h2_ref, wup_ref, wdw_ref, bdw_ref, wd_ref, gpost_ref, o_ref, g_scr, shifted, emit)


def _ffn_weights(lw):
    return (lw["w_up"], lw["w_dwconv"], lw["b_dwconv"], lw["w_down"], lw["g_post_ffn"])


def _ffn_weight_specs(lw, layer):
    return [_layer_spec(a, layer, pipeline_mode=pl.Buffered(1)) for a in _ffn_weights(lw)]


def _ffn_prompt(x, h2, hist, lw, layer):
    nb, s, _ = x.shape
    tm = FFN_ROWS
    nt = s // tm
    row = pl.BlockSpec((None, tm, D_MODEL), lambda b, i: (b, i, 0))
    return pl.pallas_call(
        _ffn_prompt_kernel,
        grid=(nb, nt),
        in_specs=[row, row] + _ffn_weight_specs(lw, layer) + [
            pl.BlockSpec((None, CONV_WIDTH - 1, FFN_DIM), lambda b, i: (b, 0, 0))],
        out_specs=[pl.BlockSpec((None, tm, D_MODEL), lambda b, i: (b, i, 0)),
                   pl.BlockSpec((None, None, SUBLANES, FFN_DIM), lambda b, i: (b, i, 0, 0))],
        out_shape=[jax.ShapeDtypeStruct(x.shape, F32),
                   jax.ShapeDtypeStruct((nb, nt, SUBLANES, FFN_DIM), F32)],
        scratch_shapes=[pltpu.VMEM((tm, FFN_DIM), BF16),
                        pltpu.VMEM((SUBLANES, FFN_DIM), F32),
                        pltpu.VMEM((2, tm + SUBLANES, FFN_COLS), F32)],
        compiler_params=_params("parallel", "arbitrary"),
        name="conv_ffn_prompt",
    )(x, h2, *_ffn_weights(lw), hist)


def _ffn_sample(x, h2, e1, e2, lw, layer, seg):
    rows, _ = x.shape
    whole = lambda w: pl.BlockSpec((rows, w), lambda i: (0, 0))
    return pl.pallas_call(
        functools.partial(_ffn_sample_kernel, seg=seg),
        grid=(1,),
        in_specs=[whole(D_MODEL), whole(D_MODEL)] + _ffn_weight_specs(lw, layer) + [whole(FFN_DIM), whole(FFN_DIM)],
        out_specs=[whole(D_MODEL), whole(FFN_DIM)],
        out_shape=[jax.ShapeDtypeStruct(x.shape, F32), jax.ShapeDtypeStruct((rows, FFN_DIM), F32)],
        scratch_shapes=[pltpu.VMEM((rows, FFN_DIM), BF16),
                        pltpu.VMEM((2, rows + SUBLANES, FFN_COLS), F32)],
        compiler_params=_params("arbitrary"),
        name="conv_ffn_sample",
    )(x, h2, *_ffn_weights(lw), e1, e2)


def _prepare_weights(g_pre_mix, w_in, b_forget, w_spatial, b_spatial, g_sgu, g_group_out, w_out, g_post_mix,
                     g_pre_ffn, w_up, w_dwconv, b_dwconv, w_down, g_post_ffn, dec_seq):
    depth = w_in.shape[0]
    row = lambda a: a.reshape(depth, 1, -1)
    c0 = 3 * FOX_WIDTH
    c1 = c0 + FOX_HEADS
    c2 = c1 + 2 * SGU_WIDTH
    w_b = jnp.concatenate([w_in[:, :, c1:c2], w_in[:, :, c2:], w_in[:, :, c0:c1],
                           jnp.zeros((depth, D_MODEL, B_COLS - (2 * SGU_WIDTH + MEM_WIDTH + FOX_HEADS)), F32)],
                          axis=2)
    group_dim = SGU_WIDTH // SGU_GROUPS
    ws_prompt = jnp.concatenate([w_spatial[:, g] for g in range(SGU_GROUPS)], axis=2)
    bt_prompt = jnp.repeat(jnp.swapaxes(b_spatial, 1, 2), group_dim, axis=2)
    reps = GMLP_CHUNK // dec_seq
    blk = (jnp.arange(GMLP_CHUNK)[:, None] // dec_seq) == (jnp.arange(GMLP_CHUNK)[None, :] // dec_seq)
    ws_sample = jnp.concatenate(
        [jnp.where(blk, jnp.tile(w_spatial[:, g, :dec_seq, :dec_seq], (1, reps, reps)), 0.0)
         for g in range(SGU_GROUPS)], axis=2)
    bt_sample = jnp.tile(jnp.repeat(jnp.swapaxes(b_spatial[:, :, :dec_seq], 1, 2), group_dim, axis=2),
                         (1, reps, 1))
    return {
        "g_pre_mix": row(g_pre_mix), "w_qkv": w_in[:, :, :c0].astype(BF16), "w_b": w_b.astype(BF16),
        "b_forget": row(b_forget), "g_sgu": row(g_sgu),
        "g_fox": row(g_group_out[:, :FOX_WIDTH]), "g_rest": row(g_group_out[:, FOX_WIDTH:]),
        "w_out": w_out.astype(BF16), "g_post_mix": row(g_post_mix), "g_pre_ffn": row(g_pre_ffn),
        "w_up": w_up.astype(BF16), "w_dwconv": w_dwconv, "b_dwconv": row(b_dwconv),
        "w_down": w_down.astype(BF16), "g_post_ffn": row(g_post_ffn),
        "ws_prompt": ws_prompt, "bt_prompt": bt_prompt, "ws_sample": ws_sample, "bt_sample": bt_sample,
    }


def kernel(x_prompt, x_sample, mem_prompt, cache_fox_k, cache_fox_v, cache_fox_logf, cache_mem_k, cache_mem_v,
           cache_ffn_conv, g_pre_mix, w_in, b_forget, w_spatial, b_spatial, g_sgu, g_mem, w_mem_kv, g_group_out,
           w_out, g_post_mix, g_pre_ffn, w_up, w_dwconv, b_dwconv, w_down, g_post_ffn):
    depth = w_in.shape[0]
    batch, seq, _ = x_prompt.shape
    dec_batch, dec_seq, _ = x_sample.shape
    past = cache_fox_k.shape[2]
    dec_rows = dec_batch * dec_seq

    mem_k_all, mem_v_all = _memory_kv(mem_prompt, g_mem, w_mem_kv.astype(BF16))
    feature_major = lambda a: jnp.transpose(a, (0, 1, 3, 4, 2)).reshape(a.shape[0], a.shape[1], -1, a.shape[2])
    token_major = lambda a, heads: jnp.transpose(
        a.reshape(a.shape[0], a.shape[1], heads, HEAD_DIM, a.shape[3]), (0, 1, 4, 2, 3))
    flat_mem = lambda a: a.reshape(-1, MEM_WIDTH, N_MEM)
    pmk, pmv = flat_mem(mem_k_all), flat_mem(mem_v_all)
    smk, smv = flat_mem(feature_major(cache_mem_k)), flat_mem(feature_major(cache_mem_v))
    hk = feature_major(cache_fox_k)
    hv = feature_major(cache_fox_v)

    lw = _prepare_weights(g_pre_mix, w_in, b_forget, w_spatial, b_spatial, g_sgu, g_group_out, w_out, g_post_mix,
                          g_pre_ffn, w_up, w_dwconv, b_dwconv, w_down, g_post_ffn, dec_seq)
    hist_rows = _lane_cumsum(jnp.swapaxes(cache_fox_logf, 2, 3).reshape(depth * dec_batch * FOX_HEADS, past),
                             minus_total=True).reshape(depth, dec_batch, FOX_HEADS, past)
    pad_rows = lambda a: jnp.pad(a, ((0, 0), (0, 0), (0, dec_seq - a.shape[2]), (0, 0))
                                 ).reshape(depth, dec_rows, FFN_DIM)
    e1_all = pad_rows(cache_ffn_conv[:, :, 1:2, :])
    e2_all = pad_rows(cache_ffn_conv)
    zeros_hist = jnp.zeros((batch, CONV_WIDTH - 1, FFN_DIM), F32)

    yp = x_prompt
    ys = x_sample.reshape(1, dec_rows, D_MODEL)
    outs = {name: [] for name in ("logf_p", "conv_p", "k_s", "v_s", "logf_s", "gv_s", "conv_s")}
    stacked_kv = ()
    per_b = lambda a: a.reshape(dec_batch, dec_seq, a.shape[-1])
    for l in range(depth):
        qt, k_all, v_all, kb, vt, logf, rest = _in_proj(yp, pmk, pmv, l * batch, 1, lw, IN_PROJ_ROWS,
                                                        lw["ws_prompt"], lw["bt_prompt"], True,
                                                        layer=l, depth=depth, stacked_kv=stacked_kv)
        stacked_kv = (k_all, v_all)
        logf_row = jnp.swapaxes(logf, 1, 2).reshape(batch * FOX_HEADS, seq)
        c_row = _lane_cumsum(logf_row).reshape(batch, FOX_HEADS, seq)
        c_col = jnp.swapaxes(c_row, 1, 2)
        yp, h2 = _fox_prompt(qt, kb, vt, c_col, c_row, rest, yp, lw, l)
        yp, tail = _ffn_prompt(yp, h2, zeros_hist, lw, l)
        outs["logf_p"].append(logf)
        outs["conv_p"].append(tail[:, -1, SUBLANES - (CONV_WIDTH - 1):, :])

        q, k, v, kb, vb, logf, rest, vrows = _in_proj(ys, smk, smv, l * dec_batch, dec_batch, lw, dec_rows,
                                                      lw["ws_sample"], lw["bt_sample"], False,
                                                      layer=l, depth=depth)
        logf_new_row = jnp.swapaxes(per_b(logf), 1, 2).reshape(dec_batch * FOX_HEADS, dec_seq)
        new_row = _lane_cumsum(jnp.pad(logf_new_row, ((0, 0), (0, LANES - dec_seq)))
                               ).reshape(dec_batch, FOX_HEADS, LANES)
        new_col = new_row[:, :, :dec_seq].reshape(dec_batch, FOX_HEADS * dec_seq, 1)
        ys2, h2s = _fox_sample(per_b(q[0]), per_b(kb[0]), per_b(vb[0]), hk, hv, l, hist_rows[l], new_col, new_row,
                               rest[0], ys[0], lw)
        ys_flat, a_all = _ffn_sample(ys2, h2s, e1_all[l], e2_all[l], lw, l, dec_seq)
        ys = ys_flat.reshape(1, dec_rows, D_MODEL)
        outs["k_s"].append(per_b(k[0]).reshape(dec_batch, dec_seq, FOX_HEADS, HEAD_DIM))
        outs["v_s"].append(per_b(v[0]).reshape(dec_batch, dec_seq, FOX_HEADS, HEAD_DIM))
        outs["logf_s"].append(per_b(logf[0]))
        outs["gv_s"].append(per_b(vrows[0]))
        outs["conv_s"].append(a_all.reshape(dec_batch, dec_seq, FFN_DIM)[:, dec_seq - (CONV_WIDTH - 1):, :])

    st = {name: jnp.stack(vals) for name, vals in outs.items()}
    return (yp, ys.reshape(dec_batch, dec_seq, D_MODEL),
            token_major(stacked_kv[0], FOX_HEADS), token_major(stacked_kv[1], FOX_HEADS), st["logf_p"],
            token_major(mem_k_all, MEM_HEADS), token_major(mem_v_all, MEM_HEADS),
            st["conv_p"], st["k_s"], st["v_s"], st["logf_s"], st["gv_s"], st["conv_s"])
```

```python
import functools

import jax
import jax.numpy as jnp
from jax import lax
from jax.experimental import pallas as pl
from jax.experimental.pallas import tpu as pltpu

D_MODEL = 1024
HEAD_DIM = 64
FOX_WIDTH = 512
FOX_HEADS = 8
SGU_WIDTH = 256
SGU_GROUPS = 4
GMLP_CHUNK = 128
CHUNK = 64
MEM_WIDTH = 256
MEM_HEADS = 4
N_MEM = 256
FFN_DIM = 2816
CONV_WIDTH = 3
RMS_EPS = 1e-6
NEG_INF = -1e30
QK_SCALE = HEAD_DIM ** -0.5
LOG2E = 1.4426950408889634

LANES = 128
SUBLANES = 8
BF16_SUBLANES = 2 * SUBLANES
HEAD_PAIR = LANES // HEAD_DIM
VMEM_LIMIT_BYTES = 56 * 1024 * 1024

IN_PROJ_ROWS = 512
ATTN_Q_ROWS = 256
ATTN_K_ROWS = 256
FFN_ROWS = 512
FFN_COLS = 256
B_COLS = 896

BF16 = jnp.bfloat16
F32 = jnp.float32


def _rms(x, g):
    y = x * lax.rsqrt(jnp.mean(x * x, axis=-1, keepdims=True) + RMS_EPS)
    return y * g


def _dot(a, b):
    return jnp.dot(a, b, preferred_element_type=F32)


def _dot_nt(a, b):
    return lax.dot_general(a, b, (((1,), (1,)), ((), ())), preferred_element_type=F32)


def _params(*semantics):
    return pltpu.CompilerParams(dimension_semantics=semantics, vmem_limit_bytes=VMEM_LIMIT_BYTES)


def _layer_spec(a, layer, **kwargs):
    return pl.BlockSpec((None,) + a.shape[1:], lambda *_: (layer,) + (0,) * (a.ndim - 1), **kwargs)


def _memkv_kernel(mem_ref, g_ref, w_ref, mk_ref, mv_ref):
    h = _rms(mem_ref[...], g_ref[...]).astype(BF16)
    kv = _dot(h, w_ref[...])
    mk_ref[...] = kv[:, :MEM_WIDTH].T
    mv_ref[...] = kv[:, MEM_WIDTH:].T


def _memory_kv(mem, g_mem, w_mem_kv_bf):
    depth = g_mem.shape[0]
    batch = mem.shape[0]
    out = jax.ShapeDtypeStruct((depth, batch, MEM_WIDTH, N_MEM), F32)
    return pl.pallas_call(
        _memkv_kernel,
        grid=(depth, batch),
        in_specs=[
            pl.BlockSpec((None, N_MEM, D_MODEL), lambda l, b: (b, 0, 0)),
            pl.BlockSpec((None, 1, D_MODEL), lambda l, b: (l, 0, 0)),
            pl.BlockSpec((None, D_MODEL, 2 * MEM_WIDTH), lambda l, b: (l, 0, 0)),
        ],
        out_specs=[
            pl.BlockSpec((None, None, MEM_WIDTH, N_MEM), lambda l, b: (l, b, 0, 0)),
            pl.BlockSpec((None, None, MEM_WIDTH, N_MEM), lambda l, b: (l, b, 0, 0)),
        ],
        out_shape=[out, out],
        compiler_params=_params("arbitrary", "arbitrary"),
        name="memory_kv",
    )(mem, g_mem.reshape(depth, 1, D_MODEL), w_mem_kv_bf)


def _inproj_kernel(x_ref, gpre_ref, wqkv_ref, wb_ref, bfg_ref, ws_ref, bt_ref, gsgu_ref, gg_ref,
                   mk_ref, mv_ref, *out_refs, transposed, n_stack_in):
    out_refs = out_refs[n_stack_in:]
    tm = x_ref.shape[0]
    n_mem_blocks = mk_ref.shape[0]
    if transposed:
        qt_ref, k_ref, v_ref, kb_ref, vt_ref, lf_ref, rest_ref = out_refs
        vrows_ref = None
    else:
        q_ref, k_ref, v_ref, kb_ref, vb_ref, lf_ref, rest_ref, vrows_ref = out_refs
    h = _rms(x_ref[...], gpre_ref[...]).astype(BF16)

    yb = _dot(h, wb_ref[...])
    fg = yb[:, 2 * SGU_WIDTH + MEM_WIDTH:2 * SGU_WIDTH + MEM_WIDTH + LANES]
    lf = jax.nn.log_sigmoid(fg + bfg_ref[...])
    if transposed:
        lf_ref[...] = lf.T[:FOX_HEADS, :]
    else:
        lf_ref[...] = lf[:, :FOX_HEADS]

    qmb = (yb[:, 2 * SGU_WIDTH:2 * SGU_WIDTH + MEM_WIDTH] * QK_SCALE).astype(BF16)
    rows = tm // n_mem_blocks
    hlane = lax.broadcasted_iota(jnp.int32, (rows, MEM_WIDTH), 1) // HEAD_DIM
    zero_q = jnp.zeros((rows, MEM_WIDTH), BF16)
    mem_scores = []
    for bi in range(n_mem_blocks):
        qb = qmb[bi * rows:(bi + 1) * rows, :]
        mkt = mk_ref[bi].astype(BF16)
        mem_scores.append([_dot(jnp.where(hlane == hd, qb, zero_q), mkt) for hd in range(MEM_HEADS)])

    q = _dot(h, wqkv_ref[:, :FOX_WIDTH]) * (QK_SCALE * LOG2E if transposed else QK_SCALE)

    z = jax.nn.gelu(yb[:, :2 * SGU_WIDTH])
    u = z[:, :SGU_WIDTH]
    vv = _rms(z[:, SGU_WIDTH:], gsgu_ref[...])
    if vrows_ref is not None:
        vrows_ref[...] = vv
    vvb = vv.astype(BF16)
    mem_exp = [[jnp.exp(s - jnp.max(s, axis=-1, keepdims=True)) for s in per_block] for per_block in mem_scores]

    wrow = lax.broadcasted_iota(jnp.int32, (GMLP_CHUNK, SGU_GROUPS * GMLP_CHUNK), 0)
    wcol = lax.broadcasted_iota(jnp.int32, (GMLP_CHUNK, SGU_GROUPS * GMLP_CHUNK), 1)
    wmask = ((wcol % GMLP_CHUNK) // CHUNK) <= (wrow // CHUNK)
    wcat = jnp.where(wmask, ws_ref[...], 0.0).astype(BF16)
    glane = lax.broadcasted_iota(jnp.int32, (GMLP_CHUNK, SGU_WIDTH), 1) // (SGU_WIDTH // SGU_GROUPS)
    zero_chunk = jnp.zeros((GMLP_CHUNK, SGU_WIDTH), BF16)
    mixed_parts = []
    for c in range(tm // GMLP_CHUNK):
        vc = vvb[c * GMLP_CHUNK:(c + 1) * GMLP_CHUNK, :]
        rhs = jnp.concatenate([jnp.where(glane == g, vc, zero_chunk) for g in range(SGU_GROUPS)], axis=0)
        mixed_parts.append(_dot(wcat, rhs))
    mem_pv = []
    for bi in range(n_mem_blocks):
        mvt = mv_ref[bi].astype(BF16)
        mem_pv.append([_dot_nt(e.astype(BF16), mvt) for e in mem_exp[bi]])

    k = _dot(h, wqkv_ref[:, FOX_WIDTH:2 * FOX_WIDTH])

    if transposed:
        qt_ref[...] = q.T.astype(BF16)
    else:
        q_ref[...] = q.astype(BF16)
    sgu = jnp.concatenate([u[c * GMLP_CHUNK:(c + 1) * GMLP_CHUNK, :] * (mixed + bt_ref[...])
                           for c, mixed in enumerate(mixed_parts)], axis=0)
    rest_ref[:, :SGU_WIDTH] = _rms(sgu, gg_ref[:, :SGU_WIDTH]).astype(BF16)
    mem_parts = []
    for bi in range(n_mem_blocks):
        out = jnp.zeros((rows, MEM_WIDTH), F32)
        for hd in range(MEM_HEADS):
            o = mem_pv[bi][hd] / jnp.sum(mem_exp[bi][hd], axis=-1, keepdims=True)
            out = jnp.where(hlane == hd, o, out)
        mem_parts.append(out)
    mem = mem_parts[0] if n_mem_blocks == 1 else jnp.concatenate(mem_parts, axis=0)
    rest_ref[:, SGU_WIDTH:] = _rms(mem, gg_ref[:, SGU_WIDTH:]).astype(BF16)

    v = _dot(h, wqkv_ref[:, 2 * FOX_WIDTH:])

    kb_ref[...] = k.astype(BF16)
    if transposed:
        k_ref[...] = k.T
        vt = v.T
        v_ref[...] = vt
        vtb = vt.astype(BF16)
        tk = vt_ref.shape[-1]
        for c in range(tm // tk):
            vt_ref[c] = vtb[:, c * tk:(c + 1) * tk]
    else:
        k_ref[...] = k
        v_ref[...] = v
        vb_ref[...] = v.astype(BF16)


def _in_proj(x, mem_k, mem_v, mem_first, mem_per_tile, lw, tm, spatial_w, spatial_b, transposed,
             layer, depth, stacked_kv=()):
    nb, t, _ = x.shape
    nt = t // tm
    if transposed:
        assert mem_per_tile == 1
        mem_index = lambda b, i: (mem_first + b, 0, 0)
    else:
        mem_index = lambda b, i: (mem_first // mem_per_tile + b * nt + i, 0, 0)
    row = lambda w: pl.BlockSpec((None, tm, w), lambda b, i: (b, i, 0))
    memspec = pl.BlockSpec((mem_per_tile, MEM_WIDTH, N_MEM), mem_index)
    f32 = lambda w: jax.ShapeDtypeStruct((nb, t, w), F32)
    bf = lambda w: jax.ShapeDtypeStruct((nb, t, w), BF16)
    consts = (lw["g_pre_mix"], lw["w_qkv"], lw["w_b"], lw["b_forget"], spatial_w, spatial_b,
              lw["g_sgu"], lw["g_rest"])
    if transposed:
        tk = ATTN_K_ROWS
        stack_spec = pl.BlockSpec((None, None, FOX_WIDTH, tm), lambda b, i: (layer, b, 0, i))
        stack_shape = jax.ShapeDtypeStruct((depth, nb, FOX_WIDTH, t), F32)
        out_specs = [pl.BlockSpec((None, FOX_WIDTH, tm), lambda b, i: (b, 0, i)),
                     stack_spec, stack_spec, row(FOX_WIDTH),
                     pl.BlockSpec((None, tm // tk, FOX_WIDTH, tk), lambda b, i: (b, i, 0, 0)),
                     pl.BlockSpec((None, FOX_HEADS, tm), lambda b, i: (b, 0, i)), row(SGU_WIDTH + MEM_WIDTH)]
        out_shape = [jax.ShapeDtypeStruct((nb, FOX_WIDTH, t), BF16), stack_shape, stack_shape,
                     bf(FOX_WIDTH), jax.ShapeDtypeStruct((nb, t // tk, FOX_WIDTH, tk), BF16),
                     jax.ShapeDtypeStruct((nb, FOX_HEADS, t), F32), bf(SGU_WIDTH + MEM_WIDTH)]
    else:
        out_specs = [row(FOX_WIDTH)] * 5 + [row(FOX_HEADS), row(SGU_WIDTH + MEM_WIDTH), row(SGU_WIDTH)]
        out_shape = [bf(FOX_WIDTH), f32(FOX_WIDTH), f32(FOX_WIDTH), bf(FOX_WIDTH), bf(FOX_WIDTH),
                     f32(FOX_HEADS), bf(SGU_WIDTH + MEM_WIDTH), f32(SGU_WIDTH)]
    in_specs = [row(D_MODEL)] + [_layer_spec(a, layer) for a in consts] + [memspec, memspec]
    aliases = {len(in_specs) + n: 1 + n for n in range(len(stacked_kv))}
    in_specs += [pl.BlockSpec(memory_space=pl.ANY)] * len(stacked_kv)
    return pl.pallas_call(
        functools.partial(_inproj_kernel, transposed=transposed, n_stack_in=len(stacked_kv)),
        grid=(nb, nt),
        in_specs=in_specs,
        out_specs=out_specs,
        out_shape=out_shape,
        input_output_aliases=aliases,
        compiler_params=_params("parallel", "arbitrary"),
        name="in_proj",
    )(x, *consts, mem_k, mem_v, *stacked_kv)


def _cumsum_kernel(x_ref, o_ref, *, minus_total):
    x = x_ref[...]
    n = x.shape[-1]
    lane = lax.broadcasted_iota(jnp.int32, x.shape, 1)
    shift = 1
    while shift < n:
        x = x + jnp.where(lane >= shift, pltpu.roll(x, shift, 1), 0.0)
        shift *= 2
    if minus_total:
        x = x - x[:, n - 1:n]
    o_ref[...] = x


def _lane_cumsum(x, minus_total=False):
    return pl.pallas_call(
        functools.partial(_cumsum_kernel, minus_total=minus_total),
        out_shape=jax.ShapeDtypeStruct(x.shape, F32),
        name="lane_cumsum",
    )(x)


def _mix_out(fox, rest, x, wout_ref, gfox_ref, gpost_ref, gffn_ref, o_ref, h2_ref):
    fox_n = _rms(fox, gfox_ref[...]).astype(BF16)
    y = _dot(fox_n, wout_ref[:FOX_WIDTH, :]) + _dot(rest, wout_ref[FOX_WIDTH:, :])
    x_mid = x + _rms(y, gpost_ref[...])
    o_ref[...] = x_mid
    h2_ref[...] = _rms(x_mid, gffn_ref[...]).astype(BF16)


def _split3(c):
    hi = c.astype(BF16).astype(F32)
    r = c - hi
    mid = r.astype(BF16).astype(F32)
    lo = (r - mid).astype(BF16).astype(F32)
    return hi, mid, lo


_BIAS_ONES = 6


def _fox_prompt_kernel(qt_ref, kb_ref, vt_ref, call_ref, crow_ref, rest_ref, x_ref, wout_ref, gfox_ref,
                       gpost_ref, gffn_ref, o_ref, h2_ref, kaug_scr, qa_scr, sa_scr, sb_scr, m_scr, l_scr, acc_scr):
    tq = qt_ref.shape[1]
    tk = vt_ref.shape[-1]
    s_len = kb_ref.shape[0]
    slab = 2 * LANES
    i = pl.program_id(1)

    @pl.when(i == 0)
    def _():
        n_slab = FOX_HEADS // HEAD_PAIR
        src = lax.broadcasted_iota(jnp.int32, (LANES, n_slab * LANES), 0)
        dst = lax.broadcasted_iota(jnp.int32, (LANES, n_slab * LANES), 1)
        head, piece = src % FOX_HEADS, src // FOX_HEADS
        piece_dst = (head // HEAD_PAIR) * LANES + 3 * (head % HEAD_PAIR) + piece
        is_piece = (src < 3 * FOX_HEADS) & (dst == piece_dst)
        is_one = (src == 3 * FOX_HEADS) & (dst % LANES >= _BIAS_ONES) & (dst % LANES < _BIAS_ONES + 3)
        place = jnp.where(is_piece | is_one, 1.0, 0.0).astype(BF16)
        pad_row = lax.broadcasted_iota(jnp.int32, (LANES - 3 * FOX_HEADS, tk), 0)
        pad = jnp.where(pad_row == 0, 1.0, 0.0)
        for r in range(s_len // tk):
            rows = slice(r * tk, (r + 1) * tk)
            pieces_t = jnp.concatenate(_split3(call_ref[:, rows] * LOG2E) + (pad,), axis=0)
            bias = _dot(pieces_t.T.astype(BF16), place)
            for p in range(n_slab):
                kaug_scr[rows, p * slab:p * slab + LANES] = kb_ref[rows, p * LANES:(p + 1) * LANES]
                kaug_scr[rows, p * slab + LANES:(p + 1) * slab] = bias[:, p * LANES:(p + 1) * LANES].astype(BF16)

    rowi = lax.broadcasted_iota(jnp.int32, (LANES, tq), 0)
    diag = lax.broadcasted_iota(jnp.int32, (tk, tq), 0) <= lax.broadcasted_iota(jnp.int32, (tk, tq), 1)
    crow = crow_ref[...]
    zero_q = jnp.zeros((LANES, tq), BF16)
    for p in range(FOX_HEADS // HEAD_PAIR):
        qtp = qt_ref[p * LANES:(p + 1) * LANES, :]
        for hh in range(HEAD_PAIR):
            own = (rowi >= hh * HEAD_DIM) & (rowi < (hh + 1) * HEAD_DIM)
            chi, cmid, clo = _split3(crow[2 * p + hh:2 * p + hh + 1, :] * LOG2E)
            br = jnp.where(rowi == _BIAS_ONES, chi,
                           jnp.where(rowi == _BIAS_ONES + 1, cmid, jnp.where(rowi == _BIAS_ONES + 2, clo, 0.0)))
            br = jnp.where((rowi >= 3 * hh) & (rowi < 3 * hh + 3), -1.0, br)
            qa_scr[2 * p + hh, :LANES, :] = jnp.where(own, qtp, zero_q)
            qa_scr[2 * p + hh, LANES:, :] = br.astype(BF16)

    m_scr[...] = jnp.full(m_scr.shape, NEG_INF, F32)
    l_scr[...] = jnp.zeros(l_scr.shape, F32)
    acc_scr[...] = jnp.zeros(acc_scr.shape, F32)

    def scores_to(buf_ref, j):
        off = pl.multiple_of(j * tk, tk)
        for h in range(FOX_HEADS):
            ka = kaug_scr[pl.ds(off, tk), (h // HEAD_PAIR) * slab:(h // HEAD_PAIR + 1) * slab]
            buf_ref[h] = _dot(ka, qa_scr[h])

    ones_rows = jnp.ones((BF16_SUBLANES, tk), BF16)

    def absorb_from(buf_ref, j, mask):
        vt = vt_ref[j]
        for h in range(FOX_HEADS):
            feat = slice(h * HEAD_DIM, (h + 1) * HEAD_DIM)
            s = buf_ref[h]
            if mask is not None:
                s = jnp.where(mask, s, NEG_INF)
            m = m_scr[h:h + 1, :]
            m_new = jnp.maximum(m, jnp.max(s, axis=0, keepdims=True))
            alpha = jnp.exp2(m - m_new)
            e = jnp.exp2(s - m_new).astype(BF16)
            m_scr[h:h + 1, :] = m_new
            pv = _dot(jnp.concatenate([vt[feat, :], ones_rows], axis=0), e)
            l_scr[h:h + 1, :] = alpha * l_scr[h:h + 1, :] + pv[HEAD_DIM:HEAD_DIM + 1, :]
            acc_scr[feat, :] = alpha * acc_scr[feat, :] + pv[:HEAD_DIM, :]

    odd = i % 2

    @pl.when(odd == 1)
    def _():
        scores_to(sb_scr, 0)
        scores_to(sa_scr, 1)
        absorb_from(sb_scr, 0, None)

    @pl.when(odd == 0)
    def _():
        scores_to(sa_scr, 0)

    def step(jj, _):
        t = odd + 2 * jj
        scores_to(sb_scr, t + 1)
        absorb_from(sa_scr, t, None)
        scores_to(sa_scr, t + 2)
        absorb_from(sb_scr, t + 1, None)
        return 0

    lax.fori_loop(0, i // 2, step, 0)
    absorb_from(sa_scr, i, diag)
    fox_t = jnp.concatenate([acc_scr[h * HEAD_DIM:(h + 1) * HEAD_DIM, :] / l_scr[h:h + 1, :]
                             for h in range(FOX_HEADS)], axis=0)
    fox = fox_t.T
    _mix_out(fox, rest_ref[...], x_ref[...], wout_ref, gfox_ref, gpost_ref, gffn_ref, o_ref, h2_ref)


def _mix_weights(lw):
    return (lw["w_out"], lw["g_fox"], lw["g_post_mix"], lw["g_pre_ffn"])


def _fox_prompt(qt, kb, vt, c_row, rest, x, lw, layer):
    nb, s, _ = x.shape
    tq, tk = ATTN_Q_ROWS, ATTN_K_ROWS
    assert tq == tk and vt.shape[-1] == tk
    nq = s // tq
    qrow = lambda w: pl.BlockSpec((None, tq, w), lambda b, i: (b, i, 0))
    qcol = lambda r: pl.BlockSpec((None, r, tq), lambda b, i: (b, 0, i))
    full = lambda a: pl.BlockSpec((None,) + a.shape[1:], lambda b, i: (b,) + (0,) * (a.ndim - 1))
    return pl.pallas_call(
        _fox_prompt_kernel,
        grid=(nb, nq),
        in_specs=[qcol(FOX_WIDTH), full(kb), full(vt), full(c_row), qcol(FOX_HEADS),
                  qrow(SGU_WIDTH + MEM_WIDTH), qrow(D_MODEL)] + [_layer_spec(a, layer) for a in _mix_weights(lw)],
        out_specs=[qrow(D_MODEL), qrow(D_MODEL)],
        out_shape=[jax.ShapeDtypeStruct(x.shape, F32), jax.ShapeDtypeStruct(x.shape, BF16)],
        scratch_shapes=[pltpu.VMEM((s, 2 * FOX_WIDTH), BF16),
                        pltpu.VMEM((FOX_HEADS, 2 * LANES, tq), BF16),
                        pltpu.VMEM((FOX_HEADS, tk, tq), F32), pltpu.VMEM((FOX_HEADS, tk, tq), F32),
                        pltpu.VMEM((FOX_HEADS, tq), F32), pltpu.VMEM((FOX_HEADS, tq), F32),
                        pltpu.VMEM((FOX_WIDTH, tq), F32)],
        compiler_params=_params("parallel", "arbitrary"),
        name="fox_mix_prompt",
    )(qt, kb, vt, c_row, c_row, rest, x, *_mix_weights(lw))


def _fox_sample_kernel(q_ref, kn_ref, vn_ref, hk_ref, hv_ref, hrow_ref, ncol_ref, nrow_ref, rest_ref, x_ref,
                       wout_ref, gfox_ref, gpost_ref, gffn_ref, o_ref, h2_ref, fox_scr):
    t = q_ref.shape[0]
    b = pl.program_id(0)
    rows = FOX_HEADS * t
    qt = jnp.concatenate([q_ref[...]] * FOX_HEADS, axis=0)
    row_head = lax.broadcasted_iota(jnp.int32, (rows, FOX_WIDTH), 0) // t
    lane_head = lax.broadcasted_iota(jnp.int32, (rows, FOX_WIDTH), 1) // HEAD_DIM
    own = row_head == lane_head
    qb = jnp.where(own, qt, jnp.zeros_like(qt))
    per_head = lambda a: jnp.concatenate(
        [jnp.broadcast_to(a[h:h + 1, :], (t, a.shape[1])) for h in range(FOX_HEADS)], axis=0)
    bq = ncol_ref[...]
    s_hist = _dot(qb, hk_ref[...].astype(BF16)) + bq - per_head(hrow_ref[...])
    s_new = _dot_nt(qb, kn_ref[...]) + bq - per_head(nrow_ref[:, :t])
    causal = (lax.broadcasted_iota(jnp.int32, (rows, t), 1)
              <= lax.broadcasted_iota(jnp.int32, (rows, t), 0) % t)
    s_new = jnp.where(causal, s_new, NEG_INF)
    m = jnp.maximum(jnp.max(s_hist, axis=-1, keepdims=True), jnp.max(s_new, axis=-1, keepdims=True))
    e_hist = jnp.exp(s_hist - m)
    e_new = jnp.exp(s_new - m)
    l = jnp.sum(e_hist, axis=-1, keepdims=True) + jnp.sum(e_new, axis=-1, keepdims=True)
    o = (_dot_nt(e_hist.astype(BF16), hv_ref[...].astype(BF16)) + _dot(e_new.astype(BF16), vn_ref[...])) / l
    o = jnp.where(own, o, 0.0)
    fox = o[0:t, :]
    for h in range(1, FOX_HEADS):
        fox = fox + o[h * t:(h + 1) * t, :]
    fox_scr[pl.ds(pl.multiple_of(b * t, t), t), :] = fox

    @pl.when(b == pl.num_programs(0) - 1)
    def _():
        _mix_out(fox_scr[...], rest_ref[...], x_ref[...], wout_ref, gfox_ref, gpost_ref, gffn_ref, o_ref, h2_ref)


def _fox_sample(q, kb, vb, hk, hv, layer, hist_row, new_col, new_row, rest, x, lw):
    nb, t, _ = q.shape
    per_b = lambda a: pl.BlockSpec((None,) + a.shape[1:], lambda b: (b,) + (0,) * (a.ndim - 1))
    cache = pl.BlockSpec((None, None) + hk.shape[2:], lambda b: (layer, b, 0, 0))
    const = lambda a: pl.BlockSpec(a.shape, lambda b: (0,) * a.ndim)
    return pl.pallas_call(
        _fox_sample_kernel,
        grid=(nb,),
        in_specs=[per_b(q), per_b(kb), per_b(vb), cache, cache, per_b(hist_row), per_b(new_col),
                  per_b(new_row), const(rest), const(x)] + [_layer_spec(a, layer) for a in _mix_weights(lw)],
        out_specs=[const(x), const(x)],
        out_shape=[jax.ShapeDtypeStruct(x.shape, F32), jax.ShapeDtypeStruct(x.shape, BF16)],
        scratch_shapes=[pltpu.VMEM((nb * t, FOX_WIDTH), F32)],
        compiler_params=_params("arbitrary"),
        name="fox_mix_sample",
    )(q, kb, vb, hk, hv, hist_row, new_col, new_row, rest, x, *_mix_weights(lw))


def _ffn_body(x_ref, h2_ref, wup_ref, wdw_ref, bdw_ref, wd_ref, gpost_ref, o_ref, g_scr, shifted, emit):
    h2 = h2_ref[...]
    nf = FFN_DIM // FFN_COLS

    def up(c):
        conv_cols = slice(c * FFN_COLS, (c + 1) * FFN_COLS)
        lin_cols = slice(FFN_DIM + c * FFN_COLS, FFN_DIM + (c + 1) * FFN_COLS)
        return _dot(h2, wup_ref[:, conv_cols]), _dot(h2, wup_ref[:, lin_cols])

    nxt = up(0)
    for c in range(nf):
        a, lin = nxt
        if c + 1 < nf:
            nxt = up(c + 1)
        cols = slice(c * FFN_COLS, (c + 1) * FFN_COLS)
        a1, a2 = shifted(a, c)
        conv = bdw_ref[:, cols] + wdw_ref[0:1, cols] * a2
        conv = conv + wdw_ref[1:2, cols] * a1
        conv = conv + wdw_ref[2:3, cols] * a
        g_scr[:, cols] = (jax.nn.silu(conv) * lin).astype(BF16)
        emit(a, c)
    o_ref[...] = x_ref[...] + _rms(_dot(g_scr[...], wd_ref[...]), gpost_ref[...])


def _ffn_prompt_kernel(x_ref, h2_ref, wup_ref, wdw_ref, bdw_ref, wd_ref, gpost_ref, hist_ref,
                       o_ref, tail_ref, g_scr, carry_scr, work_scr):
    tm = x_ref.shape[0]
    head = SUBLANES

    @pl.when(pl.program_id(1) == 0)
    def _():
        carry_scr[0:head - (CONV_WIDTH - 1), :] = jnp.zeros((head - (CONV_WIDTH - 1), FFN_DIM), F32)
        carry_scr[head - (CONV_WIDTH - 1):head, :] = hist_ref[...]

    def shifted(a, c):
        cols = slice(c * FFN_COLS, (c + 1) * FFN_COLS)
        work = work_scr.at[c % 2]
        work[0:head, :] = carry_scr[:, cols]
        work[head:head + tm, :] = a
        return work[head - 1:head - 1 + tm, :], work[head - 2:head - 2 + tm, :]

    def emit(a, c):
        cols = slice(c * FFN_COLS, (c + 1) * FFN_COLS)
        carry_scr[:, cols] = a[tm - head:, :]
        tail_ref[:, cols] = a[tm - head:, :]

    _ffn_body(x_ref, h2_ref, wup_ref, wdw_ref, bdw_ref, wd_ref, gpost_ref, o_ref, g_scr, shifted, emit)


def _ffn_sample_kernel(x_ref, h2_ref, wup_ref, wdw_ref, bdw_ref, wd_ref, gpost_ref, e1_ref, e2_ref,
                       o_ref, a_ref, g_scr, work_scr, *, seg):
    tm = x_ref.shape[0]
    head = SUBLANES
    rmod = lax.broadcasted_iota(jnp.int32, (tm, FFN_COLS), 0) % seg

    def shifted(a, c):
        cols = slice(c * FFN_COLS, (c + 1) * FFN_COLS)
        work = work_scr.at[c % 2]
        work[0:head, :] = jnp.zeros((head, FFN_COLS), F32)
        work[head:head + tm, :] = a
        a1 = jnp.where(rmod >= 1, work[head - 1:head - 1 + tm, :], e1_ref[:, cols])
        a2 = jnp.where(rmod >= 2, work[head - 2:head - 2 + tm, :], e2_ref[:, cols])
        return a1, a2

    def emit(a, c):
        a_ref[:, c * FFN_COLS:(c + 1) * FFN_COLS] = a

    _ffn_body(x_ref, h2_ref, wup_ref, wdw_ref, bdw_ref, wd_ref, gpost_ref, o_ref, g_scr, shifted, emit)


def _ffn_weights(lw):
    return (lw["w_up"], lw["w_dwconv"], lw["b_dwconv"], lw["w_down"], lw["g_post_ffn"])


def _ffn_weight_specs(lw, layer):
    return [_layer_spec(a, layer, pipeline_mode=pl.Buffered(1)) for a in _ffn_weights(lw)]


def _ffn_prompt(x, h2, hist, lw, layer):
    nb, s, _ = x.shape
    tm = FFN_ROWS
    nt = s // tm
    row = pl.BlockSpec((None, tm, D_MODEL), lambda b, i: (b, i, 0))
    return pl.pallas_call(
        _ffn_prompt_kernel,
        grid=(nb, nt),
        in_specs=[row, row] + _ffn_weight_specs(lw, layer) + [
            pl.BlockSpec((None, CONV_WIDTH - 1, FFN_DIM), lambda b, i: (b, 0, 0))],
        out_specs=[pl.BlockSpec((None, tm, D_MODEL), lambda b, i: (b, i, 0)),
                   pl.BlockSpec((None, None, SUBLANES, FFN_DIM), lambda b, i: (b, i, 0, 0))],
        out_shape=[jax.ShapeDtypeStruct(x.shape, F32),
                   jax.ShapeDtypeStruct((nb, nt, SUBLANES, FFN_DIM), F32)],
        scratch_shapes=[pltpu.VMEM((tm, FFN_DIM), BF16),
                        pltpu.VMEM((SUBLANES, FFN_DIM), F32),
                        pltpu.VMEM((2, tm + SUBLANES, FFN_COLS), F32)],
        compiler_params=_params("parallel", "arbitrary"),
        name="conv_ffn_prompt",
    )(x, h2, *_ffn_weights(lw), hist)


def _ffn_sample(x, h2, e1, e2, lw, layer, seg):
    rows, _ = x.shape
    whole = lambda w: pl.BlockSpec((rows, w), lambda i: (0, 0))
    return pl.pallas_call(
        functools.partial(_ffn_sample_kernel, seg=seg),
        grid=(1,),
        in_specs=[whole(D_MODEL), whole(D_MODEL)] + _ffn_weight_specs(lw, layer) + [whole(FFN_DIM), whole(FFN_DIM)],
        out_specs=[whole(D_MODEL), whole(FFN_DIM)],
        out_shape=[jax.ShapeDtypeStruct(x.shape, F32), jax.ShapeDtypeStruct((rows, FFN_DIM), F32)],
        scratch_shapes=[pltpu.VMEM((rows, FFN_DIM), BF16),
                        pltpu.VMEM((2, rows + SUBLANES, FFN_COLS), F32)],
        compiler_params=_params("arbitrary"),
        name="conv_ffn_sample",
    )(x, h2, *_ffn_weights(lw), e1, e2)


def _prepare_weights(g_pre_mix, w_in, b_forget, w_spatial, b_spatial, g_sgu, g_group_out, w_out, g_post_mix,
                     g_pre_ffn, w_up, w_dwconv, b_dwconv, w_down, g_post_ffn, dec_seq):
    depth = w_in.shape[0]
    row = lambda a: a.reshape(depth, 1, -1)
    c0 = 3 * FOX_WIDTH
    c1 = c0 + FOX_HEADS
    c2 = c1 + 2 * SGU_WIDTH
    w_b = jnp.concatenate([w_in[:, :, c1:c2], w_in[:, :, c2:], w_in[:, :, c0:c1],
                           jnp.zeros((depth, D_MODEL, B_COLS - (2 * SGU_WIDTH + MEM_WIDTH + FOX_HEADS)), F32)],
                          axis=2)
    group_dim = SGU_WIDTH // SGU_GROUPS
    ws_prompt = jnp.concatenate([w_spatial[:, g] for g in range(SGU_GROUPS)], axis=2)
    bt_prompt = jnp.repeat(jnp.swapaxes(b_spatial, 1, 2), group_dim, axis=2)
    reps = GMLP_CHUNK // dec_seq
    blk = (jnp.arange(GMLP_CHUNK)[:, None] // dec_seq) == (jnp.arange(GMLP_CHUNK)[None, :] // dec_seq)
    ws_sample = jnp.concatenate(
        [jnp.where(blk, jnp.tile(w_spatial[:, g, :dec_seq, :dec_seq], (1, reps, reps)), 0.0)
         for g in range(SGU_GROUPS)], axis=2)
    bt_sample = jnp.tile(jnp.repeat(jnp.swapaxes(b_spatial[:, :, :dec_seq], 1, 2), group_dim, axis=2),
                         (1, reps, 1))
    return {
        "g_pre_mix": row(g_pre_mix), "w_qkv": w_in[:, :, :c0].astype(BF16), "w_b": w_b.astype(BF16),
        "b_forget": jnp.pad(row(b_forget), ((0, 0), (0, 0), (0, LANES - FOX_HEADS))), "g_sgu": row(g_sgu),
        "g_fox": row(g_group_out[:, :FOX_WIDTH]), "g_rest": row(g_group_out[:, FOX_WIDTH:]),
        "w_out": w_out.astype(BF16), "g_post_mix": row(g_post_mix), "g_pre_ffn": row(g_pre_ffn),
        "w_up": w_up.astype(BF16), "w_dwconv": w_dwconv, "b_dwconv": row(b_dwconv),
        "w_down": w_down.astype(BF16), "g_post_ffn": row(g_post_ffn),
        "ws_prompt": ws_prompt, "bt_prompt": bt_prompt, "ws_sample": ws_sample, "bt_sample": bt_sample,
    }


def kernel(x_prompt, x_sample, mem_prompt, cache_fox_k, cache_fox_v, cache_fox_logf, cache_mem_k, cache_mem_v,
           cache_ffn_conv, g_pre_mix, w_in, b_forget, w_spatial, b_spatial, g_sgu, g_mem, w_mem_kv, g_group_out,
           w_out, g_post_mix, g_pre_ffn, w_up, w_dwconv, b_dwconv, w_down, g_post_ffn):
    depth = w_in.shape[0]
    batch, seq, _ = x_prompt.shape
    dec_batch, dec_seq, _ = x_sample.shape
    past = cache_fox_k.shape[2]
    dec_rows = dec_batch * dec_seq

    mem_k_all, mem_v_all = _memory_kv(mem_prompt, g_mem, w_mem_kv.astype(BF16))
    feature_major = lambda a: jnp.transpose(a, (0, 1, 3, 4, 2)).reshape(a.shape[0], a.shape[1], -1, a.shape[2])
    token_major = lambda a, heads: jnp.transpose(
        a.reshape(a.shape[0], a.shape[1], heads, HEAD_DIM, a.shape[3]), (0, 1, 4, 2, 3))
    flat_mem = lambda a: a.reshape(-1, MEM_WIDTH, N_MEM)
    pmk, pmv = flat_mem(mem_k_all), flat_mem(mem_v_all)
    smk, smv = flat_mem(feature_major(cache_mem_k)), flat_mem(feature_major(cache_mem_v))
    hk = feature_major(cache_fox_k)
    hv = feature_major(cache_fox_v)

    lw = _prepare_weights(g_pre_mix, w_in, b_forget, w_spatial, b_spatial, g_sgu, g_group_out, w_out, g_post_mix,
                          g_pre_ffn, w_up, w_dwconv, b_dwconv, w_down, g_post_ffn, dec_seq)
    hist_rows = _lane_cumsum(jnp.swapaxes(cache_fox_logf, 2, 3).reshape(depth * dec_batch * FOX_HEADS, past),
                             minus_total=True).reshape(depth, dec_batch, FOX_HEADS, past)
    pad_rows = lambda a: jnp.pad(a, ((0, 0), (0, 0), (0, dec_seq - a.shape[2]), (0, 0))
                                 ).reshape(depth, dec_rows, FFN_DIM)
    e1_all = pad_rows(cache_ffn_conv[:, :, 1:2, :])
    e2_all = pad_rows(cache_ffn_conv)
    zeros_hist = jnp.zeros((batch, CONV_WIDTH - 1, FFN_DIM), F32)

    yp = x_prompt
    ys = x_sample.reshape(1, dec_rows, D_MODEL)
    outs = {name: [] for name in ("logf_p", "conv_p", "k_s", "v_s", "logf_s", "gv_s", "conv_s")}
    stacked_kv = ()
    per_b = lambda a: a.reshape(dec_batch, dec_seq, a.shape[-1])
    for l in range(depth):
        qt, k_all, v_all, kb, vt, logf, rest = _in_proj(yp, pmk, pmv, l * batch, 1, lw, IN_PROJ_ROWS,
                                                        lw["ws_prompt"], lw["bt_prompt"], True,
                                                        layer=l, depth=depth, stacked_kv=stacked_kv)
        stacked_kv = (k_all, v_all)
        c_row = _lane_cumsum(logf.reshape(batch * FOX_HEADS, seq)).reshape(batch, FOX_HEADS, seq)
        yp, h2 = _fox_prompt(qt, kb, vt, c_row, rest, yp, lw, l)
        yp, tail = _ffn_prompt(yp, h2, zeros_hist, lw, l)
        outs["logf_p"].append(logf)
        outs["conv_p"].append(tail[:, -1, SUBLANES - (CONV_WIDTH - 1):, :])

        q, k, v, kb, vb, logf, rest, vrows = _in_proj(ys, smk, smv, l * dec_batch, dec_batch, lw, dec_rows,
                                                      lw["ws_sample"], lw["bt_sample"], False,
                                                      layer=l, depth=depth)
        logf_new_row = jnp.swapaxes(per_b(logf), 1, 2).reshape(dec_batch * FOX_HEADS, dec_seq)
        new_row = _lane_cumsum(jnp.pad(logf_new_row, ((0, 0), (0, LANES - dec_seq)))
                               ).reshape(dec_batch, FOX_HEADS, LANES)
        new_col = new_row[:, :, :dec_seq].reshape(dec_batch, FOX_HEADS * dec_seq, 1)
        ys2, h2s = _fox_sample(per_b(q[0]), per_b(kb[0]), per_b(vb[0]), hk, hv, l, hist_rows[l], new_col, new_row,
                               rest[0], ys[0], lw)
        ys_flat, a_all = _ffn_sample(ys2, h2s, e1_all[l], e2_all[l], lw, l, dec_seq)
        ys = ys_flat.reshape(1, dec_rows, D_MODEL)
        outs["k_s"].append(per_b(k[0]).reshape(dec_batch, dec_seq, FOX_HEADS, HEAD_DIM))
        outs["v_s"].append(per_b(v[0]).reshape(dec_batch, dec_seq, FOX_HEADS, HEAD_DIM))
        outs["logf_s"].append(per_b(logf[0]))
        outs["gv_s"].append(per_b(vrows[0]))
        outs["conv_s"].append(a_all.reshape(dec_batch, dec_seq, FFN_DIM)[:, dec_seq - (CONV_WIDTH - 1):, :])

    st = {name: jnp.stack(vals) for name, vals in outs.items()}
    return (yp, ys.reshape(dec_batch, dec_seq, D_MODEL),
            token_major(stacked_kv[0], FOX_HEADS), token_major(stacked_kv[1], FOX_HEADS),
            jnp.swapaxes(st["logf_p"], 2, 3),
            token_major(mem_k_all, MEM_HEADS), token_major(mem_v_all, MEM_HEADS),
            st["conv_p"], st["k_s"], st["v_s"], st["logf_s"], st["gv_s"], st["conv_s"])
```

```python
import functools

import jax
import jax.numpy as jnp
from jax import lax
from jax.experimental import pallas as pl
from jax.experimental.pallas import tpu as pltpu

D_MODEL = 1024
HEAD_DIM = 64
FOX_WIDTH = 512
FOX_HEADS = 8
SGU_WIDTH = 256
SGU_GROUPS = 4
GMLP_CHUNK = 128
CHUNK = 64
MEM_WIDTH = 256
MEM_HEADS = 4
N_MEM = 256
FFN_DIM = 2816
CONV_WIDTH = 3
RMS_EPS = 1e-6
NEG_INF = -1e30
QK_SCALE = HEAD_DIM ** -0.5
LOG2E = 1.4426950408889634

LANES = 128
SUBLANES = 8
BF16_SUBLANES = 2 * SUBLANES
HEAD_PAIR = LANES // HEAD_DIM
VMEM_LIMIT_BYTES = 56 * 1024 * 1024

IN_PROJ_ROWS = 512
ATTN_Q_ROWS = 512
ATTN_K_ROWS = 256
FFN_ROWS = 512
FFN_COLS = 256
B_COLS = 896

BF16 = jnp.bfloat16
F32 = jnp.float32


def _rms(x, g):
    y = x * lax.rsqrt(jnp.mean(x * x, axis=-1, keepdims=True) + RMS_EPS)
    return y * g


def _dot(a, b):
    return jnp.dot(a, b, preferred_element_type=F32)


def _dot_nt(a, b):
    return lax.dot_general(a, b, (((1,), (1,)), ((), ())), preferred_element_type=F32)


def _params(*semantics):
    return pltpu.CompilerParams(dimension_semantics=semantics, vmem_limit_bytes=VMEM_LIMIT_BYTES)


def _layer_spec(a, layer, **kwargs):
    return pl.BlockSpec((None,) + a.shape[1:], lambda *_: (layer,) + (0,) * (a.ndim - 1), **kwargs)


def _memkv_kernel(mem_ref, g_ref, w_ref, mk_ref, mv_ref):
    h = _rms(mem_ref[...], g_ref[...]).astype(BF16)
    kv = _dot(h, w_ref[...])
    mk_ref[...] = kv[:, :MEM_WIDTH].T
    mv_ref[...] = kv[:, MEM_WIDTH:].T


def _memory_kv(mem, g_mem, w_mem_kv_bf):
    depth = g_mem.shape[0]
    batch = mem.shape[0]
    out = jax.ShapeDtypeStruct((depth, batch, MEM_WIDTH, N_MEM), F32)
    return pl.pallas_call(
        _memkv_kernel,
        grid=(depth, batch),
        in_specs=[
            pl.BlockSpec((None, N_MEM, D_MODEL), lambda l, b: (b, 0, 0)),
            pl.BlockSpec((None, 1, D_MODEL), lambda l, b: (l, 0, 0)),
            pl.BlockSpec((None, D_MODEL, 2 * MEM_WIDTH), lambda l, b: (l, 0, 0)),
        ],
        out_specs=[
            pl.BlockSpec((None, None, MEM_WIDTH, N_MEM), lambda l, b: (l, b, 0, 0)),
            pl.BlockSpec((None, None, MEM_WIDTH, N_MEM), lambda l, b: (l, b, 0, 0)),
        ],
        out_shape=[out, out],
        compiler_params=_params("arbitrary", "arbitrary"),
        name="memory_kv",
    )(mem, g_mem.reshape(depth, 1, D_MODEL), w_mem_kv_bf)


def _inproj_kernel(x_ref, gpre_ref, wqkv_ref, wb_ref, bfg_ref, ws_ref, bt_ref, gsgu_ref, gg_ref,
                   mk_ref, mv_ref, *out_refs, transposed, n_stack_in):
    out_refs = out_refs[n_stack_in:]
    tm = x_ref.shape[0]
    n_mem_blocks = mk_ref.shape[0]
    if transposed:
        qt_ref, k_ref, v_ref, kb_ref, vt_ref, lf_ref, rest_ref = out_refs
        vrows_ref = None
    else:
        q_ref, k_ref, v_ref, kb_ref, vb_ref, lf_ref, rest_ref, vrows_ref = out_refs
    h = _rms(x_ref[...], gpre_ref[...]).astype(BF16)

    yb = _dot(h, wb_ref[...])
    fg = yb[:, 2 * SGU_WIDTH + MEM_WIDTH:2 * SGU_WIDTH + MEM_WIDTH + LANES]
    lf = jax.nn.log_sigmoid(fg + bfg_ref[...])
    if transposed:
        lf_ref[...] = lf.T[:FOX_HEADS, :]
    else:
        lf_ref[...] = lf[:, :FOX_HEADS]

    qmb = (yb[:, 2 * SGU_WIDTH:2 * SGU_WIDTH + MEM_WIDTH] * QK_SCALE).astype(BF16)
    rows = tm // n_mem_blocks
    hlane = lax.broadcasted_iota(jnp.int32, (rows, MEM_WIDTH), 1) // HEAD_DIM
    zero_q = jnp.zeros((rows, MEM_WIDTH), BF16)
    mem_scores = []
    for bi in range(n_mem_blocks):
        qb = qmb[bi * rows:(bi + 1) * rows, :]
        mkt = mk_ref[bi].astype(BF16)
        mem_scores.append([_dot(jnp.where(hlane == hd, qb, zero_q), mkt) for hd in range(MEM_HEADS)])

    q = _dot(h, wqkv_ref[:, :FOX_WIDTH]) * (QK_SCALE * LOG2E if transposed else QK_SCALE)

    z = jax.nn.gelu(yb[:, :2 * SGU_WIDTH])
    u = z[:, :SGU_WIDTH]
    vv = _rms(z[:, SGU_WIDTH:], gsgu_ref[...])
    if vrows_ref is not None:
        vrows_ref[...] = vv
    vvb = vv.astype(BF16)
    mem_exp = [[jnp.exp(s - jnp.max(s, axis=-1, keepdims=True)) for s in per_block] for per_block in mem_scores]

    wrow = lax.broadcasted_iota(jnp.int32, (GMLP_CHUNK, SGU_GROUPS * GMLP_CHUNK), 0)
    wcol = lax.broadcasted_iota(jnp.int32, (GMLP_CHUNK, SGU_GROUPS * GMLP_CHUNK), 1)
    wmask = ((wcol % GMLP_CHUNK) // CHUNK) <= (wrow // CHUNK)
    wcat = jnp.where(wmask, ws_ref[...], 0.0).astype(BF16)
    glane = lax.broadcasted_iota(jnp.int32, (GMLP_CHUNK, SGU_WIDTH), 1) // (SGU_WIDTH // SGU_GROUPS)
    zero_chunk = jnp.zeros((GMLP_CHUNK, SGU_WIDTH), BF16)
    mixed_parts = []
    for c in range(tm // GMLP_CHUNK):
        vc = vvb[c * GMLP_CHUNK:(c + 1) * GMLP_CHUNK, :]
        rhs = jnp.concatenate([jnp.where(glane == g, vc, zero_chunk) for g in range(SGU_GROUPS)], axis=0)
        mixed_parts.append(_dot(wcat, rhs))
    mem_pv = []
    for bi in range(n_mem_blocks):
        mvt = mv_ref[bi].astype(BF16)
        mem_pv.append([_dot_nt(e.astype(BF16), mvt) for e in mem_exp[bi]])

    k = _dot(h, wqkv_ref[:, FOX_WIDTH:2 * FOX_WIDTH])

    if transposed:
        qt_ref[...] = q.T.astype(BF16)
    else:
        q_ref[...] = q.astype(BF16)
    sgu = jnp.concatenate([u[c * GMLP_CHUNK:(c + 1) * GMLP_CHUNK, :] * (mixed + bt_ref[...])
                           for c, mixed in enumerate(mixed_parts)], axis=0)
    rest_ref[:, :SGU_WIDTH] = _rms(sgu, gg_ref[:, :SGU_WIDTH]).astype(BF16)
    mem_parts = []
    for bi in range(n_mem_blocks):
        out = jnp.zeros((rows, MEM_WIDTH), F32)
        for hd in range(MEM_HEADS):
            o = mem_pv[bi][hd] / jnp.sum(mem_exp[bi][hd], axis=-1, keepdims=True)
            out = jnp.where(hlane == hd, o, out)
        mem_parts.append(out)
    mem = mem_parts[0] if n_mem_blocks == 1 else jnp.concatenate(mem_parts, axis=0)
    rest_ref[:, SGU_WIDTH:] = _rms(mem, gg_ref[:, SGU_WIDTH:]).astype(BF16)

    v = _dot(h, wqkv_ref[:, 2 * FOX_WIDTH:])

    kb_ref[...] = k.astype(BF16)
    if transposed:
        k_ref[...] = k.T
        vt = v.T
        v_ref[...] = vt
        vtb = vt.astype(BF16)
        tk = vt_ref.shape[-1]
        for c in range(tm // tk):
            vt_ref[c] = vtb[:, c * tk:(c + 1) * tk]
    else:
        k_ref[...] = k
        v_ref[...] = v
        vb_ref[...] = v.astype(BF16)


def _in_proj(x, mem_k, mem_v, mem_first, mem_per_tile, lw, tm, spatial_w, spatial_b, transposed,
             layer, depth, stacked_kv=()):
    nb, t, _ = x.shape
    nt = t // tm
    if transposed:
        assert mem_per_tile == 1
        mem_index = lambda b, i: (mem_first + b, 0, 0)
    else:
        mem_index = lambda b, i: (mem_first // mem_per_tile + b * nt + i, 0, 0)
    row = lambda w: pl.BlockSpec((None, tm, w), lambda b, i: (b, i, 0))
    memspec = pl.BlockSpec((mem_per_tile, MEM_WIDTH, N_MEM), mem_index)
    f32 = lambda w: jax.ShapeDtypeStruct((nb, t, w), F32)
    bf = lambda w: jax.ShapeDtypeStruct((nb, t, w), BF16)
    consts = (lw["g_pre_mix"], lw["w_qkv"], lw["w_b"], lw["b_forget"], spatial_w, spatial_b,
              lw["g_sgu"], lw["g_rest"])
    if transposed:
        tk = ATTN_K_ROWS
        stack_spec = pl.BlockSpec((None, None, FOX_WIDTH, tm), lambda b, i: (layer, b, 0, i))
        stack_shape = jax.ShapeDtypeStruct((depth, nb, FOX_WIDTH, t), F32)
        out_specs = [pl.BlockSpec((None, FOX_WIDTH, tm), lambda b, i: (b, 0, i)),
                     stack_spec, stack_spec, row(FOX_WIDTH),
                     pl.BlockSpec((None, tm // tk, FOX_WIDTH, tk), lambda b, i: (b, i, 0, 0)),
                     pl.BlockSpec((None, FOX_HEADS, tm), lambda b, i: (b, 0, i)), row(SGU_WIDTH + MEM_WIDTH)]
        out_shape = [jax.ShapeDtypeStruct((nb, FOX_WIDTH, t), BF16), stack_shape, stack_shape,
                     bf(FOX_WIDTH), jax.ShapeDtypeStruct((nb, t // tk, FOX_WIDTH, tk), BF16),
                     jax.ShapeDtypeStruct((nb, FOX_HEADS, t), F32), bf(SGU_WIDTH + MEM_WIDTH)]
    else:
        out_specs = [row(FOX_WIDTH)] * 5 + [row(FOX_HEADS), row(SGU_WIDTH + MEM_WIDTH), row(SGU_WIDTH)]
        out_shape = [bf(FOX_WIDTH), f32(FOX_WIDTH), f32(FOX_WIDTH), bf(FOX_WIDTH), bf(FOX_WIDTH),
                     f32(FOX_HEADS), bf(SGU_WIDTH + MEM_WIDTH), f32(SGU_WIDTH)]
    in_specs = [row(D_MODEL)] + [_layer_spec(a, layer) for a in consts] + [memspec, memspec]
    aliases = {len(in_specs) + n: 1 + n for n in range(len(stacked_kv))}
    in_specs += [pl.BlockSpec(memory_space=pl.ANY)] * len(stacked_kv)
    return pl.pallas_call(
        functools.partial(_inproj_kernel, transposed=transposed, n_stack_in=len(stacked_kv)),
        grid=(nb, nt),
        in_specs=in_specs,
        out_specs=out_specs,
        out_shape=out_shape,
        input_output_aliases=aliases,
        compiler_params=_params("parallel", "arbitrary"),
        name="in_proj",
    )(x, *consts, mem_k, mem_v, *stacked_kv)


def _cumsum_kernel(x_ref, o_ref, *, minus_total):
    x = x_ref[...]
    n = x.shape[-1]
    lane = lax.broadcasted_iota(jnp.int32, x.shape, 1)
    shift = 1
    while shift < n:
        x = x + jnp.where(lane >= shift, pltpu.roll(x, shift, 1), 0.0)
        shift *= 2
    if minus_total:
        x = x - x[:, n - 1:n]
    o_ref[...] = x


def _lane_cumsum(x, minus_total=False):
    return pl.pallas_call(
        functools.partial(_cumsum_kernel, minus_total=minus_total),
        out_shape=jax.ShapeDtypeStruct(x.shape, F32),
        name="lane_cumsum",
    )(x)


def _mix_out(fox, rest, x, wout_ref, gfox_ref, gpost_ref, gffn_ref, o_ref, h2_ref):
    fox_n = _rms(fox, gfox_ref[...]).astype(BF16)
    y = _dot(fox_n, wout_ref[:FOX_WIDTH, :]) + _dot(rest, wout_ref[FOX_WIDTH:, :])
    x_mid = x + _rms(y, gpost_ref[...])
    o_ref[...] = x_mid
    h2_ref[...] = _rms(x_mid, gffn_ref[...]).astype(BF16)


def _split3(c):
    hi = c.astype(BF16).astype(F32)
    r = c - hi
    mid = r.astype(BF16).astype(F32)
    lo = (r - mid).astype(BF16).astype(F32)
    return hi, mid, lo


_BIAS_ONES = 6


def _fox_prompt_kernel(qt_ref, kb_ref, vt_ref, call_ref, crow_ref, rest_ref, x_ref, wout_ref, gfox_ref,
                       gpost_ref, gffn_ref, o_ref, h2_ref, kaug_scr, qa_scr, sa_scr, sb_scr, m_scr, l_scr, acc_scr):
    tq = qt_ref.shape[1]
    tk = vt_ref.shape[-1]
    s_len = kb_ref.shape[0]
    slab = 2 * LANES
    i = pl.program_id(1)

    @pl.when(i == 0)
    def _():
        n_slab = FOX_HEADS // HEAD_PAIR
        src = lax.broadcasted_iota(jnp.int32, (LANES, n_slab * LANES), 0)
        dst = lax.broadcasted_iota(jnp.int32, (LANES, n_slab * LANES), 1)
        head, piece = src % FOX_HEADS, src // FOX_HEADS
        piece_dst = (head // HEAD_PAIR) * LANES + 3 * (head % HEAD_PAIR) + piece
        is_piece = (src < 3 * FOX_HEADS) & (dst == piece_dst)
        is_one = (src == 3 * FOX_HEADS) & (dst % LANES >= _BIAS_ONES) & (dst % LANES < _BIAS_ONES + 3)
        place = jnp.where(is_piece | is_one, 1.0, 0.0).astype(BF16)
        pad_row = lax.broadcasted_iota(jnp.int32, (LANES - 3 * FOX_HEADS, tk), 0)
        pad = jnp.where(pad_row == 0, 1.0, 0.0)
        for r in range(s_len // tk):
            rows = slice(r * tk, (r + 1) * tk)
            pieces_t = jnp.concatenate(_split3(call_ref[:, rows] * LOG2E) + (pad,), axis=0)
            bias = _dot(pieces_t.T.astype(BF16), place)
            for p in range(n_slab):
                kaug_scr[rows, p * slab:p * slab + LANES] = kb_ref[rows, p * LANES:(p + 1) * LANES]
                kaug_scr[rows, p * slab + LANES:(p + 1) * slab] = bias[:, p * LANES:(p + 1) * LANES].astype(BF16)

    rowi = lax.broadcasted_iota(jnp.int32, (LANES, tq), 0)
    crow = crow_ref[...]
    zero_q = jnp.zeros((LANES, tq), BF16)
    for p in range(FOX_HEADS // HEAD_PAIR):
        qtp = qt_ref[p * LANES:(p + 1) * LANES, :]
        for hh in range(HEAD_PAIR):
            own = (rowi >= hh * HEAD_DIM) & (rowi < (hh + 1) * HEAD_DIM)
            chi, cmid, clo = _split3(crow[2 * p + hh:2 * p + hh + 1, :] * LOG2E)
            br = jnp.where(rowi == _BIAS_ONES, chi,
                           jnp.where(rowi == _BIAS_ONES + 1, cmid, jnp.where(rowi == _BIAS_ONES + 2, clo, 0.0)))
            br = jnp.where((rowi >= 3 * hh) & (rowi < 3 * hh + 3), -1.0, br)
            qa_scr[2 * p + hh, :LANES, :] = jnp.where(own, qtp, zero_q)
            qa_scr[2 * p + hh, LANES:, :] = br.astype(BF16)

    m_scr[...] = jnp.full(m_scr.shape, NEG_INF, F32)
    l_scr[...] = jnp.zeros(l_scr.shape, F32)
    acc_scr[...] = jnp.zeros(acc_scr.shape, F32)

    def scores_to(buf_ref, j):
        off = pl.multiple_of(j * tk, tk)
        for h in range(FOX_HEADS):
            ka = kaug_scr[pl.ds(off, tk), (h // HEAD_PAIR) * slab:(h // HEAD_PAIR + 1) * slab]
            buf_ref[h] = _dot(ka, qa_scr[h])

    ones_rows = jnp.ones((BF16_SUBLANES, tk), BF16)

    def absorb_from(buf_ref, j, mask, cols=slice(None)):
        vt = vt_ref[j]
        for h in range(FOX_HEADS):
            feat = slice(h * HEAD_DIM, (h + 1) * HEAD_DIM)
            s = buf_ref[h, :, cols]
            if mask is not None:
                s = jnp.where(mask, s, NEG_INF)
            m = m_scr[h:h + 1, cols]
            m_new = jnp.maximum(m, jnp.max(s, axis=0, keepdims=True))
            alpha = jnp.exp2(m - m_new)
            e = jnp.exp2(s - m_new).astype(BF16)
            m_scr[h:h + 1, cols] = m_new
            pv = _dot(jnp.concatenate([vt[feat, :], ones_rows], axis=0), e)
            l_scr[h:h + 1, cols] = alpha * l_scr[h:h + 1, cols] + pv[HEAD_DIM:HEAD_DIM + 1, :]
            acc_scr[feat, cols] = alpha * acc_scr[feat, cols] + pv[:HEAD_DIM, :]

    scores_to(sa_scr, 0)

    def step(jj, _):
        t = 2 * jj
        scores_to(sb_scr, t + 1)
        absorb_from(sa_scr, t, None)
        scores_to(sa_scr, t + 2)
        absorb_from(sb_scr, t + 1, None)
        return 0

    lax.fori_loop(0, i, step, 0)
    scores_to(sb_scr, 2 * i + 1)
    causal = lambda n: lax.broadcasted_iota(jnp.int32, (tk, n), 0) <= lax.broadcasted_iota(jnp.int32, (tk, n), 1)
    absorb_from(sa_scr, 2 * i, causal(tq))
    absorb_from(sb_scr, 2 * i + 1, causal(tq - tk), cols=slice(tk, tq))
    fox_t = jnp.concatenate([acc_scr[h * HEAD_DIM:(h + 1) * HEAD_DIM, :] / l_scr[h:h + 1, :]
                             for h in range(FOX_HEADS)], axis=0)
    fox = fox_t.T
    _mix_out(fox, rest_ref[...], x_ref[...], wout_ref, gfox_ref, gpost_ref, gffn_ref, o_ref, h2_ref)


def _mix_weights(lw):
    return (lw["w_out"], lw["g_fox"], lw["g_post_mix"], lw["g_pre_ffn"])


def _fox_prompt(qt, kb, vt, c_row, rest, x, lw, layer):
    nb, s, _ = x.shape
    tq, tk = ATTN_Q_ROWS, ATTN_K_ROWS
    assert tq == 2 * tk and vt.shape[-1] == tk
    nq = s // tq
    qrow = lambda w: pl.BlockSpec((None, tq, w), lambda b, i: (b, i, 0))
    qcol = lambda r: pl.BlockSpec((None, r, tq), lambda b, i: (b, 0, i))
    full = lambda a: pl.BlockSpec((None,) + a.shape[1:], lambda b, i: (b,) + (0,) * (a.ndim - 1))
    return pl.pallas_call(
        _fox_prompt_kernel,
        grid=(nb, nq),
        in_specs=[qcol(FOX_WIDTH), full(kb), full(vt), full(c_row), qcol(FOX_HEADS),
                  qrow(SGU_WIDTH + MEM_WIDTH), qrow(D_MODEL)] + [_layer_spec(a, layer) for a in _mix_weights(lw)],
        out_specs=[qrow(D_MODEL), qrow(D_MODEL)],
        out_shape=[jax.ShapeDtypeStruct(x.shape, F32), jax.ShapeDtypeStruct(x.shape, BF16)],
        scratch_shapes=[pltpu.VMEM((s, 2 * FOX_WIDTH), BF16),
                        pltpu.VMEM((FOX_HEADS, 2 * LANES, tq), BF16),
                        pltpu.VMEM((FOX_HEADS, tk, tq), F32), pltpu.VMEM((FOX_HEADS, tk, tq), F32),
                        pltpu.VMEM((FOX_HEADS, tq), F32), pltpu.VMEM((FOX_HEADS, tq), F32),
                        pltpu.VMEM((FOX_WIDTH, tq), F32)],
        compiler_params=_params("parallel", "arbitrary"),
        name="fox_mix_prompt",
    )(qt, kb, vt, c_row, c_row, rest, x, *_mix_weights(lw))


def _fox_sample_kernel(q_ref, kn_ref, vn_ref, hk_ref, hv_ref, hrow_ref, ncol_ref, nrow_ref, rest_ref, x_ref,
                       wout_ref, gfox_ref, gpost_ref, gffn_ref, o_ref, h2_ref, fox_scr):
    t = q_ref.shape[0]
    b = pl.program_id(0)
    rows = FOX_HEADS * t
    qt = jnp.concatenate([q_ref[...]] * FOX_HEADS, axis=0)
    row_head = lax.broadcasted_iota(jnp.int32, (rows, FOX_WIDTH), 0) // t
    lane_head = lax.broadcasted_iota(jnp.int32, (rows, FOX_WIDTH), 1) // HEAD_DIM
    own = row_head == lane_head
    qb = jnp.where(own, qt, jnp.zeros_like(qt))
    per_head = lambda a: jnp.concatenate(
        [jnp.broadcast_to(a[h:h + 1, :], (t, a.shape[1])) for h in range(FOX_HEADS)], axis=0)
    bq = ncol_ref[...]
    s_hist = _dot(qb, hk_ref[...].astype(BF16)) + bq - per_head(hrow_ref[...])
    s_new = _dot_nt(qb, kn_ref[...]) + bq - per_head(nrow_ref[:, :t])
    causal = (lax.broadcasted_iota(jnp.int32, (rows, t), 1)
              <= lax.broadcasted_iota(jnp.int32, (rows, t), 0) % t)
    s_new = jnp.where(causal, s_new, NEG_INF)
    m = jnp.maximum(jnp.max(s_hist, axis=-1, keepdims=True), jnp.max(s_new, axis=-1, keepdims=True))
    e_hist = jnp.exp(s_hist - m)
    e_new = jnp.exp(s_new - m)
    l = jnp.sum(e_hist, axis=-1, keepdims=True) + jnp.sum(e_new, axis=-1, keepdims=True)
    o = (_dot_nt(e_hist.astype(BF16), hv_ref[...].astype(BF16)) + _dot(e_new.astype(BF16), vn_ref[...])) / l
    o = jnp.where(own, o, 0.0)
    fox = o[0:t, :]
    for h in range(1, FOX_HEADS):
        fox = fox + o[h * t:(h + 1) * t, :]
    fox_scr[pl.ds(pl.multiple_of(b * t, t), t), :] = fox

    @pl.when(b == pl.num_programs(0) - 1)
    def _():
        _mix_out(fox_scr[...], rest_ref[...], x_ref[...], wout_ref, gfox_ref, gpost_ref, gffn_ref, o_ref, h2_ref)


def _fox_sample(q, kb, vb, hk, hv, layer, hist_row, new_col, new_row, rest, x, lw):
    nb, t, _ = q.shape
    per_b = lambda a: pl.BlockSpec((None,) + a.shape[1:], lambda b: (b,) + (0,) * (a.ndim - 1))
    cache = pl.BlockSpec((None, None) + hk.shape[2:], lambda b: (layer, b, 0, 0))
    const = lambda a: pl.BlockSpec(a.shape, lambda b: (0,) * a.ndim)
    return pl.pallas_call(
        _fox_sample_kernel,
        grid=(nb,),
        in_specs=[per_b(q), per_b(kb), per_b(vb), cache, cache, per_b(hist_row), per_b(new_col),
                  per_b(new_row), const(rest), const(x)] + [_layer_spec(a, layer) for a in _mix_weights(lw)],
        out_specs=[const(x), const(x)],
        out_shape=[jax.ShapeDtypeStruct(x.shape, F32), jax.ShapeDtypeStruct(x.shape, BF16)],
        scratch_shapes=[pltpu.VMEM((nb * t, FOX_WIDTH), F32)],
        compiler_params=_params("arbitrary"),
        name="fox_mix_sample",
    )(q, kb, vb, hk, hv, hist_row, new_col, new_row, rest, x, *_mix_weights(lw))


def _ffn_body(x_ref, h2_ref, wup_ref, wdw_ref, bdw_ref, wd_ref, gpost_ref, o_ref, g_scr, shifted, emit):
    h2 = h2_ref[...]
    nf = FFN_DIM // FFN_COLS

    def up(c):
        conv_cols = slice(c * FFN_COLS, (c + 1) * FFN_COLS)
        lin_cols = slice(FFN_DIM + c * FFN_COLS, FFN_DIM + (c + 1) * FFN_COLS)
        return _dot(h2, wup_ref[:, conv_cols]), _dot(h2, wup_ref[:, lin_cols])

    nxt = up(0)
    for c in range(nf):
        a, lin = nxt
        if c + 1 < nf:
            nxt = up(c + 1)
        cols = slice(c * FFN_COLS, (c + 1) * FFN_COLS)
        a1, a2 = shifted(a, c)
        conv = bdw_ref[:, cols] + wdw_ref[0:1, cols] * a2
        conv = conv + wdw_ref[1:2, cols] * a1
        conv = conv + wdw_ref[2:3, cols] * a
        g_scr[:, cols] = (jax.nn.silu(conv) * lin).astype(BF16)
        emit(a, c)
    o_ref[...] = x_ref[...] + _rms(_dot(g_scr[...], wd_ref[...]), gpost_ref[...])


def _ffn_prompt_kernel(x_ref, h2_ref, wup_ref, wdw_ref, bdw_ref, wd_ref, gpost_ref, hist_ref,
                       o_ref, tail_ref, g_scr, carry_scr, work_scr):
    tm = x_ref.shape[0]
    head = SUBLANES

    @pl.when(pl.program_id(1) == 0)
    def _():
        carry_scr[0:head - (CONV_WIDTH - 1), :] = jnp.zeros((head - (CONV_WIDTH - 1), FFN_DIM), F32)
        carry_scr[head - (CONV_WIDTH - 1):head, :] = hist_ref[...]

    def shifted(a, c):
        cols = slice(c * FFN_COLS, (c + 1) * FFN_COLS)
        work = work_scr.at[c % 2]
        work[0:head, :] = carry_scr[:, cols]
        work[head:head + tm, :] = a
        return work[head - 1:head - 1 + tm, :], work[head - 2:head - 2 + tm, :]

    def emit(a, c):
        cols = slice(c * FFN_COLS, (c + 1) * FFN_COLS)
        carry_scr[:, cols] = a[tm - head:, :]
        tail_ref[:, cols] = a[tm - head:, :]

    _ffn_body(x_ref, h2_ref, wup_ref, wdw_ref, bdw_ref, wd_ref, gpost_ref, o_ref, g_scr, shifted, emit)


def _ffn_sample_kernel(x_ref, h2_ref, wup_ref, wdw_ref, bdw_ref, wd_ref, gpost_ref, e1_ref, e2_ref,
                       o_ref, a_ref, g_scr, work_scr, *, seg):
    tm = x_ref.shape[0]
    head = SUBLANES
    rmod = lax.broadcasted_iota(jnp.int32, (tm, FFN_COLS), 0) % seg

    def shifted(a, c):
        cols = slice(c * FFN_COLS, (c + 1) * FFN_COLS)
        work = work_scr.at[c % 2]
        work[0:head, :] = jnp.zeros((head, FFN_COLS), F32)
        work[head:head + tm, :] = a
        a1 = jnp.where(rmod >= 1, work[head - 1:head - 1 + tm, :], e1_ref[:, cols])
        a2 = jnp.where(rmod >= 2, work[head - 2:head - 2 + tm, :], e2_ref[:, cols])
        return a1, a2

    def emit(a, c):
        a_ref[:, c * FFN_COLS:(c + 1) * FFN_COLS] = a

    _ffn_body(x_ref, h2_ref, wup_ref, wdw_ref, bdw_ref, wd_ref, gpost_ref, o_ref, g_scr, shifted, emit)


def _ffn_weights(lw):
    return (lw["w_up"], lw["w_dwconv"], lw["b_dwconv"], lw["w_down"], lw["g_post_ffn"])


def _ffn_weight_specs(lw, layer):
    return [_layer_spec(a, layer, pipeline_mode=pl.Buffered(1)) for a in _ffn_weights(lw)]


def _ffn_prompt(x, h2, hist, lw, layer):
    nb, s, _ = x.shape
    tm = FFN_ROWS
    nt = s // tm
    row = pl.BlockSpec((None, tm, D_MODEL), lambda b, i: (b, i, 0))
    return pl.pallas_call(
        _ffn_prompt_kernel,
        grid=(nb, nt),
        in_specs=[row, row] + _ffn_weight_specs(lw, layer) + [
            pl.BlockSpec((None, CONV_WIDTH - 1, FFN_DIM), lambda b, i: (b, 0, 0))],
        out_specs=[pl.BlockSpec((None, tm, D_MODEL), lambda b, i: (b, i, 0)),
                   pl.BlockSpec((None, None, SUBLANES, FFN_DIM), lambda b, i: (b, i, 0, 0))],
        out_shape=[jax.ShapeDtypeStruct(x.shape, F32),
                   jax.ShapeDtypeStruct((nb, nt, SUBLANES, FFN_DIM), F32)],
        scratch_shapes=[pltpu.VMEM((tm, FFN_DIM), BF16),
                        pltpu.VMEM((SUBLANES, FFN_DIM), F32),
                        pltpu.VMEM((2, tm + SUBLANES, FFN_COLS), F32)],
        compiler_params=_params("parallel", "arbitrary"),
        name="conv_ffn_prompt",
    )(x, h2, *_ffn_weights(lw), hist)


def _ffn_sample(x, h2, e1, e2, lw, layer, seg):
    rows, _ = x.shape
    whole = lambda w: pl.BlockSpec((rows, w), lambda i: (0, 0))
    return pl.pallas_call(
        functools.partial(_ffn_sample_kernel, seg=seg),
        grid=(1,),
        in_specs=[whole(D_MODEL), whole(D_MODEL)] + _ffn_weight_specs(lw, layer) + [whole(FFN_DIM), whole(FFN_DIM)],
        out_specs=[whole(D_MODEL), whole(FFN_DIM)],
        out_shape=[jax.ShapeDtypeStruct(x.shape, F32), jax.ShapeDtypeStruct((rows, FFN_DIM), F32)],
        scratch_shapes=[pltpu.VMEM((rows, FFN_DIM), BF16),
                        pltpu.VMEM((2, rows + SUBLANES, FFN_COLS), F32)],
        compiler_params=_params("arbitrary"),
        name="conv_ffn_sample",
    )(x, h2, *_ffn_weights(lw), e1, e2)


def _prepare_weights(g_pre_mix, w_in, b_forget, w_spatial, b_spatial, g_sgu, g_group_out, w_out, g_post_mix,
                     g_pre_ffn, w_up, w_dwconv, b_dwconv, w_down, g_post_ffn, dec_seq):
    depth = w_in.shape[0]
    row = lambda a: a.reshape(depth, 1, -1)
    c0 = 3 * FOX_WIDTH
    c1 = c0 + FOX_HEADS
    c2 = c1 + 2 * SGU_WIDTH
    w_b = jnp.concatenate([w_in[:, :, c1:c2], w_in[:, :, c2:], w_in[:, :, c0:c1],
                           jnp.zeros((depth, D_MODEL, B_COLS - (2 * SGU_WIDTH + MEM_WIDTH + FOX_HEADS)), F32)],
                          axis=2)
    group_dim = SGU_WIDTH // SGU_GROUPS
    ws_prompt = jnp.concatenate([w_spatial[:, g] for g in range(SGU_GROUPS)], axis=2)
    bt_prompt = jnp.repeat(jnp.swapaxes(b_spatial, 1, 2), group_dim, axis=2)
    reps = GMLP_CHUNK // dec_seq
    blk = (jnp.arange(GMLP_CHUNK)[:, None] // dec_seq) == (jnp.arange(GMLP_CHUNK)[None, :] // dec_seq)
    ws_sample = jnp.concatenate(
        [jnp.where(blk, jnp.tile(w_spatial[:, g, :dec_seq, :dec_seq], (1, reps, reps)), 0.0)
         for g in range(SGU_GROUPS)], axis=2)
    bt_sample = jnp.tile(jnp.repeat(jnp.swapaxes(b_spatial[:, :, :dec_seq], 1, 2), group_dim, axis=2),
                         (1, reps, 1))
    return {
        "g_pre_mix": row(g_pre_mix), "w_qkv": w_in[:, :, :c0].astype(BF16), "w_b": w_b.astype(BF16),
        "b_forget": jnp.pad(row(b_forget), ((0, 0), (0, 0), (0, LANES - FOX_HEADS))), "g_sgu": row(g_sgu),
        "g_fox": row(g_group_out[:, :FOX_WIDTH]), "g_rest": row(g_group_out[:, FOX_WIDTH:]),
        "w_out": w_out.astype(BF16), "g_post_mix": row(g_post_mix), "g_pre_ffn": row(g_pre_ffn),
        "w_up": w_up.astype(BF16), "w_dwconv": w_dwconv, "b_dwconv": row(b_dwconv),
        "w_down": w_down.astype(BF16), "g_post_ffn": row(g_post_ffn),
        "ws_prompt": ws_prompt, "bt_prompt": bt_prompt, "ws_sample": ws_sample, "bt_sample": bt_sample,
    }


def kernel(x_prompt, x_sample, mem_prompt, cache_fox_k, cache_fox_v, cache_fox_logf, cache_mem_k, cache_mem_v,
           cache_ffn_conv, g_pre_mix, w_in, b_forget, w_spatial, b_spatial, g_sgu, g_mem, w_mem_kv, g_group_out,
           w_out, g_post_mix, g_pre_ffn, w_up, w_dwconv, b_dwconv, w_down, g_post_ffn):
    depth = w_in.shape[0]
    batch, seq, _ = x_prompt.shape
    dec_batch, dec_seq, _ = x_sample.shape
    past = cache_fox_k.shape[2]
    dec_rows = dec_batch * dec_seq

    mem_k_all, mem_v_all = _memory_kv(mem_prompt, g_mem, w_mem_kv.astype(BF16))
    feature_major = lambda a: jnp.transpose(a, (0, 1, 3, 4, 2)).reshape(a.shape[0], a.shape[1], -1, a.shape[2])
    token_major = lambda a, heads: jnp.transpose(
        a.reshape(a.shape[0], a.shape[1], heads, HEAD_DIM, a.shape[3]), (0, 1, 4, 2, 3))
    flat_mem = lambda a: a.reshape(-1, MEM_WIDTH, N_MEM)
    pmk, pmv = flat_mem(mem_k_all), flat_mem(mem_v_all)
    smk, smv = flat_mem(feature_major(cache_mem_k)), flat_mem(feature_major(cache_mem_v))
    hk = feature_major(cache_fox_k)
    hv = feature_major(cache_fox_v)

    lw = _prepare_weights(g_pre_mix, w_in, b_forget, w_spatial, b_spatial, g_sgu, g_group_out, w_out, g_post_mix,
                          g_pre_ffn, w_up, w_dwconv, b_dwconv, w_down, g_post_ffn, dec_seq)
    hist_rows = _lane_cumsum(jnp.swapaxes(cache_fox_logf, 2, 3).reshape(depth * dec_batch * FOX_HEADS, past),
                             minus_total=True).reshape(depth, dec_batch, FOX_HEADS, past)
    pad_rows = lambda a: jnp.pad(a, ((0, 0), (0, 0), (0, dec_seq - a.shape[2]), (0, 0))
                                 ).reshape(depth, dec_rows, FFN_DIM)
    e1_all = pad_rows(cache_ffn_conv[:, :, 1:2, :])
    e2_all = pad_rows(cache_ffn_conv)
    zeros_hist = jnp.zeros((batch, CONV_WIDTH - 1, FFN_DIM), F32)

    yp = x_prompt
    ys = x_sample.reshape(1, dec_rows, D_MODEL)
    outs = {name: [] for name in ("logf_p", "conv_p", "k_s", "v_s", "logf_s", "gv_s", "conv_s")}
    stacked_kv = ()
    per_b = lambda a: a.reshape(dec_batch, dec_seq, a.shape[-1])
    for l in range(depth):
        qt, k_all, v_all, kb, vt, logf, rest = _in_proj(yp, pmk, pmv, l * batch, 1, lw, IN_PROJ_ROWS,
                                                        lw["ws_prompt"], lw["bt_prompt"], True,
                                                        layer=l, depth=depth, stacked_kv=stacked_kv)
        stacked_kv = (k_all, v_all)
        c_row = _lane_cumsum(logf.reshape(batch * FOX_HEADS, seq)).reshape(batch, FOX_HEADS, seq)
        yp, h2 = _fox_prompt(qt, kb, vt, c_row, rest, yp, lw, l)
        yp, tail = _ffn_prompt(yp, h2, zeros_hist, lw, l)
        outs["logf_p"].append(logf)
        outs["conv_p"].append(tail[:, -1, SUBLANES - (CONV_WIDTH - 1):, :])

        q, k, v, kb, vb, logf, rest, vrows = _in_proj(ys, smk, smv, l * dec_batch, dec_batch, lw, dec_rows,
                                                      lw["ws_sample"], lw["bt_sample"], False,
                                                      layer=l, depth=depth)
        logf_new_row = jnp.swapaxes(per_b(logf), 1, 2).reshape(dec_batch * FOX_HEADS, dec_seq)
        new_row = _lane_cumsum(jnp.pad(logf_new_row, ((0, 0), (0, LANES - dec_seq)))
                               ).reshape(dec_batch, FOX_HEADS, LANES)
        new_col = new_row[:, :, :dec_seq].reshape(dec_batch, FOX_HEADS * dec_seq, 1)
        ys2, h2s = _fox_sample(per_b(q[0]), per_b(kb[0]), per_b(vb[0]), hk, hv, l, hist_rows[l], new_col, new_row,
                               rest[0], ys[0], lw)
        ys_flat, a_all = _ffn_sample(ys2, h2s, e1_all[l], e2_all[l], lw, l, dec_seq)
        ys = ys_flat.reshape(1, dec_rows, D_MODEL)
        outs["k_s"].append(per_b(k[0]).reshape(dec_batch, dec_seq, FOX_HEADS, HEAD_DIM))
        outs["v_s"].append(per_b(v[0]).reshape(dec_batch, dec_seq, FOX_HEADS, HEAD_DIM))
        outs["logf_s"].append(per_b(logf[0]))
        outs["gv_s"].append(per_b(vrows[0]))
        outs["conv_s"].append(a_all.reshape(dec_batch, dec_seq, FFN_DIM)[:, dec_seq - (CONV_WIDTH - 1):, :])

    st = {name: jnp.stack(vals) for name, vals in outs.items()}
    return (yp, ys.reshape(dec_batch, dec_seq, D_MODEL),
            token_major(stacked_kv[0], FOX_HEADS), token_major(stacked_kv[1], FOX_HEADS),
            jnp.swapaxes(st["logf_p"], 2, 3),
            token_major(mem_k_all, MEM_HEADS), token_major(mem_v_all, MEM_HEADS),
            st["conv_p"], st["k_s"], st["v_s"], st["logf_s"], st["gv_s"], st["conv_s"])
```

```python
import functools

import jax
import jax.numpy as jnp
from jax import lax
from jax.experimental import pallas as pl
from jax.experimental.pallas import tpu as pltpu

D_MODEL = 1024
HEAD_DIM = 64
FOX_WIDTH = 512
FOX_HEADS = 8
SGU_WIDTH = 256
SGU_GROUPS = 4
GMLP_CHUNK = 128
CHUNK = 64
MEM_WIDTH = 256
MEM_HEADS = 4
N_MEM = 256
FFN_DIM = 2816
CONV_WIDTH = 3
RMS_EPS = 1e-6
NEG_INF = -1e30
QK_SCALE = HEAD_DIM ** -0.5
LOG2E = 1.4426950408889634

LANES = 128
SUBLANES = 8
BF16_SUBLANES = 2 * SUBLANES
HEAD_PAIR = LANES // HEAD_DIM
VMEM_LIMIT_BYTES = 56 * 1024 * 1024

IN_PROJ_ROWS = 1024
ATTN_Q_ROWS = 512
ATTN_K_ROWS = 256
FFN_ROWS = 1024
FFN_COLS = 256
B_COLS = 896

BF16 = jnp.bfloat16
F32 = jnp.float32


def _rms(x, g):
    y = x * lax.rsqrt(jnp.mean(x * x, axis=-1, keepdims=True) + RMS_EPS)
    return y * g


def _dot(a, b):
    return jnp.dot(a, b, preferred_element_type=F32)


def _dot_nt(a, b):
    return lax.dot_general(a, b, (((1,), (1,)), ((), ())), preferred_element_type=F32)


def _params(*semantics):
    return pltpu.CompilerParams(dimension_semantics=semantics, vmem_limit_bytes=VMEM_LIMIT_BYTES)


def _layer_spec(a, layer, **kwargs):
    return pl.BlockSpec((None,) + a.shape[1:], lambda *_: (layer,) + (0,) * (a.ndim - 1), **kwargs)


def _memkv_kernel(mem_ref, g_ref, w_ref, mk_ref, mv_ref):
    h = _rms(mem_ref[...], g_ref[...]).astype(BF16)
    kv = _dot(h, w_ref[...])
    mk_ref[...] = kv[:, :MEM_WIDTH].T
    mv_ref[...] = kv[:, MEM_WIDTH:].T


def _memory_kv(mem, g_mem, w_mem_kv_bf):
    depth = g_mem.shape[0]
    batch = mem.shape[0]
    out = jax.ShapeDtypeStruct((depth, batch, MEM_WIDTH, N_MEM), F32)
    return pl.pallas_call(
        _memkv_kernel,
        grid=(depth, batch),
        in_specs=[
            pl.BlockSpec((None, N_MEM, D_MODEL), lambda l, b: (b, 0, 0)),
            pl.BlockSpec((None, 1, D_MODEL), lambda l, b: (l, 0, 0)),
            pl.BlockSpec((None, D_MODEL, 2 * MEM_WIDTH), lambda l, b: (l, 0, 0)),
        ],
        out_specs=[
            pl.BlockSpec((None, None, MEM_WIDTH, N_MEM), lambda l, b: (l, b, 0, 0)),
            pl.BlockSpec((None, None, MEM_WIDTH, N_MEM), lambda l, b: (l, b, 0, 0)),
        ],
        out_shape=[out, out],
        compiler_params=_params("arbitrary", "arbitrary"),
        name="memory_kv",
    )(mem, g_mem.reshape(depth, 1, D_MODEL), w_mem_kv_bf)


def _inproj_kernel(x_ref, gpre_ref, wqkv_ref, wb_ref, bfg_ref, ws_ref, bt_ref, gsgu_ref, gg_ref,
                   mk_ref, mv_ref, *out_refs, transposed, n_stack_in):
    out_refs = out_refs[n_stack_in:]
    tm = x_ref.shape[0]
    n_mem_blocks = mk_ref.shape[0]
    if transposed:
        qt_ref, k_ref, v_ref, kb_ref, vt_ref, lf_ref, rest_ref = out_refs
        vrows_ref = None
    else:
        q_ref, k_ref, v_ref, kb_ref, vb_ref, lf_ref, rest_ref, vrows_ref = out_refs
    h = _rms(x_ref[...], gpre_ref[...]).astype(BF16)

    yb = _dot(h, wb_ref[...])
    fg = yb[:, 2 * SGU_WIDTH + MEM_WIDTH:2 * SGU_WIDTH + MEM_WIDTH + LANES]
    lf = jax.nn.log_sigmoid(fg + bfg_ref[...])
    if transposed:
        lf_ref[...] = lf.T[:FOX_HEADS, :]
    else:
        lf_ref[...] = lf[:, :FOX_HEADS]

    qmb = (yb[:, 2 * SGU_WIDTH:2 * SGU_WIDTH + MEM_WIDTH] * QK_SCALE).astype(BF16)
    rows = tm // n_mem_blocks
    hlane = lax.broadcasted_iota(jnp.int32, (rows, MEM_WIDTH), 1) // HEAD_DIM
    zero_q = jnp.zeros((rows, MEM_WIDTH), BF16)
    mem_scores = []
    for bi in range(n_mem_blocks):
        qb = qmb[bi * rows:(bi + 1) * rows, :]
        mkt = mk_ref[bi].astype(BF16)
        mem_scores.append([_dot(jnp.where(hlane == hd, qb, zero_q), mkt) for hd in range(MEM_HEADS)])

    q = _dot(h, wqkv_ref[:, :FOX_WIDTH]) * (QK_SCALE * LOG2E if transposed else QK_SCALE)

    z = jax.nn.gelu(yb[:, :2 * SGU_WIDTH])
    u = z[:, :SGU_WIDTH]
    vv = _rms(z[:, SGU_WIDTH:], gsgu_ref[...])
    if vrows_ref is not None:
        vrows_ref[...] = vv
    vvb = vv.astype(BF16)
    mem_exp = [[jnp.exp(s - jnp.max(s, axis=-1, keepdims=True)) for s in per_block] for per_block in mem_scores]

    wrow = lax.broadcasted_iota(jnp.int32, (GMLP_CHUNK, SGU_GROUPS * GMLP_CHUNK), 0)
    wcol = lax.broadcasted_iota(jnp.int32, (GMLP_CHUNK, SGU_GROUPS * GMLP_CHUNK), 1)
    wmask = ((wcol % GMLP_CHUNK) // CHUNK) <= (wrow // CHUNK)
    wcat = jnp.where(wmask, ws_ref[...], 0.0).astype(BF16)
    glane = lax.broadcasted_iota(jnp.int32, (GMLP_CHUNK, SGU_WIDTH), 1) // (SGU_WIDTH // SGU_GROUPS)
    zero_chunk = jnp.zeros((GMLP_CHUNK, SGU_WIDTH), BF16)
    mixed_parts = []
    for c in range(tm // GMLP_CHUNK):
        vc = vvb[c * GMLP_CHUNK:(c + 1) * GMLP_CHUNK, :]
        rhs = jnp.concatenate([jnp.where(glane == g, vc, zero_chunk) for g in range(SGU_GROUPS)], axis=0)
        mixed_parts.append(_dot(wcat, rhs))
    mem_pv = []
    for bi in range(n_mem_blocks):
        mvt = mv_ref[bi].astype(BF16)
        mem_pv.append([_dot_nt(e.astype(BF16), mvt) for e in mem_exp[bi]])

    k = _dot(h, wqkv_ref[:, FOX_WIDTH:2 * FOX_WIDTH])

    if transposed:
        qt_ref[...] = q.T.astype(BF16)
    else:
        q_ref[...] = q.astype(BF16)
    sgu = jnp.concatenate([u[c * GMLP_CHUNK:(c + 1) * GMLP_CHUNK, :] * (mixed + bt_ref[...])
                           for c, mixed in enumerate(mixed_parts)], axis=0)
    rest_ref[:, :SGU_WIDTH] = _rms(sgu, gg_ref[:, :SGU_WIDTH]).astype(BF16)
    mem_parts = []
    for bi in range(n_mem_blocks):
        out = jnp.zeros((rows, MEM_WIDTH), F32)
        for hd in range(MEM_HEADS):
            o = mem_pv[bi][hd] / jnp.sum(mem_exp[bi][hd], axis=-1, keepdims=True)
            out = jnp.where(hlane == hd, o, out)
        mem_parts.append(out)
    mem = mem_parts[0] if n_mem_blocks == 1 else jnp.concatenate(mem_parts, axis=0)
    rest_ref[:, SGU_WIDTH:] = _rms(mem, gg_ref[:, SGU_WIDTH:]).astype(BF16)

    v = _dot(h, wqkv_ref[:, 2 * FOX_WIDTH:])

    kb_ref[...] = k.astype(BF16)
    if transposed:
        k_ref[...] = k.T
        vt = v.T
        v_ref[...] = vt
        vtb = vt.astype(BF16)
        tk = vt_ref.shape[-1]
        for c in range(tm // tk):
            vt_ref[c] = vtb[:, c * tk:(c + 1) * tk]
    else:
        k_ref[...] = k
        v_ref[...] = v
        vb_ref[...] = v.astype(BF16)


def _in_proj(x, mem_k, mem_v, mem_first, mem_per_tile, lw, tm, spatial_w, spatial_b, transposed,
             layer, depth, stacked_kv=()):
    nb, t, _ = x.shape
    nt = t // tm
    if transposed:
        assert mem_per_tile == 1
        mem_index = lambda b, i: (mem_first + b, 0, 0)
    else:
        mem_index = lambda b, i: (mem_first // mem_per_tile + b * nt + i, 0, 0)
    row = lambda w: pl.BlockSpec((None, tm, w), lambda b, i: (b, i, 0))
    memspec = pl.BlockSpec((mem_per_tile, MEM_WIDTH, N_MEM), mem_index)
    f32 = lambda w: jax.ShapeDtypeStruct((nb, t, w), F32)
    bf = lambda w: jax.ShapeDtypeStruct((nb, t, w), BF16)
    consts = (lw["g_pre_mix"], lw["w_qkv"], lw["w_b"], lw["b_forget"], spatial_w, spatial_b,
              lw["g_sgu"], lw["g_rest"])
    if transposed:
        tk = ATTN_K_ROWS
        stack_spec = pl.BlockSpec((None, None, FOX_WIDTH, tm), lambda b, i: (layer, b, 0, i))
        stack_shape = jax.ShapeDtypeStruct((depth, nb, FOX_WIDTH, t), F32)
        out_specs = [pl.BlockSpec((None, FOX_WIDTH, tm), lambda b, i: (b, 0, i)),
                     stack_spec, stack_spec, row(FOX_WIDTH),
                     pl.BlockSpec((None, tm // tk, FOX_WIDTH, tk), lambda b, i: (b, i, 0, 0)),
                     pl.BlockSpec((None, FOX_HEADS, tm), lambda b, i: (b, 0, i)), row(SGU_WIDTH + MEM_WIDTH)]
        out_shape = [jax.ShapeDtypeStruct((nb, FOX_WIDTH, t), BF16), stack_shape, stack_shape,
                     bf(FOX_WIDTH), jax.ShapeDtypeStruct((nb, t // tk, FOX_WIDTH, tk), BF16),
                     jax.ShapeDtypeStruct((nb, FOX_HEADS, t), F32), bf(SGU_WIDTH + MEM_WIDTH)]
    else:
        out_specs = [row(FOX_WIDTH)] * 5 + [row(FOX_HEADS), row(SGU_WIDTH + MEM_WIDTH), row(SGU_WIDTH)]
        out_shape = [bf(FOX_WIDTH), f32(FOX_WIDTH), f32(FOX_WIDTH), bf(FOX_WIDTH), bf(FOX_WIDTH),
                     f32(FOX_HEADS), bf(SGU_WIDTH + MEM_WIDTH), f32(SGU_WIDTH)]
    in_specs = [row(D_MODEL)] + [_layer_spec(a, layer) for a in consts] + [memspec, memspec]
    aliases = {len(in_specs) + n: 1 + n for n in range(len(stacked_kv))}
    in_specs += [pl.BlockSpec(memory_space=pl.ANY)] * len(stacked_kv)
    return pl.pallas_call(
        functools.partial(_inproj_kernel, transposed=transposed, n_stack_in=len(stacked_kv)),
        grid=(nb, nt),
        in_specs=in_specs,
        out_specs=out_specs,
        out_shape=out_shape,
        input_output_aliases=aliases,
        compiler_params=_params("parallel", "arbitrary"),
        name="in_proj",
    )(x, *consts, mem_k, mem_v, *stacked_kv)


def _cumsum_kernel(x_ref, o_ref, *, minus_total):
    x = x_ref[...]
    n = x.shape[-1]
    lane = lax.broadcasted_iota(jnp.int32, x.shape, 1)
    shift = 1
    while shift < n:
        x = x + jnp.where(lane >= shift, pltpu.roll(x, shift, 1), 0.0)
        shift *= 2
    if minus_total:
        x = x - x[:, n - 1:n]
    o_ref[...] = x


def _lane_cumsum(x, minus_total=False):
    return pl.pallas_call(
        functools.partial(_cumsum_kernel, minus_total=minus_total),
        out_shape=jax.ShapeDtypeStruct(x.shape, F32),
        name="lane_cumsum",
    )(x)


def _mix_out(fox, rest, x, wout_ref, gfox_ref, gpost_ref, gffn_ref, o_ref, h2_ref):
    fox_n = _rms(fox, gfox_ref[...]).astype(BF16)
    y = _dot(fox_n, wout_ref[:FOX_WIDTH, :]) + _dot(rest, wout_ref[FOX_WIDTH:, :])
    x_mid = x + _rms(y, gpost_ref[...])
    o_ref[...] = x_mid
    h2_ref[...] = _rms(x_mid, gffn_ref[...]).astype(BF16)


def _split3(c):
    hi = c.astype(BF16).astype(F32)
    r = c - hi
    mid = r.astype(BF16).astype(F32)
    lo = (r - mid).astype(BF16).astype(F32)
    return hi, mid, lo


_BIAS_ONES = 6


def _fox_prompt_kernel(qt_ref, kb_ref, vt_ref, call_ref, crow_ref, rest_ref, x_ref, wout_ref, gfox_ref,
                       gpost_ref, gffn_ref, o_ref, h2_ref, kaug_scr, qa_scr, sa_scr, sb_scr, m_scr, l_scr, acc_scr):
    tq = qt_ref.shape[1]
    tk = vt_ref.shape[-1]
    s_len = kb_ref.shape[0]
    slab = 2 * LANES
    i = pl.program_id(1)

    @pl.when(i == 0)
    def _():
        n_slab = FOX_HEADS // HEAD_PAIR
        src = lax.broadcasted_iota(jnp.int32, (LANES, n_slab * LANES), 0)
        dst = lax.broadcasted_iota(jnp.int32, (LANES, n_slab * LANES), 1)
        head, piece = src % FOX_HEADS, src // FOX_HEADS
        piece_dst = (head // HEAD_PAIR) * LANES + 3 * (head % HEAD_PAIR) + piece
        is_piece = (src < 3 * FOX_HEADS) & (dst == piece_dst)
        is_one = (src == 3 * FOX_HEADS) & (dst % LANES >= _BIAS_ONES) & (dst % LANES < _BIAS_ONES + 3)
        place = jnp.where(is_piece | is_one, 1.0, 0.0).astype(BF16)
        pad_row = lax.broadcasted_iota(jnp.int32, (LANES - 3 * FOX_HEADS, tk), 0)
        pad = jnp.where(pad_row == 0, 1.0, 0.0)
        for r in range(s_len // tk):
            rows = slice(r * tk, (r + 1) * tk)
            pieces_t = jnp.concatenate(_split3(call_ref[:, rows] * LOG2E) + (pad,), axis=0)
            bias = _dot(pieces_t.T.astype(BF16), place)
            for p in range(n_slab):
                kaug_scr[rows, p * slab:p * slab + LANES] = kb_ref[rows, p * LANES:(p + 1) * LANES]
                kaug_scr[rows, p * slab + LANES:(p + 1) * slab] = bias[:, p * LANES:(p + 1) * LANES].astype(BF16)

    rowi = lax.broadcasted_iota(jnp.int32, (LANES, tq), 0)
    crow = crow_ref[...]
    zero_q = jnp.zeros((LANES, tq), BF16)
    for p in range(FOX_HEADS // HEAD_PAIR):
        qtp = qt_ref[p * LANES:(p + 1) * LANES, :]
        for hh in range(HEAD_PAIR):
            own = (rowi >= hh * HEAD_DIM) & (rowi < (hh + 1) * HEAD_DIM)
            chi, cmid, clo = _split3(crow[2 * p + hh:2 * p + hh + 1, :] * LOG2E)
            br = jnp.where(rowi == _BIAS_ONES, chi,
                           jnp.where(rowi == _BIAS_ONES + 1, cmid, jnp.where(rowi == _BIAS_ONES + 2, clo, 0.0)))
            br = jnp.where((rowi >= 3 * hh) & (rowi < 3 * hh + 3), -1.0, br)
            qa_scr[2 * p + hh, :LANES, :] = jnp.where(own, qtp, zero_q)
            qa_scr[2 * p + hh, LANES:, :] = br.astype(BF16)

    m_scr[...] = jnp.full(m_scr.shape, NEG_INF, F32)
    l_scr[...] = jnp.zeros(l_scr.shape, F32)
    acc_scr[...] = jnp.zeros(acc_scr.shape, F32)

    def scores_to(buf_ref, j):
        off = pl.multiple_of(j * tk, tk)
        for h in range(FOX_HEADS):
            ka = kaug_scr[pl.ds(off, tk), (h // HEAD_PAIR) * slab:(h // HEAD_PAIR + 1) * slab]
            buf_ref[h] = _dot(ka, qa_scr[h])

    ones_rows = jnp.ones((BF16_SUBLANES, tk), BF16)

    def absorb_from(buf_ref, j, mask, cols=slice(None)):
        vt = vt_ref[j]
        for h in range(FOX_HEADS):
            feat = slice(h * HEAD_DIM, (h + 1) * HEAD_DIM)
            s = buf_ref[h, :, cols]
            if mask is not None:
                s = jnp.where(mask, s, NEG_INF)
            m = m_scr[h:h + 1, cols]
            m_new = jnp.maximum(m, jnp.max(s, axis=0, keepdims=True))
            alpha = jnp.exp2(m - m_new)
            e = jnp.exp2(s - m_new).astype(BF16)
            m_scr[h:h + 1, cols] = m_new
            pv = _dot(jnp.concatenate([vt[feat, :], ones_rows], axis=0), e)
            l_scr[h:h + 1, cols] = alpha * l_scr[h:h + 1, cols] + pv[HEAD_DIM:HEAD_DIM + 1, :]
            acc_scr[feat, cols] = alpha * acc_scr[feat, cols] + pv[:HEAD_DIM, :]

    scores_to(sa_scr, 0)

    def step(jj, _):
        t = 2 * jj
        scores_to(sb_scr, t + 1)
        absorb_from(sa_scr, t, None)
        scores_to(sa_scr, t + 2)
        absorb_from(sb_scr, t + 1, None)
        return 0

    lax.fori_loop(0, i, step, 0)
    scores_to(sb_scr, 2 * i + 1)
    causal = lambda n: lax.broadcasted_iota(jnp.int32, (tk, n), 0) <= lax.broadcasted_iota(jnp.int32, (tk, n), 1)
    absorb_from(sa_scr, 2 * i, causal(tq))
    absorb_from(sb_scr, 2 * i + 1, causal(tq - tk), cols=slice(tk, tq))
    fox_t = jnp.concatenate([acc_scr[h * HEAD_DIM:(h + 1) * HEAD_DIM, :] / l_scr[h:h + 1, :]
                             for h in range(FOX_HEADS)], axis=0)
    fox = fox_t.T
    _mix_out(fox, rest_ref[...], x_ref[...], wout_ref, gfox_ref, gpost_ref, gffn_ref, o_ref, h2_ref)


def _mix_weights(lw):
    return (lw["w_out"], lw["g_fox"], lw["g_post_mix"], lw["g_pre_ffn"])


def _fox_prompt(qt, kb, vt, c_row, rest, x, lw, layer):
    nb, s, _ = x.shape
    tq, tk = ATTN_Q_ROWS, ATTN_K_ROWS
    assert tq == 2 * tk and vt.shape[-1] == tk
    nq = s // tq
    qrow = lambda w: pl.BlockSpec((None, tq, w), lambda b, i: (b, i, 0))
    qcol = lambda r: pl.BlockSpec((None, r, tq), lambda b, i: (b, 0, i))
    full = lambda a: pl.BlockSpec((None,) + a.shape[1:], lambda b, i: (b,) + (0,) * (a.ndim - 1))
    return pl.pallas_call(
        _fox_prompt_kernel,
        grid=(nb, nq),
        in_specs=[qcol(FOX_WIDTH), full(kb), full(vt), full(c_row), qcol(FOX_HEADS),
                  qrow(SGU_WIDTH + MEM_WIDTH), qrow(D_MODEL)] + [_layer_spec(a, layer) for a in _mix_weights(lw)],
        out_specs=[qrow(D_MODEL), qrow(D_MODEL)],
        out_shape=[jax.ShapeDtypeStruct(x.shape, F32), jax.ShapeDtypeStruct(x.shape, BF16)],
        scratch_shapes=[pltpu.VMEM((s, 2 * FOX_WIDTH), BF16),
                        pltpu.VMEM((FOX_HEADS, 2 * LANES, tq), BF16),
                        pltpu.VMEM((FOX_HEADS, tk, tq), F32), pltpu.VMEM((FOX_HEADS, tk, tq), F32),
                        pltpu.VMEM((FOX_HEADS, tq), F32), pltpu.VMEM((FOX_HEADS, tq), F32),
                        pltpu.VMEM((FOX_WIDTH, tq), F32)],
        compiler_params=_params("parallel", "arbitrary"),
        name="fox_mix_prompt",
    )(qt, kb, vt, c_row, c_row, rest, x, *_mix_weights(lw))


def _fox_sample_kernel(q_ref, kn_ref, vn_ref, hk_ref, hv_ref, hrow_ref, ncol_ref, nrow_ref, rest_ref, x_ref,
                       wout_ref, gfox_ref, gpost_ref, gffn_ref, o_ref, h2_ref, fox_scr):
    t = q_ref.shape[0]
    b = pl.program_id(0)
    rows = FOX_HEADS * t
    qt = jnp.concatenate([q_ref[...]] * FOX_HEADS, axis=0)
    row_head = lax.broadcasted_iota(jnp.int32, (rows, FOX_WIDTH), 0) // t
    lane_head = lax.broadcasted_iota(jnp.int32, (rows, FOX_WIDTH), 1) // HEAD_DIM
    own = row_head == lane_head
    qb = jnp.where(own, qt, jnp.zeros_like(qt))
    per_head = lambda a: jnp.concatenate(
        [jnp.broadcast_to(a[h:h + 1, :], (t, a.shape[1])) for h in range(FOX_HEADS)], axis=0)
    bq = ncol_ref[...]
    s_hist = _dot(qb, hk_ref[...].astype(BF16)) + bq - per_head(hrow_ref[...])
    s_new = _dot_nt(qb, kn_ref[...]) + bq - per_head(nrow_ref[:, :t])
    causal = (lax.broadcasted_iota(jnp.int32, (rows, t), 1)
              <= lax.broadcasted_iota(jnp.int32, (rows, t), 0) % t)
    s_new = jnp.where(causal, s_new, NEG_INF)
    m = jnp.maximum(jnp.max(s_hist, axis=-1, keepdims=True), jnp.max(s_new, axis=-1, keepdims=True))
    e_hist = jnp.exp(s_hist - m)
    e_new = jnp.exp(s_new - m)
    l = jnp.sum(e_hist, axis=-1, keepdims=True) + jnp.sum(e_new, axis=-1, keepdims=True)
    o = (_dot_nt(e_hist.astype(BF16), hv_ref[...].astype(BF16)) + _dot(e_new.astype(BF16), vn_ref[...])) / l
    o = jnp.where(own, o, 0.0)
    fox = o[0:t, :]
    for h in range(1, FOX_HEADS):
        fox = fox + o[h * t:(h + 1) * t, :]
    fox_scr[pl.ds(pl.multiple_of(b * t, t), t), :] = fox

    @pl.when(b == pl.num_programs(0) - 1)
    def _():
        _mix_out(fox_scr[...], rest_ref[...], x_ref[...], wout_ref, gfox_ref, gpost_ref, gffn_ref, o_ref, h2_ref)


def _fox_sample(q, kb, vb, hk, hv, layer, hist_row, new_col, new_row, rest, x, lw):
    nb, t, _ = q.shape
    per_b = lambda a: pl.BlockSpec((None,) + a.shape[1:], lambda b: (b,) + (0,) * (a.ndim - 1))
    cache = pl.BlockSpec((None, None) + hk.shape[2:], lambda b: (layer, b, 0, 0))
    const = lambda a: pl.BlockSpec(a.shape, lambda b: (0,) * a.ndim)
    return pl.pallas_call(
        _fox_sample_kernel,
        grid=(nb,),
        in_specs=[per_b(q), per_b(kb), per_b(vb), cache, cache, per_b(hist_row), per_b(new_col),
                  per_b(new_row), const(rest), const(x)] + [_layer_spec(a, layer) for a in _mix_weights(lw)],
        out_specs=[const(x), const(x)],
        out_shape=[jax.ShapeDtypeStruct(x.shape, F32), jax.ShapeDtypeStruct(x.shape, BF16)],
        scratch_shapes=[pltpu.VMEM((nb * t, FOX_WIDTH), F32)],
        compiler_params=_params("arbitrary"),
        name="fox_mix_sample",
    )(q, kb, vb, hk, hv, hist_row, new_col, new_row, rest, x, *_mix_weights(lw))


def _ffn_body(x_ref, h2_ref, wup_ref, wdw_ref, bdw_ref, wd_ref, gpost_ref, o_ref, g_scr, shifted, emit):
    h2 = h2_ref[...]
    nf = FFN_DIM // FFN_COLS

    def up(c):
        conv_cols = slice(c * FFN_COLS, (c + 1) * FFN_COLS)
        lin_cols = slice(FFN_DIM + c * FFN_COLS, FFN_DIM + (c + 1) * FFN_COLS)
        return _dot(h2, wup_ref[:, conv_cols]), _dot(h2, wup_ref[:, lin_cols])

    nxt = up(0)
    for c in range(nf):
        a, lin = nxt
        if c + 1 < nf:
            nxt = up(c + 1)
        cols = slice(c * FFN_COLS, (c + 1) * FFN_COLS)
        a1, a2 = shifted(a, c)
        conv = bdw_ref[:, cols] + wdw_ref[0:1, cols] * a2
        conv = conv + wdw_ref[1:2, cols] * a1
        conv = conv + wdw_ref[2:3, cols] * a
        g_scr[:, cols] = (jax.nn.silu(conv) * lin).astype(BF16)
        emit(a, c)
    o_ref[...] = x_ref[...] + _rms(_dot(g_scr[...], wd_ref[...]), gpost_ref[...])


def _ffn_prompt_kernel(x_ref, h2_ref, wup_ref, wdw_ref, bdw_ref, wd_ref, gpost_ref, hist_ref,
                       o_ref, tail_ref, g_scr, carry_scr, work_scr):
    tm = x_ref.shape[0]
    head = SUBLANES

    @pl.when(pl.program_id(1) == 0)
    def _():
        carry_scr[0:head - (CONV_WIDTH - 1), :] = jnp.zeros((head - (CONV_WIDTH - 1), FFN_DIM), F32)
        carry_scr[head - (CONV_WIDTH - 1):head, :] = hist_ref[...]

    def shifted(a, c):
        cols = slice(c * FFN_COLS, (c + 1) * FFN_COLS)
        work = work_scr.at[c % 2]
        work[0:head, :] = carry_scr[:, cols]
        work[head:head + tm, :] = a
        return work[head - 1:head - 1 + tm, :], work[head - 2:head - 2 + tm, :]

    def emit(a, c):
        cols = slice(c * FFN_COLS, (c + 1) * FFN_COLS)
        carry_scr[:, cols] = a[tm - head:, :]
        tail_ref[:, cols] = a[tm - head:, :]

    _ffn_body(x_ref, h2_ref, wup_ref, wdw_ref, bdw_ref, wd_ref, gpost_ref, o_ref, g_scr, shifted, emit)


def _ffn_sample_kernel(x_ref, h2_ref, wup_ref, wdw_ref, bdw_ref, wd_ref, gpost_ref, e1_ref, e2_ref,
                       o_ref, a_ref, g_scr, work_scr, *, seg):
    tm = x_ref.shape[0]
    head = SUBLANES
    rmod = lax.broadcasted_iota(jnp.int32, (tm, FFN_COLS), 0) % seg

    def shifted(a, c):
        cols = slice(c * FFN_COLS, (c + 1) * FFN_COLS)
        work = work_scr.at[c % 2]
        work[0:head, :] = jnp.zeros((head, FFN_COLS), F32)
        work[head:head + tm, :] = a
        a1 = jnp.where(rmod >= 1, work[head - 1:head - 1 + tm, :], e1_ref[:, cols])
        a2 = jnp.where(rmod >= 2, work[head - 2:head - 2 + tm, :], e2_ref[:, cols])
        return a1, a2

    def emit(a, c):
        a_ref[:, c * FFN_COLS:(c + 1) * FFN_COLS] = a

    _ffn_body(x_ref, h2_ref, wup_ref, wdw_ref, bdw_ref, wd_ref, gpost_ref, o_ref, g_scr, shifted, emit)


def _ffn_weights(lw):
    return (lw["w_up"], lw["w_dwconv"], lw["b_dwconv"], lw["w_down"], lw["g_post_ffn"])


def _ffn_weight_specs(lw, layer):
    return [_layer_spec(a, layer, pipeline_mode=pl.Buffered(1)) for a in _ffn_weights(lw)]


def _ffn_prompt(x, h2, hist, lw, layer):
    nb, s, _ = x.shape
    tm = FFN_ROWS
    nt = s // tm
    row = pl.BlockSpec((None, tm, D_MODEL), lambda b, i: (b, i, 0))
    return pl.pallas_call(
        _ffn_prompt_kernel,
        grid=(nb, nt),
        in_specs=[row, row] + _ffn_weight_specs(lw, layer) + [
            pl.BlockSpec((None, CONV_WIDTH - 1, FFN_DIM), lambda b, i: (b, 0, 0))],
        out_specs=[pl.BlockSpec((None, tm, D_MODEL), lambda b, i: (b, i, 0)),
                   pl.BlockSpec((None, None, SUBLANES, FFN_DIM), lambda b, i: (b, i, 0, 0))],
        out_shape=[jax.ShapeDtypeStruct(x.shape, F32),
                   jax.ShapeDtypeStruct((nb, nt, SUBLANES, FFN_DIM), F32)],
        scratch_shapes=[pltpu.VMEM((tm, FFN_DIM), BF16),
                        pltpu.VMEM((SUBLANES, FFN_DIM), F32),
                        pltpu.VMEM((2, tm + SUBLANES, FFN_COLS), F32)],
        compiler_params=_params("parallel", "arbitrary"),
        name="conv_ffn_prompt",
    )(x, h2, *_ffn_weights(lw), hist)


def _ffn_sample(x, h2, e1, e2, lw, layer, seg):
    rows, _ = x.shape
    whole = lambda w: pl.BlockSpec((rows, w), lambda i: (0, 0))
    return pl.pallas_call(
        functools.partial(_ffn_sample_kernel, seg=seg),
        grid=(1,),
        in_specs=[whole(D_MODEL), whole(D_MODEL)] + _ffn_weight_specs(lw, layer) + [whole(FFN_DIM), whole(FFN_DIM)],
        out_specs=[whole(D_MODEL), whole(FFN_DIM)],
        out_shape=[jax.ShapeDtypeStruct(x.shape, F32), jax.ShapeDtypeStruct((rows, FFN_DIM), F32)],
        scratch_shapes=[pltpu.VMEM((rows, FFN_DIM), BF16),
                        pltpu.VMEM((2, rows + SUBLANES, FFN_COLS), F32)],
        compiler_params=_params("arbitrary"),
        name="conv_ffn_sample",
    )(x, h2, *_ffn_weights(lw), e1, e2)


def _prepare_weights(g_pre_mix, w_in, b_forget, w_spatial, b_spatial, g_sgu, g_group_out, w_out, g_post_mix,
                     g_pre_ffn, w_up, w_dwconv, b_dwconv, w_down, g_post_ffn, dec_seq):
    depth = w_in.shape[0]
    row = lambda a: a.reshape(depth, 1, -1)
    c0 = 3 * FOX_WIDTH
    c1 = c0 + FOX_HEADS
    c2 = c1 + 2 * SGU_WIDTH
    w_b = jnp.concatenate([w_in[:, :, c1:c2], w_in[:, :, c2:], w_in[:, :, c0:c1],
                           jnp.zeros((depth, D_MODEL, B_COLS - (2 * SGU_WIDTH + MEM_WIDTH + FOX_HEADS)), F32)],
                          axis=2)
    group_dim = SGU_WIDTH // SGU_GROUPS
    ws_prompt = jnp.concatenate([w_spatial[:, g] for g in range(SGU_GROUPS)], axis=2)
    bt_prompt = jnp.repeat(jnp.swapaxes(b_spatial, 1, 2), group_dim, axis=2)
    reps = GMLP_CHUNK // dec_seq
    blk = (jnp.arange(GMLP_CHUNK)[:, None] // dec_seq) == (jnp.arange(GMLP_CHUNK)[None, :] // dec_seq)
    ws_sample = jnp.concatenate(
        [jnp.where(blk, jnp.tile(w_spatial[:, g, :dec_seq, :dec_seq], (1, reps, reps)), 0.0)
         for g in range(SGU_GROUPS)], axis=2)
    bt_sample = jnp.tile(jnp.repeat(jnp.swapaxes(b_spatial[:, :, :dec_seq], 1, 2), group_dim, axis=2),
                         (1, reps, 1))
    return {
        "g_pre_mix": row(g_pre_mix), "w_qkv": w_in[:, :, :c0].astype(BF16), "w_b": w_b.astype(BF16),
        "b_forget": jnp.pad(row(b_forget), ((0, 0), (0, 0), (0, LANES - FOX_HEADS))), "g_sgu": row(g_sgu),
        "g_fox": row(g_group_out[:, :FOX_WIDTH]), "g_rest": row(g_group_out[:, FOX_WIDTH:]),
        "w_out": w_out.astype(BF16), "g_post_mix": row(g_post_mix), "g_pre_ffn": row(g_pre_ffn),
        "w_up": w_up.astype(BF16), "w_dwconv": w_dwconv, "b_dwconv": row(b_dwconv),
        "w_down": w_down.astype(BF16), "g_post_ffn": row(g_post_ffn),
        "ws_prompt": ws_prompt, "bt_prompt": bt_prompt, "ws_sample": ws_sample, "bt_sample": bt_sample,
    }


def kernel(x_prompt, x_sample, mem_prompt, cache_fox_k, cache_fox_v, cache_fox_logf, cache_mem_k, cache_mem_v,
           cache_ffn_conv, g_pre_mix, w_in, b_forget, w_spatial, b_spatial, g_sgu, g_mem, w_mem_kv, g_group_out,
           w_out, g_post_mix, g_pre_ffn, w_up, w_dwconv, b_dwconv, w_down, g_post_ffn):
    depth = w_in.shape[0]
    batch, seq, _ = x_prompt.shape
    dec_batch, dec_seq, _ = x_sample.shape
    past = cache_fox_k.shape[2]
    dec_rows = dec_batch * dec_seq

    mem_k_all, mem_v_all = _memory_kv(mem_prompt, g_mem, w_mem_kv.astype(BF16))
    feature_major = lambda a: jnp.transpose(a, (0, 1, 3, 4, 2)).reshape(a.shape[0], a.shape[1], -1, a.shape[2])
    token_major = lambda a, heads: jnp.transpose(
        a.reshape(a.shape[0], a.shape[1], heads, HEAD_DIM, a.shape[3]), (0, 1, 4, 2, 3))
    flat_mem = lambda a: a.reshape(-1, MEM_WIDTH, N_MEM)
    pmk, pmv = flat_mem(mem_k_all), flat_mem(mem_v_all)
    smk, smv = flat_mem(feature_major(cache_mem_k)), flat_mem(feature_major(cache_mem_v))
    hk = feature_major(cache_fox_k)
    hv = feature_major(cache_fox_v)

    lw = _prepare_weights(g_pre_mix, w_in, b_forget, w_spatial, b_spatial, g_sgu, g_group_out, w_out, g_post_mix,
                          g_pre_ffn, w_up, w_dwconv, b_dwconv, w_down, g_post_ffn, dec_seq)
    hist_rows = _lane_cumsum(jnp.swapaxes(cache_fox_logf, 2, 3).reshape(depth * dec_batch * FOX_HEADS, past),
                             minus_total=True).reshape(depth, dec_batch, FOX_HEADS, past)
    pad_rows = lambda a: jnp.pad(a, ((0, 0), (0, 0), (0, dec_seq - a.shape[2]), (0, 0))
                                 ).reshape(depth, dec_rows, FFN_DIM)
    e1_all = pad_rows(cache_ffn_conv[:, :, 1:2, :])
    e2_all = pad_rows(cache_ffn_conv)
    zeros_hist = jnp.zeros((batch, CONV_WIDTH - 1, FFN_DIM), F32)

    yp = x_prompt
    ys = x_sample.reshape(1, dec_rows, D_MODEL)
    outs = {name: [] for name in ("logf_p", "conv_p", "k_s", "v_s", "logf_s", "gv_s", "conv_s")}
    stacked_kv = ()
    per_b = lambda a: a.reshape(dec_batch, dec_seq, a.shape[-1])
    for l in range(depth):
        qt, k_all, v_all, kb, vt, logf, rest = _in_proj(yp, pmk, pmv, l * batch, 1, lw, IN_PROJ_ROWS,
                                                        lw["ws_prompt"], lw["bt_prompt"], True,
                                                        layer=l, depth=depth, stacked_kv=stacked_kv)
        stacked_kv = (k_all, v_all)
        c_row = _lane_cumsum(logf.reshape(batch * FOX_HEADS, seq)).reshape(batch, FOX_HEADS, seq)
        yp, h2 = _fox_prompt(qt, kb, vt, c_row, rest, yp, lw, l)
        yp, tail = _ffn_prompt(yp, h2, zeros_hist, lw, l)
        outs["logf_p"].append(logf)
        outs["conv_p"].append(tail[:, -1, SUBLANES - (CONV_WIDTH - 1):, :])

        q, k, v, kb, vb, logf, rest, vrows = _in_proj(ys, smk, smv, l * dec_batch, dec_batch, lw, dec_rows,
                                                      lw["ws_sample"], lw["bt_sample"], False,
                                                      layer=l, depth=depth)
        logf_new_row = jnp.swapaxes(per_b(logf), 1, 2).reshape(dec_batch * FOX_HEADS, dec_seq)
        new_row = _lane_cumsum(jnp.pad(logf_new_row, ((0, 0), (0, LANES - dec_seq)))
                               ).reshape(dec_batch, FOX_HEADS, LANES)
        new_col = new_row[:, :, :dec_seq].reshape(dec_batch, FOX_HEADS * dec_seq, 1)
        ys2, h2s = _fox_sample(per_b(q[0]), per_b(kb[0]), per_b(vb[0]), hk, hv, l, hist_rows[l], new_col, new_row,
                               rest[0], ys[0], lw)
        ys_flat, a_all = _ffn_sample(ys2, h2s, e1_all[l], e2_all[l], lw, l, dec_seq)
        ys = ys_flat.reshape(1, dec_rows, D_MODEL)
        outs["k_s"].append(per_b(k[0]).reshape(dec_batch, dec_seq, FOX_HEADS, HEAD_DIM))
        outs["v_s"].append(per_b(v[0]).reshape(dec_batch, dec_seq, FOX_HEADS, HEAD_DIM))
        outs["logf_s"].append(per_b(logf[0]))
        outs["gv_s"].append(per_b(vrows[0]))
        outs["conv_s"].append(a_all.reshape(dec_batch, dec_seq, FFN_DIM)[:, dec_seq - (CONV_WIDTH - 1):, :])

    st = {name: jnp.stack(vals) for name, vals in outs.items()}
    return (yp, ys.reshape(dec_batch, dec_seq, D_MODEL),
            token_major(stacked_kv[0], FOX_HEADS), token_major(stacked_kv[1], FOX_HEADS),
            jnp.swapaxes(st["logf_p"], 2, 3),
            token_major(mem_k_all, MEM_HEADS), token_major(mem_v_all, MEM_HEADS),
            st["conv_p"], st["k_s"], st["v_s"], st["logf_s"], st["gv_s"], st["conv_s"])
```

```python
import functools

import jax
import jax.numpy as jnp
from jax import lax
from jax.experimental import pallas as pl
from jax.experimental.pallas import tpu as pltpu

D_MODEL = 1024
HEAD_DIM = 64
FOX_WIDTH = 512
FOX_HEADS = 8
SGU_WIDTH = 256
SGU_GROUPS = 4
GMLP_CHUNK = 128
CHUNK = 64
MEM_WIDTH = 256
MEM_HEADS = 4
N_MEM = 256
FFN_DIM = 2816
CONV_WIDTH = 3
RMS_EPS = 1e-6
NEG_INF = -1e30
QK_SCALE = HEAD_DIM ** -0.5
LOG2E = 1.4426950408889634

LANES = 128
SUBLANES = 8
BF16_SUBLANES = 2 * SUBLANES
HEAD_PAIR = LANES // HEAD_DIM
VMEM_LIMIT_BYTES = 56 * 1024 * 1024

IN_PROJ_ROWS = 1024
ATTN_Q_ROWS = 512
ATTN_K_ROWS = 256
FFN_ROWS = 1024
FFN_COLS = 256
B_COLS = 896
MEMKV_BATCH_GROUP = 4

BF16 = jnp.bfloat16
F32 = jnp.float32


def _rms(x, g):
    y = x * lax.rsqrt(jnp.mean(x * x, axis=-1, keepdims=True) + RMS_EPS)
    return y * g


def _dot(a, b):
    return jnp.dot(a, b, preferred_element_type=F32)


def _dot_nt(a, b):
    return lax.dot_general(a, b, (((1,), (1,)), ((), ())), preferred_element_type=F32)


def _params(*semantics):
    return pltpu.CompilerParams(dimension_semantics=semantics, vmem_limit_bytes=VMEM_LIMIT_BYTES)


def _layer_spec(a, layer, **kwargs):
    return pl.BlockSpec((None,) + a.shape[1:], lambda *_: (layer,) + (0,) * (a.ndim - 1), **kwargs)


def _memkv_kernel(mem_ref, g_ref, w_ref, mk_ref, mv_ref):
    for j in range(mem_ref.shape[0]):
        h = _rms(mem_ref[j], g_ref[...]).astype(BF16)
        kv = _dot(h, w_ref[...])
        mk_ref[j] = kv[:, :MEM_WIDTH].T
        mv_ref[j] = kv[:, MEM_WIDTH:].T


def _memory_kv(mem, g_mem, w_mem_kv_bf):
    depth = g_mem.shape[0]
    batch = mem.shape[0]
    group = MEMKV_BATCH_GROUP
    out = jax.ShapeDtypeStruct((depth, batch, MEM_WIDTH, N_MEM), F32)
    out_spec = pl.BlockSpec((None, group, MEM_WIDTH, N_MEM), lambda g, l: (l, g, 0, 0))
    return pl.pallas_call(
        _memkv_kernel,
        grid=(batch // group, depth),
        in_specs=[
            pl.BlockSpec((group, N_MEM, D_MODEL), lambda g, l: (g, 0, 0)),
            pl.BlockSpec((None, 1, D_MODEL), lambda g, l: (l, 0, 0)),
            pl.BlockSpec((None, D_MODEL, 2 * MEM_WIDTH), lambda g, l: (l, 0, 0)),
        ],
        out_specs=[out_spec, out_spec],
        out_shape=[out, out],
        compiler_params=_params("arbitrary", "arbitrary"),
        name="memory_kv",
    )(mem, g_mem.reshape(depth, 1, D_MODEL), w_mem_kv_bf)


def _inproj_kernel(x_ref, gpre_ref, wqkv_ref, wb_ref, bfg_ref, ws_ref, bt_ref, gsgu_ref, gg_ref,
                   mk_ref, mv_ref, *out_refs, transposed, n_stack_in):
    out_refs = out_refs[n_stack_in:]
    tm = x_ref.shape[0]
    n_mem_blocks = mk_ref.shape[0]
    if transposed:
        qt_ref, k_ref, v_ref, kb_ref, vt_ref, lf_ref, c_ref, rest_ref, csum_scr = out_refs
        vrows_ref = None

        @pl.when(pl.program_id(1) == 0)
        def _():
            csum_scr[...] = jnp.zeros_like(csum_scr)
    else:
        q_ref, k_ref, v_ref, kb_ref, vb_ref, lf_ref, rest_ref, vrows_ref = out_refs
    h = _rms(x_ref[...], gpre_ref[...]).astype(BF16)

    yb = _dot(h, wb_ref[...])
    fg = yb[:, 2 * SGU_WIDTH + MEM_WIDTH:2 * SGU_WIDTH + MEM_WIDTH + LANES]
    lf = jax.nn.log_sigmoid(fg + bfg_ref[...])
    if transposed:
        lf_t = lf.T[:FOX_HEADS, :]
        lf_ref[...] = lf_t
        c = _lane_scan(lf_t) + csum_scr[:, :1]
        c_ref[...] = c
        csum_scr[...] = jnp.broadcast_to(c[:, tm - 1:], csum_scr.shape)
    else:
        lf_ref[...] = lf[:, :FOX_HEADS]

    qmb = (yb[:, 2 * SGU_WIDTH:2 * SGU_WIDTH + MEM_WIDTH] * QK_SCALE).astype(BF16)
    rows = tm // n_mem_blocks
    hlane = lax.broadcasted_iota(jnp.int32, (rows, MEM_WIDTH), 1) // HEAD_DIM
    zero_q = jnp.zeros((rows, MEM_WIDTH), BF16)
    mem_scores = []
    for bi in range(n_mem_blocks):
        qb = qmb[bi * rows:(bi + 1) * rows, :]
        mkt = mk_ref[bi].astype(BF16)
        mem_scores.append([_dot(jnp.where(hlane == hd, qb, zero_q), mkt) for hd in range(MEM_HEADS)])

    q = _dot(h, wqkv_ref[:, :FOX_WIDTH]) * (QK_SCALE * LOG2E if transposed else QK_SCALE)

    z = jax.nn.gelu(yb[:, :2 * SGU_WIDTH])
    u = z[:, :SGU_WIDTH]
    vv = _rms(z[:, SGU_WIDTH:], gsgu_ref[...])
    if vrows_ref is not None:
        vrows_ref[...] = vv
    vvb = vv.astype(BF16)
    mem_exp = [[jnp.exp(s - jnp.max(s, axis=-1, keepdims=True)) for s in per_block] for per_block in mem_scores]

    wrow = lax.broadcasted_iota(jnp.int32, (GMLP_CHUNK, SGU_GROUPS * GMLP_CHUNK), 0)
    wcol = lax.broadcasted_iota(jnp.int32, (GMLP_CHUNK, SGU_GROUPS * GMLP_CHUNK), 1)
    wmask = ((wcol % GMLP_CHUNK) // CHUNK) <= (wrow // CHUNK)
    wcat = jnp.where(wmask, ws_ref[...], 0.0).astype(BF16)
    glane = lax.broadcasted_iota(jnp.int32, (GMLP_CHUNK, SGU_WIDTH), 1) // (SGU_WIDTH // SGU_GROUPS)
    zero_chunk = jnp.zeros((GMLP_CHUNK, SGU_WIDTH), BF16)
    mixed_parts = []
    for c in range(tm // GMLP_CHUNK):
        vc = vvb[c * GMLP_CHUNK:(c + 1) * GMLP_CHUNK, :]
        rhs = jnp.concatenate([jnp.where(glane == g, vc, zero_chunk) for g in range(SGU_GROUPS)], axis=0)
        mixed_parts.append(_dot(wcat, rhs))
    mem_pv = []
    for bi in range(n_mem_blocks):
        mvt = mv_ref[bi].astype(BF16)
        mem_pv.append([_dot_nt(e.astype(BF16), mvt) for e in mem_exp[bi]])

    k = _dot(h, wqkv_ref[:, FOX_WIDTH:2 * FOX_WIDTH])

    if transposed:
        qt_ref[...] = q.T.astype(BF16)
    else:
        q_ref[...] = q.astype(BF16)
    sgu = jnp.concatenate([u[c * GMLP_CHUNK:(c + 1) * GMLP_CHUNK, :] * (mixed + bt_ref[...])
                           for c, mixed in enumerate(mixed_parts)], axis=0)
    rest_ref[:, :SGU_WIDTH] = _rms(sgu, gg_ref[:, :SGU_WIDTH]).astype(BF16)
    mem_parts = []
    for bi in range(n_mem_blocks):
        out = jnp.zeros((rows, MEM_WIDTH), F32)
        for hd in range(MEM_HEADS):
            o = mem_pv[bi][hd] / jnp.sum(mem_exp[bi][hd], axis=-1, keepdims=True)
            out = jnp.where(hlane == hd, o, out)
        mem_parts.append(out)
    mem = mem_parts[0] if n_mem_blocks == 1 else jnp.concatenate(mem_parts, axis=0)
    rest_ref[:, SGU_WIDTH:] = _rms(mem, gg_ref[:, SGU_WIDTH:]).astype(BF16)

    v = _dot(h, wqkv_ref[:, 2 * FOX_WIDTH:])

    kb_ref[...] = k.astype(BF16)
    if transposed:
        k_ref[...] = k.T
        vt = v.T
        v_ref[...] = vt
        vtb = vt.astype(BF16)
        tk = vt_ref.shape[-1]
        for c in range(tm // tk):
            vt_ref[c] = vtb[:, c * tk:(c + 1) * tk]
    else:
        k_ref[...] = k
        v_ref[...] = v
        vb_ref[...] = v.astype(BF16)


def _in_proj(x, mem_k, mem_v, mem_first, mem_per_tile, lw, tm, spatial_w, spatial_b, transposed,
             layer, depth, stacked_kv=()):
    nb, t, _ = x.shape
    nt = t // tm
    if transposed:
        assert mem_per_tile == 1
        mem_index = lambda b, i: (mem_first + b, 0, 0)
    else:
        mem_index = lambda b, i: (mem_first // mem_per_tile + b * nt + i, 0, 0)
    row = lambda w: pl.BlockSpec((None, tm, w), lambda b, i: (b, i, 0))
    memspec = pl.BlockSpec((mem_per_tile, MEM_WIDTH, N_MEM), mem_index)
    f32 = lambda w: jax.ShapeDtypeStruct((nb, t, w), F32)
    bf = lambda w: jax.ShapeDtypeStruct((nb, t, w), BF16)
    consts = (lw["g_pre_mix"], lw["w_qkv"], lw["w_b"], lw["b_forget"], spatial_w, spatial_b,
              lw["g_sgu"], lw["g_rest"])
    if transposed:
        tk = ATTN_K_ROWS
        stack_spec = pl.BlockSpec((None, None, FOX_WIDTH, tm), lambda b, i: (layer, b, 0, i))
        stack_shape = jax.ShapeDtypeStruct((depth, nb, FOX_WIDTH, t), F32)
        out_specs = [pl.BlockSpec((None, FOX_WIDTH, tm), lambda b, i: (b, 0, i)),
                     stack_spec, stack_spec, row(FOX_WIDTH),
                     pl.BlockSpec((None, tm // tk, FOX_WIDTH, tk), lambda b, i: (b, i, 0, 0)),
                     pl.BlockSpec((None, FOX_HEADS, tm), lambda b, i: (b, 0, i)),
                     pl.BlockSpec((None, FOX_HEADS, tm), lambda b, i: (b, 0, i)), row(SGU_WIDTH + MEM_WIDTH)]
        out_shape = [jax.ShapeDtypeStruct((nb, FOX_WIDTH, t), BF16), stack_shape, stack_shape,
                     bf(FOX_WIDTH), jax.ShapeDtypeStruct((nb, t // tk, FOX_WIDTH, tk), BF16),
                     jax.ShapeDtypeStruct((nb, FOX_HEADS, t), F32), jax.ShapeDtypeStruct((nb, FOX_HEADS, t), F32),
                     bf(SGU_WIDTH + MEM_WIDTH)]
        scratch = [pltpu.VMEM((FOX_HEADS, LANES), F32)]
    else:
        out_specs = [row(FOX_WIDTH)] * 5 + [row(FOX_HEADS), row(SGU_WIDTH + MEM_WIDTH), row(SGU_WIDTH)]
        out_shape = [bf(FOX_WIDTH), f32(FOX_WIDTH), f32(FOX_WIDTH), bf(FOX_WIDTH), bf(FOX_WIDTH),
                     f32(FOX_HEADS), bf(SGU_WIDTH + MEM_WIDTH), f32(SGU_WIDTH)]
        scratch = []
    in_specs = [row(D_MODEL)] + [_layer_spec(a, layer) for a in consts] + [memspec, memspec]
    aliases = {len(in_specs) + n: 1 + n for n in range(len(stacked_kv))}
    in_specs += [pl.BlockSpec(memory_space=pl.ANY)] * len(stacked_kv)
    return pl.pallas_call(
        functools.partial(_inproj_kernel, transposed=transposed, n_stack_in=len(stacked_kv)),
        grid=(nb, nt),
        in_specs=in_specs,
        out_specs=out_specs,
        out_shape=out_shape,
        scratch_shapes=scratch,
        input_output_aliases=aliases,
        compiler_params=_params("parallel", "arbitrary"),
        name="in_proj",
    )(x, *consts, mem_k, mem_v, *stacked_kv)


def _lane_scan(x):
    n = x.shape[-1]
    lane = lax.broadcasted_iota(jnp.int32, x.shape, 1)
    shift = 1
    while shift < n:
        x = x + jnp.where(lane >= shift, pltpu.roll(x, shift, 1), 0.0)
        shift *= 2
    return x


def _cumsum_kernel(x_ref, o_ref, *, minus_total):
    x = _lane_scan(x_ref[...])
    if minus_total:
        x = x - x[:, x.shape[-1] - 1:]
    o_ref[...] = x


def _lane_cumsum(x, minus_total=False):
    return pl.pallas_call(
        functools.partial(_cumsum_kernel, minus_total=minus_total),
        out_shape=jax.ShapeDtypeStruct(x.shape, F32),
        name="lane_cumsum",
    )(x)


def _mix_out(fox, rest, x, wout_ref, gfox_ref, gpost_ref, gffn_ref, o_ref, h2_ref):
    fox_n = _rms(fox, gfox_ref[...]).astype(BF16)
    y = _dot(fox_n, wout_ref[:FOX_WIDTH, :]) + _dot(rest, wout_ref[FOX_WIDTH:, :])
    x_mid = x + _rms(y, gpost_ref[...])
    o_ref[...] = x_mid
    h2_ref[...] = _rms(x_mid, gffn_ref[...]).astype(BF16)


def _split3(c):
    hi = c.astype(BF16).astype(F32)
    r = c - hi
    mid = r.astype(BF16).astype(F32)
    lo = (r - mid).astype(BF16).astype(F32)
    return hi, mid, lo


_BIAS_ONES = 6


def _fox_prompt_kernel(qt_ref, kb_ref, vt_ref, call_ref, crow_ref, rest_ref, x_ref, wout_ref, gfox_ref,
                       gpost_ref, gffn_ref, o_ref, h2_ref, kaug_scr, qa_scr, sa_scr, sb_scr, m_scr, l_scr, acc_scr):
    tq = qt_ref.shape[1]
    tk = vt_ref.shape[-1]
    s_len = kb_ref.shape[0]
    slab = 2 * LANES
    i = pl.program_id(1)

    @pl.when(i == 0)
    def _():
        n_slab = FOX_HEADS // HEAD_PAIR
        src = lax.broadcasted_iota(jnp.int32, (LANES, n_slab * LANES), 0)
        dst = lax.broadcasted_iota(jnp.int32, (LANES, n_slab * LANES), 1)
        head, piece = src % FOX_HEADS, src // FOX_HEADS
        piece_dst = (head // HEAD_PAIR) * LANES + 3 * (head % HEAD_PAIR) + piece
        is_piece = (src < 3 * FOX_HEADS) & (dst == piece_dst)
        is_one = (src == 3 * FOX_HEADS) & (dst % LANES >= _BIAS_ONES) & (dst % LANES < _BIAS_ONES + 3)
        place = jnp.where(is_piece | is_one, 1.0, 0.0).astype(BF16)
        pad_row = lax.broadcasted_iota(jnp.int32, (LANES - 3 * FOX_HEADS, tk), 0)
        pad = jnp.where(pad_row == 0, 1.0, 0.0)
        for r in range(s_len // tk):
            rows = slice(r * tk, (r + 1) * tk)
            pieces_t = jnp.concatenate(_split3(call_ref[:, rows] * LOG2E) + (pad,), axis=0)
            bias = _dot(pieces_t.T.astype(BF16), place)
            for p in range(n_slab):
                kaug_scr[rows, p * slab:p * slab + LANES] = kb_ref[rows, p * LANES:(p + 1) * LANES]
                kaug_scr[rows, p * slab + LANES:(p + 1) * slab] = bias[:, p * LANES:(p + 1) * LANES].astype(BF16)

    rowi = lax.broadcasted_iota(jnp.int32, (LANES, tq), 0)
    crow = crow_ref[...]
    zero_q = jnp.zeros((LANES, tq), BF16)
    for p in range(FOX_HEADS // HEAD_PAIR):
        qtp = qt_ref[p * LANES:(p + 1) * LANES, :]
        for hh in range(HEAD_PAIR):
            own = (rowi >= hh * HEAD_DIM) & (rowi < (hh + 1) * HEAD_DIM)
            chi, cmid, clo = _split3(crow[2 * p + hh:2 * p + hh + 1, :] * LOG2E)
            br = jnp.where(rowi == _BIAS_ONES, chi,
                           jnp.where(rowi == _BIAS_ONES + 1, cmid, jnp.where(rowi == _BIAS_ONES + 2, clo, 0.0)))
            br = jnp.where((rowi >= 3 * hh) & (rowi < 3 * hh + 3), -1.0, br)
            qa_scr[2 * p + hh, :LANES, :] = jnp.where(own, qtp, zero_q)
            qa_scr[2 * p + hh, LANES:, :] = br.astype(BF16)

    m_scr[...] = jnp.full(m_scr.shape, NEG_INF, F32)
    l_scr[...] = jnp.zeros(l_scr.shape, F32)
    acc_scr[...] = jnp.zeros(acc_scr.shape, F32)

    def scores_to(buf_ref, j):
        off = pl.multiple_of(j * tk, tk)
        for h in range(FOX_HEADS):
            ka = kaug_scr[pl.ds(off, tk), (h // HEAD_PAIR) * slab:(h // HEAD_PAIR + 1) * slab]
            buf_ref[h] = _dot(ka, qa_scr[h])

    ones_rows = jnp.ones((BF16_SUBLANES, tk), BF16)

    def absorb_from(buf_ref, j, mask, cols=slice(None)):
        vt = vt_ref[j]
        for h in range(FOX_HEADS):
            feat = slice(h * HEAD_DIM, (h + 1) * HEAD_DIM)
            s = buf_ref[h, :, cols]
            if mask is not None:
                s = jnp.where(mask, s, NEG_INF)
            m = m_scr[h:h + 1, cols]
            m_new = jnp.maximum(m, jnp.max(s, axis=0, keepdims=True))
            alpha = jnp.exp2(m - m_new)
            e = jnp.exp2(s - m_new).astype(BF16)
            m_scr[h:h + 1, cols] = m_new
            pv = _dot(jnp.concatenate([vt[feat, :], ones_rows], axis=0), e)
            l_scr[h:h + 1, cols] = alpha * l_scr[h:h + 1, cols] + pv[HEAD_DIM:HEAD_DIM + 1, :]
            acc_scr[feat, cols] = alpha * acc_scr[feat, cols] + pv[:HEAD_DIM, :]

    scores_to(sa_scr, 0)

    def step(jj, _):
        t = 2 * jj
        scores_to(sb_scr, t + 1)
        absorb_from(sa_scr, t, None)
        scores_to(sa_scr, t + 2)
        absorb_from(sb_scr, t + 1, None)
        return 0

    lax.fori_loop(0, i, step, 0)
    scores_to(sb_scr, 2 * i + 1)
    causal = lambda n: lax.broadcasted_iota(jnp.int32, (tk, n), 0) <= lax.broadcasted_iota(jnp.int32, (tk, n), 1)
    absorb_from(sa_scr, 2 * i, causal(tq))
    absorb_from(sb_scr, 2 * i + 1, causal(tq - tk), cols=slice(tk, tq))
    inv_l = 1.0 / l_scr[...]
    fox_t = jnp.concatenate([acc_scr[h * HEAD_DIM:(h + 1) * HEAD_DIM, :] * inv_l[h:h + 1, :]
                             for h in range(FOX_HEADS)], axis=0)
    fox = fox_t.T
    _mix_out(fox, rest_ref[...], x_ref[...], wout_ref, gfox_ref, gpost_ref, gffn_ref, o_ref, h2_ref)


def _mix_weights(lw):
    return (lw["w_out"], lw["g_fox"], lw["g_post_mix"], lw["g_pre_ffn"])


def _fox_prompt(qt, kb, vt, c_row, rest, x, lw, layer):
    nb, s, _ = x.shape
    tq, tk = ATTN_Q_ROWS, ATTN_K_ROWS
    assert tq == 2 * tk and vt.shape[-1] == tk
    nq = s // tq
    qrow = lambda w: pl.BlockSpec((None, tq, w), lambda b, i: (b, i, 0))
    qcol = lambda r: pl.BlockSpec((None, r, tq), lambda b, i: (b, 0, i))
    full = lambda a: pl.BlockSpec((None,) + a.shape[1:], lambda b, i: (b,) + (0,) * (a.ndim - 1))
    return pl.pallas_call(
        _fox_prompt_kernel,
        grid=(nb, nq),
        in_specs=[qcol(FOX_WIDTH), full(kb), full(vt), full(c_row), qcol(FOX_HEADS),
                  qrow(SGU_WIDTH + MEM_WIDTH), qrow(D_MODEL)] + [_layer_spec(a, layer) for a in _mix_weights(lw)],
        out_specs=[qrow(D_MODEL), qrow(D_MODEL)],
        out_shape=[jax.ShapeDtypeStruct(x.shape, F32), jax.ShapeDtypeStruct(x.shape, BF16)],
        scratch_shapes=[pltpu.VMEM((s, 2 * FOX_WIDTH), BF16),
                        pltpu.VMEM((FOX_HEADS, 2 * LANES, tq), BF16),
                        pltpu.VMEM((FOX_HEADS, tk, tq), F32), pltpu.VMEM((FOX_HEADS, tk, tq), F32),
                        pltpu.VMEM((FOX_HEADS, tq), F32), pltpu.VMEM((FOX_HEADS, tq), F32),
                        pltpu.VMEM((FOX_WIDTH, tq), F32)],
        compiler_params=_params("parallel", "arbitrary"),
        name="fox_mix_prompt",
    )(qt, kb, vt, c_row, c_row, rest, x, *_mix_weights(lw))


def _fox_sample_kernel(q_ref, kn_ref, vn_ref, hk_ref, hv_ref, hrow_ref, ncol_ref, nrow_ref, rest_ref, x_ref,
                       wout_ref, gfox_ref, gpost_ref, gffn_ref, o_ref, h2_ref, fox_scr):
    t = q_ref.shape[0]
    b = pl.program_id(0)
    rows = FOX_HEADS * t
    qt = jnp.concatenate([q_ref[...]] * FOX_HEADS, axis=0)
    row_head = lax.broadcasted_iota(jnp.int32, (rows, FOX_WIDTH), 0) // t
    lane_head = lax.broadcasted_iota(jnp.int32, (rows, FOX_WIDTH), 1) // HEAD_DIM
    own = row_head == lane_head
    qb = jnp.where(own, qt, jnp.zeros_like(qt))
    per_head = lambda a: jnp.concatenate(
        [jnp.broadcast_to(a[h:h + 1, :], (t, a.shape[1])) for h in range(FOX_HEADS)], axis=0)
    bq = ncol_ref[...]
    s_hist = _dot(qb, hk_ref[...].astype(BF16)) + bq - per_head(hrow_ref[...])
    s_new = _dot_nt(qb, kn_ref[...]) + bq - per_head(nrow_ref[:, :t])
    causal = (lax.broadcasted_iota(jnp.int32, (rows, t), 1)
              <= lax.broadcasted_iota(jnp.int32, (rows, t), 0) % t)
    s_new = jnp.where(causal, s_new, NEG_INF)
    m = jnp.maximum(jnp.max(s_hist, axis=-1, keepdims=True), jnp.max(s_new, axis=-1, keepdims=True))
    e_hist = jnp.exp(s_hist - m)
    e_new = jnp.exp(s_new - m)
    l = jnp.sum(e_hist, axis=-1, keepdims=True) + jnp.sum(e_new, axis=-1, keepdims=True)
    o = (_dot_nt(e_hist.astype(BF16), hv_ref[...].astype(BF16)) + _dot(e_new.astype(BF16), vn_ref[...])) / l
    o = jnp.where(own, o, 0.0)
    fox = o[0:t, :]
    for h in range(1, FOX_HEADS):
        fox = fox + o[h * t:(h + 1) * t, :]
    fox_scr[pl.ds(pl.multiple_of(b * t, t), t), :] = fox

    @pl.when(b == pl.num_programs(0) - 1)
    def _():
        _mix_out(fox_scr[...], rest_ref[...], x_ref[...], wout_ref, gfox_ref, gpost_ref, gffn_ref, o_ref, h2_ref)


def _fox_sample(q, kb, vb, hk, hv, layer, hist_row, new_col, new_row, rest, x, lw):
    nb, t, _ = q.shape
    per_b = lambda a: pl.BlockSpec((None,) + a.shape[1:], lambda b: (b,) + (0,) * (a.ndim - 1))
    cache = pl.BlockSpec((None, None) + hk.shape[2:], lambda b: (layer, b, 0, 0))
    const = lambda a: pl.BlockSpec(a.shape, lambda b: (0,) * a.ndim)
    return pl.pallas_call(
        _fox_sample_kernel,
        grid=(nb,),
        in_specs=[per_b(q), per_b(kb), per_b(vb), cache, cache, per_b(hist_row), per_b(new_col),
                  per_b(new_row), const(rest), const(x)] + [_layer_spec(a, layer) for a in _mix_weights(lw)],
        out_specs=[const(x), const(x)],
        out_shape=[jax.ShapeDtypeStruct(x.shape, F32), jax.ShapeDtypeStruct(x.shape, BF16)],
        scratch_shapes=[pltpu.VMEM((nb * t, FOX_WIDTH), F32)],
        compiler_params=_params("arbitrary"),
        name="fox_mix_sample",
    )(q, kb, vb, hk, hv, hist_row, new_col, new_row, rest, x, *_mix_weights(lw))


def _ffn_body(x_ref, h2_ref, wup_ref, wdw_ref, bdw_ref, wd_ref, gpost_ref, o_ref, g_scr, shifted, emit):
    h2 = h2_ref[...]
    nf = FFN_DIM // FFN_COLS

    def up(c):
        conv_cols = slice(c * FFN_COLS, (c + 1) * FFN_COLS)
        lin_cols = slice(FFN_DIM + c * FFN_COLS, FFN_DIM + (c + 1) * FFN_COLS)
        return _dot(h2, wup_ref[:, conv_cols]), _dot(h2, wup_ref[:, lin_cols])

    nxt = up(0)
    for c in range(nf):
        a, lin = nxt
        if c + 1 < nf:
            nxt = up(c + 1)
        cols = slice(c * FFN_COLS, (c + 1) * FFN_COLS)
        a1, a2 = shifted(a, c)
        conv = bdw_ref[:, cols] + wdw_ref[0:1, cols] * a2
        conv = conv + wdw_ref[1:2, cols] * a1
        conv = conv + wdw_ref[2:3, cols] * a
        g_scr[:, cols] = (jax.nn.silu(conv) * lin).astype(BF16)
        emit(a, c)
    o_ref[...] = x_ref[...] + _rms(_dot(g_scr[...], wd_ref[...]), gpost_ref[...])


def _ffn_prompt_kernel(x_ref, h2_ref, wup_ref, wdw_ref, bdw_ref, wd_ref, gpost_ref, hist_ref,
                       o_ref, tail_ref, g_scr, carry_scr, work_scr):
    tm = x_ref.shape[0]
    head = SUBLANES

    @pl.when(pl.program_id(1) == 0)
    def _():
        carry_scr[0:head - (CONV_WIDTH - 1), :] = jnp.zeros((head - (CONV_WIDTH - 1), FFN_DIM), F32)
        carry_scr[head - (CONV_WIDTH - 1):head, :] = hist_ref[...]

    def shifted(a, c):
        cols = slice(c * FFN_COLS, (c + 1) * FFN_COLS)
        work = work_scr.at[c % 2]
        work[0:head, :] = carry_scr[:, cols]
        work[head:head + tm, :] = a
        return work[head - 1:head - 1 + tm, :], work[head - 2:head - 2 + tm, :]

    def emit(a, c):
        cols = slice(c * FFN_COLS, (c + 1) * FFN_COLS)
        carry_scr[:, cols] = a[tm - head:, :]
        tail_ref[:, cols] = a[tm - head:, :]

    _ffn_body(x_ref, h2_ref, wup_ref, wdw_ref, bdw_ref, wd_ref, gpost_ref, o_ref, g_scr, shifted, emit)


def _ffn_sample_kernel(x_ref, h2_ref, wup_ref, wdw_ref, bdw_ref, wd_ref, gpost_ref, e1_ref, e2_ref,
                       o_ref, a_ref, g_scr, work_scr, *, seg):
    tm = x_ref.shape[0]
    head = SUBLANES
    rmod = lax.broadcasted_iota(jnp.int32, (tm, FFN_COLS), 0) % seg

    def shifted(a, c):
        cols = slice(c * FFN_COLS, (c + 1) * FFN_COLS)
        work = work_scr.at[c % 2]
        work[0:head, :] = jnp.zeros((head, FFN_COLS), F32)
        work[head:head + tm, :] = a
        a1 = jnp.where(rmod >= 1, work[head - 1:head - 1 + tm, :], e1_ref[:, cols])
        a2 = jnp.where(rmod >= 2, work[head - 2:head - 2 + tm, :], e2_ref[:, cols])
        return a1, a2

    def emit(a, c):
        a_ref[:, c * FFN_COLS:(c + 1) * FFN_COLS] = a

    _ffn_body(x_ref, h2_ref, wup_ref, wdw_ref, bdw_ref, wd_ref, gpost_ref, o_ref, g_scr, shifted, emit)


def _ffn_weights(lw):
    return (lw["w_up"], lw["w_dwconv"], lw["b_dwconv"], lw["w_down"], lw["g_post_ffn"])


def _ffn_weight_specs(lw, layer):
    return [_layer_spec(a, layer, pipeline_mode=pl.Buffered(1)) for a in _ffn_weights(lw)]


def _ffn_prompt(x, h2, hist, lw, layer):
    nb, s, _ = x.shape
    tm = FFN_ROWS
    nt = s // tm
    row = pl.BlockSpec((None, tm, D_MODEL), lambda b, i: (b, i, 0))
    return pl.pallas_call(
        _ffn_prompt_kernel,
        grid=(nb, nt),
        in_specs=[row, row] + _ffn_weight_specs(lw, layer) + [
            pl.BlockSpec((None, CONV_WIDTH - 1, FFN_DIM), lambda b, i: (b, 0, 0))],
        out_specs=[pl.BlockSpec((None, tm, D_MODEL), lambda b, i: (b, i, 0)),
                   pl.BlockSpec((None, None, SUBLANES, FFN_DIM), lambda b, i: (b, i, 0, 0))],
        out_shape=[jax.ShapeDtypeStruct(x.shape, F32),
                   jax.ShapeDtypeStruct((nb, nt, SUBLANES, FFN_DIM), F32)],
        scratch_shapes=[pltpu.VMEM((tm, FFN_DIM), BF16),
                        pltpu.VMEM((SUBLANES, FFN_DIM), F32),
                        pltpu.VMEM((2, tm + SUBLANES, FFN_COLS), F32)],
        compiler_params=_params("parallel", "arbitrary"),
        name="conv_ffn_prompt",
    )(x, h2, *_ffn_weights(lw), hist)


def _ffn_sample(x, h2, e1, e2, lw, layer, seg):
    rows, _ = x.shape
    whole = lambda w: pl.BlockSpec((rows, w), lambda i: (0, 0))
    return pl.pallas_call(
        functools.partial(_ffn_sample_kernel, seg=seg),
        grid=(1,),
        in_specs=[whole(D_MODEL), whole(D_MODEL)] + _ffn_weight_specs(lw, layer) + [whole(FFN_DIM), whole(FFN_DIM)],
        out_specs=[whole(D_MODEL), whole(FFN_DIM)],
        out_shape=[jax.ShapeDtypeStruct(x.shape, F32), jax.ShapeDtypeStruct((rows, FFN_DIM), F32)],
        scratch_shapes=[pltpu.VMEM((rows, FFN_DIM), BF16),
                        pltpu.VMEM((2, rows + SUBLANES, FFN_COLS), F32)],
        compiler_params=_params("arbitrary"),
        name="conv_ffn_sample",
    )(x, h2, *_ffn_weights(lw), e1, e2)


def _prepare_weights(g_pre_mix, w_in, b_forget, w_spatial, b_spatial, g_sgu, g_group_out, w_out, g_post_mix,
                     g_pre_ffn, w_up, w_dwconv, b_dwconv, w_down, g_post_ffn, dec_seq):
    depth = w_in.shape[0]
    row = lambda a: a.reshape(depth, 1, -1)
    c0 = 3 * FOX_WIDTH
    c1 = c0 + FOX_HEADS
    c2 = c1 + 2 * SGU_WIDTH
    w_b = jnp.concatenate([w_in[:, :, c1:c2], w_in[:, :, c2:], w_in[:, :, c0:c1],
                           jnp.zeros((depth, D_MODEL, B_COLS - (2 * SGU_WIDTH + MEM_WIDTH + FOX_HEADS)), F32)],
                          axis=2)
    group_dim = SGU_WIDTH // SGU_GROUPS
    ws_prompt = jnp.concatenate([w_spatial[:, g] for g in range(SGU_GROUPS)], axis=2)
    bt_prompt = jnp.repeat(jnp.swapaxes(b_spatial, 1, 2), group_dim, axis=2)
    reps = GMLP_CHUNK // dec_seq
    blk = (jnp.arange(GMLP_CHUNK)[:, None] // dec_seq) == (jnp.arange(GMLP_CHUNK)[None, :] // dec_seq)
    ws_sample = jnp.concatenate(
        [jnp.where(blk, jnp.tile(w_spatial[:, g, :dec_seq, :dec_seq], (1, reps, reps)), 0.0)
         for g in range(SGU_GROUPS)], axis=2)
    bt_sample = jnp.tile(jnp.repeat(jnp.swapaxes(b_spatial[:, :, :dec_seq], 1, 2), group_dim, axis=2),
                         (1, reps, 1))
    return {
        "g_pre_mix": row(g_pre_mix), "w_qkv": w_in[:, :, :c0].astype(BF16), "w_b": w_b.astype(BF16),
        "b_forget": jnp.pad(row(b_forget), ((0, 0), (0, 0), (0, LANES - FOX_HEADS))), "g_sgu": row(g_sgu),
        "g_fox": row(g_group_out[:, :FOX_WIDTH]), "g_rest": row(g_group_out[:, FOX_WIDTH:]),
        "w_out": w_out.astype(BF16), "g_post_mix": row(g_post_mix), "g_pre_ffn": row(g_pre_ffn),
        "w_up": w_up.astype(BF16), "w_dwconv": w_dwconv, "b_dwconv": row(b_dwconv),
        "w_down": w_down.astype(BF16), "g_post_ffn": row(g_post_ffn),
        "ws_prompt": ws_prompt, "bt_prompt": bt_prompt, "ws_sample": ws_sample, "bt_sample": bt_sample,
    }


def kernel(x_prompt, x_sample, mem_prompt, cache_fox_k, cache_fox_v, cache_fox_logf, cache_mem_k, cache_mem_v,
           cache_ffn_conv, g_pre_mix, w_in, b_forget, w_spatial, b_spatial, g_sgu, g_mem, w_mem_kv, g_group_out,
           w_out, g_post_mix, g_pre_ffn, w_up, w_dwconv, b_dwconv, w_down, g_post_ffn):
    depth = w_in.shape[0]
    batch, seq, _ = x_prompt.shape
    dec_batch, dec_seq, _ = x_sample.shape
    past = cache_fox_k.shape[2]
    dec_rows = dec_batch * dec_seq

    mem_k_all, mem_v_all = _memory_kv(mem_prompt, g_mem, w_mem_kv.astype(BF16))
    feature_major = lambda a: jnp.transpose(a, (0, 1, 3, 4, 2)).reshape(a.shape[0], a.shape[1], -1, a.shape[2])
    token_major = lambda a, heads: jnp.transpose(
        a.reshape(a.shape[0], a.shape[1], heads, HEAD_DIM, a.shape[3]), (0, 1, 4, 2, 3))
    flat_mem = lambda a: a.reshape(-1, MEM_WIDTH, N_MEM)
    pmk, pmv = flat_mem(mem_k_all), flat_mem(mem_v_all)
    smk, smv = flat_mem(feature_major(cache_mem_k)), flat_mem(feature_major(cache_mem_v))
    hk = feature_major(cache_fox_k)
    hv = feature_major(cache_fox_v)

    lw = _prepare_weights(g_pre_mix, w_in, b_forget, w_spatial, b_spatial, g_sgu, g_group_out, w_out, g_post_mix,
                          g_pre_ffn, w_up, w_dwconv, b_dwconv, w_down, g_post_ffn, dec_seq)
    hist_rows = _lane_cumsum(jnp.swapaxes(cache_fox_logf, 2, 3).reshape(depth * dec_batch * FOX_HEADS, past),
                             minus_total=True).reshape(depth, dec_batch, FOX_HEADS, past)
    pad_rows = lambda a: jnp.pad(a, ((0, 0), (0, 0), (0, dec_seq - a.shape[2]), (0, 0))
                                 ).reshape(depth, dec_rows, FFN_DIM)
    e1_all = pad_rows(cache_ffn_conv[:, :, 1:2, :])
    e2_all = pad_rows(cache_ffn_conv)
    zeros_hist = jnp.zeros((batch, CONV_WIDTH - 1, FFN_DIM), F32)

    yp = x_prompt
    ys = x_sample.reshape(1, dec_rows, D_MODEL)
    outs = {name: [] for name in ("logf_p", "conv_p", "k_s", "v_s", "logf_s", "gv_s", "conv_s")}
    stacked_kv = ()
    per_b = lambda a: a.reshape(dec_batch, dec_seq, a.shape[-1])
    for l in range(depth):
        qt, k_all, v_all, kb, vt, logf, c_row, rest = _in_proj(yp, pmk, pmv, l * batch, 1, lw, IN_PROJ_ROWS,
                                                               lw["ws_prompt"], lw["bt_prompt"], True,
                                                               layer=l, depth=depth, stacked_kv=stacked_kv)
        stacked_kv = (k_all, v_all)
        yp, h2 = _fox_prompt(qt, kb, vt, c_row, rest, yp, lw, l)
        yp, tail = _ffn_prompt(yp, h2, zeros_hist, lw, l)
        outs["logf_p"].append(logf)
        outs["conv_p"].append(tail[:, -1, SUBLANES - (CONV_WIDTH - 1):, :])

        q, k, v, kb, vb, logf, rest, vrows = _in_proj(ys, smk, smv, l * dec_batch, dec_batch, lw, dec_rows,
                                                      lw["ws_sample"], lw["bt_sample"], False,
                                                      layer=l, depth=depth)
        logf_new_row = jnp.swapaxes(per_b(logf), 1, 2).reshape(dec_batch * FOX_HEADS, dec_seq)
        new_row = _lane_cumsum(jnp.pad(logf_new_row, ((0, 0), (0, LANES - dec_seq)))
                               ).reshape(dec_batch, FOX_HEADS, LANES)
        new_col = new_row[:, :, :dec_seq].reshape(dec_batch, FOX_HEADS * dec_seq, 1)
        ys2, h2s = _fox_sample(per_b(q[0]), per_b(kb[0]), per_b(vb[0]), hk, hv, l, hist_rows[l], new_col, new_row,
                               rest[0], ys[0], lw)
        ys_flat, a_all = _ffn_sample(ys2, h2s, e1_all[l], e2_all[l], lw, l, dec_seq)
        ys = ys_flat.reshape(1, dec_rows, D_MODEL)
        outs["k_s"].append(per_b(k[0]).reshape(dec_batch, dec_seq, FOX_HEADS, HEAD_DIM))
        outs["v_s"].append(per_b(v[0]).reshape(dec_batch, dec_seq, FOX_HEADS, HEAD_DIM))
        outs["logf_s"].append(per_b(logf[0]))
        outs["gv_s"].append(per_b(vrows[0]))
        outs["conv_s"].append(a_all.reshape(dec_batch, dec_seq, FFN_DIM)[:, dec_seq - (CONV_WIDTH - 1):, :])

    st = {name: jnp.stack(vals) for name, vals in outs.items()}
    return (yp, ys.reshape(dec_batch, dec_seq, D_MODEL),
            token_major(stacked_kv[0], FOX_HEADS), token_major(stacked_kv[1], FOX_HEADS),
            jnp.swapaxes(st["logf_p"], 2, 3),
            token_major(mem_k_all, MEM_HEADS), token_major(mem_v_all, MEM_HEADS),
            st["conv_p"], st["k_s"], st["v_s"], st["logf_s"], st["gv_s"], st["conv_s"])
```
